```python
import jax
import jax.numpy as jnp
from jax import lax
import numpy as np

D_MODEL = 4096
BATCH = 1
SEQ = 16384
DEPTH = 2

GRID_W = 64
CTX_LEN = 256
N_MIXERS = 2
MIXER_HGRN = 0
MIXER_POOL = 1
N_HGRN_LAYERS = (DEPTH - MIXER_HGRN + N_MIXERS - 1) // N_MIXERS
N_POOL_LAYERS = (DEPTH - MIXER_POOL + N_MIXERS - 1) // N_MIXERS
HGRN_HEAD_DIM = 128
HGRN_HEADS = D_MODEL // HGRN_HEAD_DIM
HGRN_DIM = HGRN_HEADS * HGRN_HEAD_DIM
HGRN_CHUNK = 32
POOL_WINDOWS = (2, 4, 8, 16)
POOL_GROUPS = len(POOL_WINDOWS)
POOL_GC = D_MODEL // POOL_GROUPS
N_EXPERTS = 64
D_EXPERT = 256
TOP_K = 8
N_ROUTE_GROUPS = 8
TOPK_ROUTE_GROUPS = 4
ROUTED_SCALE = 2.5
EXPERT_BLOCK = 8
N_MOD = 6
EPS = 1e-6

kernel_name = 'hybrid_hgrn2_pool_moe_flow_block'


def _rmsnorm(x, g):
    xf = x.astype(jnp.float32)
    y = xf * lax.rsqrt(jnp.mean(xf * xf, axis=-1, keepdims=True) + EPS)
    return (y * g.astype(jnp.float32)).astype(x.dtype)


def _modulate(h, shift, scale):
    return h * (1 + scale) + shift


def _ctx_read_after(i):
    return any(j % N_MIXERS == MIXER_HGRN for j in range(i + 1, DEPTH))


def _heads(a):
    b, l, _ = a.shape
    return a.astype(jnp.float32).reshape(b, l, HGRN_HEADS, HGRN_HEAD_DIM).transpose(0, 2, 1, 3)


def _forget(z, lb):
    f = lb + (1 - lb) * jax.nn.sigmoid(z)
    return jnp.log(f), 1 - f


def _chunk_scan(q, logf, k, v, s0):
    b_, h_, l_, _ = q.shape
    n = l_ // HGRN_CHUNK

    def to_chunks(a):
        return a.reshape(b_, h_, n, HGRN_CHUNK, a.shape[-1]).transpose(2, 0, 1, 3, 4)

    mask = jnp.tril(jnp.ones((HGRN_CHUNK, HGRN_CHUNK), dtype=bool))[:, :, None]

    def step(s, inp):
        qc, gc, kc, vc = inp
        bcum = jnp.cumsum(gc, axis=-2)
        blast = bcum[..., -1:, :]
        o_inter = jnp.einsum('bhtk,bhkv->bhtv', qc * jnp.exp(bcum), s)
        diff = bcum[..., :, None, :] - bcum[..., None, :, :]
        decay = jnp.exp(jnp.where(mask, diff, -jnp.inf))
        att = jnp.einsum('bhtk,bhsk,bhtsk->bhts', qc, kc, decay)
        o = o_inter + jnp.einsum('bhts,bhsv->bhtv', att, vc)
        s_new = s * jnp.exp(blast).swapaxes(-1, -2) + jnp.einsum('bhsk,bhsv->bhkv', kc * jnp.exp(blast - bcum), vc)
        return s_new, o

    s_fin, o = lax.scan(step, s0, (to_chunks(q), to_chunks(logf), to_chunks(k), to_chunks(v)))
    o = o.transpose(1, 2, 0, 3, 4).reshape(b_, h_, l_, v.shape[-1])
    return o, s_fin


def _final_state(logf, k, v):
    bcum = jnp.cumsum(logf, axis=-2)
    return jnp.einsum('bhsk,bhsv->bhkv', k * jnp.exp(bcum[..., -1:, :] - bcum), v)


def _hgrn2_out(o, g, norm_g, w_out):
    b_, h_, l_, dv = o.shape
    o = o * lax.rsqrt(jnp.mean(o * o, axis=-1, keepdims=True) + EPS)
    o = o * norm_g.astype(jnp.float32).reshape(h_, 1, dv)
    o = o.transpose(0, 2, 1, 3).reshape(b_, l_, h_ * dv) * jax.nn.silu(g.astype(jnp.float32))
    return o.astype(g.dtype) @ w_out


def _hgrn2_mixer(h, hc, w_in, lb, norm_g, w_out, ctx_out):
    lb_f = lb[0].reshape(HGRN_HEADS, 1, HGRN_HEAD_DIM)
    lb_b = lb[1].reshape(HGRN_HEADS, 1, HGRN_HEAD_DIM)
    flip = lambda a: jnp.flip(a, axis=2)

    q, g, i_in, zf, zb = jnp.split(h @ w_in, 5, axis=-1)
    q = jax.nn.silu(_heads(q))
    v = _heads(i_in)
    gf, kf = _forget(_heads(zf), lb_f)
    gb, kb = _forget(_heads(zb), lb_b)

    if ctx_out:
        qc, gcg, ic, zfc, zbc = jnp.split(hc @ w_in, 5, axis=-1)
    else:
        ic, zfc, zbc = jnp.split(hc @ w_in[:, 2 * HGRN_DIM:], 3, axis=-1)
    vc = _heads(ic)
    gfc, kfc = _forget(_heads(zfc), lb_f)
    gbc, kbc = _forget(_heads(zbc), lb_b)
    if ctx_out:
        qc = jax.nn.silu(_heads(qc))
        s0 = jnp.zeros(qc.shape[:2] + (HGRN_HEAD_DIM, HGRN_HEAD_DIM), jnp.float32)
        oc_f, s_f = _chunk_scan(qc, gfc, kfc, vc, s0)
        oc_b, s_b = _chunk_scan(flip(qc), flip(gbc), flip(kbc), flip(vc), s0)
        y_c = _hgrn2_out(oc_f + flip(oc_b), gcg, norm_g, w_out)
    else:
        s_f = _final_state(gfc, kfc, vc)
        s_b = _final_state(flip(gbc), flip(kbc), flip(vc))
        y_c = None

    o_f, _ = _chunk_scan(q, gf, kf, v, s_f)
    o_b, _ = _chunk_scan(flip(q), flip(gb), flip(kb), flip(v), s_b)
    y = _hgrn2_out(o_f + flip(o_b), g, norm_g, w_out)
    return y, y_c


def _box_mean(a, window, axis):
    n = a.shape[axis]
    t = np.arange(n)
    lo = np.maximum(t - window // 2, 0)
    hi = np.minimum(t + window - window // 2, n)
    pad = [(0, 0)] * a.ndim
    pad[axis] = (1, 0)
    cs = jnp.pad(jnp.cumsum(a, axis=axis), pad)
    total = jnp.take(cs, hi, axis=axis) - jnp.take(cs, lo, axis=axis)
    shape = [1] * a.ndim
    shape[axis] = n
    return total / jnp.asarray(hi - lo, a.dtype).reshape(shape)


def _pool_project(p, w_pool, scale):
    y = jnp.einsum('...gc,gcd->...gd', p, w_pool.astype(jnp.float32))
    return y.reshape(y.shape[:-2] + (D_MODEL,)) * scale.astype(jnp.float32)


def _pool_mixer_grid(h, w_pool, scale):
    b, l, d = h.shape
    rows = l // GRID_W
    hg = h.astype(jnp.float32).reshape(b, rows, GRID_W, POOL_GROUPS, POOL_GC)
    means = jnp.stack([_box_mean(_box_mean(hg[:, :, :, g], w, 1), w, 2)
                       for g, w in enumerate(POOL_WINDOWS)], axis=3)
    return _pool_project(means - hg, w_pool, scale).reshape(b, l, d).astype(h.dtype)


def _pool_mixer_seq(hc, w_pool, scale):
    b, n, d = hc.shape
    hs = hc.astype(jnp.float32).reshape(b, n, POOL_GROUPS, POOL_GC)
    means = jnp.stack([_box_mean(hs[:, :, g], w, 1) for g, w in enumerate(POOL_WINDOWS)], axis=2)
    return _pool_project(means - hs, w_pool, scale).astype(hc.dtype)


def _swiglu(t, w_gu, w_down):
    a, u = jnp.split(t @ w_gu, 2, axis=-1)
    return (jax.nn.silu(a) * u) @ w_down


def _moe(h, w_r, b_r, w_gu, w_down, s_gu, s_down):
    b, l, d = h.shape
    t = h.reshape(b * l, d)
    scores = jax.nn.sigmoid((t @ w_r).astype(jnp.float32))
    sel = scores + b_r.astype(jnp.float32)
    epg = N_EXPERTS // N_ROUTE_GROUPS
    grp_score = lax.top_k(sel.reshape(-1, N_ROUTE_GROUPS, epg), 2)[0].sum(-1)
    _, gidx = lax.top_k(grp_score, TOPK_ROUTE_GROUPS)
    gmask = jax.nn.one_hot(gidx, N_ROUTE_GROUPS).sum(1) > 0
    sel = jnp.where(jnp.repeat(gmask, epg, axis=1), sel, -jnp.inf)
    _, eidx = lax.top_k(sel, TOP_K)
    wts = jnp.take_along_axis(scores, eidx, axis=1)
    wts = wts / jnp.sum(wts, axis=-1, keepdims=True) * ROUTED_SCALE
    combine = jnp.einsum('tk,tke->te', wts, jax.nn.one_hot(eidx, N_EXPERTS, dtype=jnp.float32))
    out = _swiglu(t, s_gu, s_down)
    for e0 in range(0, N_EXPERTS, EXPERT_BLOCK):
        sl = slice(e0, e0 + EXPERT_BLOCK)
        a, u = jnp.split(jnp.einsum('td,edf->tef', t, w_gu[sl]), 2, axis=-1)
        act = jax.nn.silu(a) * u * combine[:, sl, None].astype(t.dtype)
        out = out + jnp.einsum('tef,efd->td', act, w_down[sl])
    return out.reshape(b, l, d)


def setup_inputs(seed: int = 0) -> dict:
    key = jax.random.key(seed)
    ks = jax.random.split(key, 19)

    def nrm(k, shape, scale):
        return jax.random.normal(k, shape, jnp.float32) * scale

    return {
        'x': nrm(ks[0], (BATCH, SEQ, D_MODEL), 1.0),
        'c': nrm(ks[1], (BATCH, D_MODEL), 1.0),
        'ctx': nrm(ks[2], (BATCH, CTX_LEN, D_MODEL), 1.0),
        'c_ctx': nrm(ks[3], (D_MODEL,), 1.0),
        'w_mod': nrm(ks[4], (DEPTH, D_MODEL, N_MOD * D_MODEL), 0.5 * D_MODEL ** -0.5),
        'b_mod': nrm(ks[5], (DEPTH, N_MOD * D_MODEL), 0.01),
        'norm_g': 1.0 + nrm(ks[6], (DEPTH, 4, D_MODEL), 0.05),
        'hgrn_w_in': nrm(ks[7], (N_HGRN_LAYERS, D_MODEL, 5 * HGRN_DIM), D_MODEL ** -0.5),
        'hgrn_lb_logits': nrm(ks[8], (2, DEPTH + 1, HGRN_DIM), 0.5),
        'hgrn_norm_g': 1.0 + nrm(ks[9], (N_HGRN_LAYERS, HGRN_DIM), 0.05),
        'hgrn_w_out': nrm(ks[10], (N_HGRN_LAYERS, HGRN_DIM, D_MODEL), HGRN_DIM ** -0.5),
        'pool_w': nrm(ks[11], (N_POOL_LAYERS, POOL_GROUPS, POOL_GC, POOL_GC), POOL_GC ** -0.5),
        'pool_scale': 1.0 + nrm(ks[12], (N_POOL_LAYERS, D_MODEL), 0.1),
        'router_w': nrm(ks[13], (DEPTH, D_MODEL, N_EXPERTS), D_MODEL ** -0.5),
        'router_bias': nrm(ks[14], (DEPTH, N_EXPERTS), 0.01),
        'exp_w_gu': nrm(ks[15], (DEPTH, N_EXPERTS, D_MODEL, 2 * D_EXPERT), D_MODEL ** -0.5),
        'exp_w_down': nrm(ks[16], (DEPTH, N_EXPERTS, D_EXPERT, D_MODEL), D_EXPERT ** -0.5),
        'shared_w_gu': nrm(ks[17], (DEPTH, D_MODEL, 2 * D_EXPERT), D_MODEL ** -0.5),
        'shared_w_down': nrm(ks[18], (DEPTH, D_EXPERT, D_MODEL), D_EXPERT ** -0.5),
    }


def reference(x, c, ctx, c_ctx, w_mod, b_mod, norm_g, hgrn_w_in, hgrn_lb_logits, hgrn_norm_g, hgrn_w_out,
              pool_w, pool_scale, router_w, router_bias, exp_w_gu, exp_w_down, shared_w_gu, shared_w_down):
    lb_all = jnp.cumsum(jax.nn.softmax(hgrn_lb_logits.astype(jnp.float32), axis=1), axis=1)
    sc = jax.nn.silu(c)
    scc = jax.nn.silu(c_ctx)
    h_ctx = ctx
    for i in range(DEPTH):
        mixer = i % N_MIXERS
        j = i // N_MIXERS
        ctx_out = _ctx_read_after(i)
        ctx_live = ctx_out or mixer == MIXER_HGRN

        sh1, sc1, g1, sh2, sc2, g2 = jnp.split((sc @ w_mod[i] + b_mod[i])[:, None, :], N_MOD, axis=-1)
        hx = _modulate(_rmsnorm(x, norm_g[i, 0]), sh1, sc1)
        hc = None
        mods_c = None
        if ctx_live:
            n_mod = N_MOD if ctx_out else 2
            mods_c = jnp.split(scc @ w_mod[i][:, :n_mod * D_MODEL] + b_mod[i][:n_mod * D_MODEL], n_mod)
            hc = _modulate(_rmsnorm(h_ctx, norm_g[i, 0]), mods_c[0], mods_c[1])

        if mixer == MIXER_HGRN:
            y, y_c = _hgrn2_mixer(hx, hc, hgrn_w_in[j], lb_all[:, i], hgrn_norm_g[j], hgrn_w_out[j], ctx_out)
        else:
            y = _pool_mixer_grid(hx, pool_w[j], pool_scale[j])
            y_c = _pool_mixer_seq(hc, pool_w[j], pool_scale[j]) if ctx_out else None

        x = x + g1 * _rmsnorm(y, norm_g[i, 1])
        f = _moe(_modulate(_rmsnorm(x, norm_g[i, 2]), sh2, sc2), router_w[i], router_bias[i],
                 exp_w_gu[i], exp_w_down[i], shared_w_gu[i], shared_w_down[i])
        x = x + g2 * _rmsnorm(f, norm_g[i, 3])

        if ctx_out:
            gc1, shc2, scc2, gc2 = mods_c[2], mods_c[3], mods_c[4], mods_c[5]
            h_ctx = h_ctx + gc1 * _rmsnorm(y_c, norm_g[i, 1])
            fc = _moe(_modulate(_rmsnorm(h_ctx, norm_g[i, 2]), shc2, scc2), router_w[i], router_bias[i],
                      exp_w_gu[i], exp_w_down[i], shared_w_gu[i], shared_w_down[i])
            h_ctx = h_ctx + gc2 * _rmsnorm(fc, norm_g[i, 3])
    return x
```

```python
import functools

import jax
import jax.numpy as jnp
from jax import lax
from jax.experimental import pallas as pl
from jax.experimental.pallas import tpu as pltpu

EPS = 1e-6
BF = jnp.bfloat16
F32 = jnp.float32
NEG_INF = float("-inf")

LANES = 128
HEAD_DIM = 128
SCAN_CHUNK = 8
GRID_W = 64
POOL_WINDOWS = (2, 4, 8, 16)
N_ROUTE_GROUPS = 8
TOPK_ROUTE_GROUPS = 4
TOP_K = 8
ROUTED_SCALE = 2.5
N_MOD = 6
VMEM_LIMIT = 56 * 1024 * 1024


def _cparams(*sem):
    return pltpu.CompilerParams(dimension_semantics=sem, vmem_limit_bytes=VMEM_LIMIT)


def _silu(v):
    return v * jax.nn.sigmoid(v)


def _rms(v):
    return v * lax.rsqrt(jnp.mean(v * v, axis=-1, keepdims=True) + EPS)


def _split_bf16(v):
    hi = v.astype(BF)
    lo = (v - hi.astype(F32)).astype(BF)
    return hi, lo


def _pick(n, want):
    t = min(n, want)
    while n % t:
        t //= 2
    assert t >= 1
    return t


def _mods_kernel(s_ref, w_ref, b_ref, o_ref):
    s = _silu(s_ref[...])
    o_ref[0] = jnp.dot(s.astype(BF), w_ref[0].astype(BF), preferred_element_type=F32) + b_ref[0]


def _mods(cvecs, w_mod, b_mod):
    depth, d, n = w_mod.shape
    tn = _pick(n, 512)
    return pl.pallas_call(
        _mods_kernel,
        grid=(depth, n // tn),
        in_specs=[
            pl.BlockSpec((8, d), lambda l, j: (0, 0)),
            pl.BlockSpec((1, d, tn), lambda l, j: (l, 0, j)),
            pl.BlockSpec((1, 1, tn), lambda l, j: (l, 0, j)),
        ],
        out_specs=pl.BlockSpec((1, 8, tn), lambda l, j: (l, 0, j)),
        out_shape=jax.ShapeDtypeStruct((depth, 8, n), F32),
        compiler_params=_cparams("parallel", "parallel"),
        name="mods",
    )(cvecs, w_mod, b_mod.reshape(depth, 1, n))


def _normmod_kernel(x_ref, g_ref, sh_ref, sc_ref, o_ref):
    y = _rms(x_ref[...]) * g_ref[...]
    o_ref[...] = (y * (1 + sc_ref[...]) + sh_ref[...]).astype(o_ref.dtype)


def _normmod(x, g, shift, scale):
    m, d = x.shape
    tm = _pick(m, 256)
    vec = pl.BlockSpec((1, d), lambda i: (0, 0))
    return pl.pallas_call(
        _normmod_kernel,
        grid=(m // tm,),
        in_specs=[pl.BlockSpec((tm, d), lambda i: (i, 0)), vec, vec, vec],
        out_specs=pl.BlockSpec((tm, d), lambda i: (i, 0)),
        out_shape=jax.ShapeDtypeStruct((m, d), BF),
        compiler_params=_cparams("parallel"),
        name="normmod",
    )(x, g.reshape(1, d), shift.reshape(1, d), scale.reshape(1, d))


def _resid_kernel(x_ref, y_ref, gate_ref, gy_ref, xo_ref):
    yn = _rms(y_ref[...].astype(F32)) * gy_ref[...]
    xo_ref[...] = x_ref[...] + gate_ref[...] * yn


def _resid_norm_kernel(x_ref, y_ref, gate_ref, gy_ref, gx_ref, sh_ref, sc_ref, xo_ref, ho_ref):
    yn = _rms(y_ref[...].astype(F32)) * gy_ref[...]
    xn = x_ref[...] + gate_ref[...] * yn
    xo_ref[...] = xn
    hn = _rms(xn) * gx_ref[...]
    ho_ref[...] = (hn * (1 + sc_ref[...]) + sh_ref[...]).astype(ho_ref.dtype)


def _resid(x, y, gate, gy, nxt=None):
    m, d = x.shape
    tm = _pick(m, 256)
    row = pl.BlockSpec((tm, d), lambda i: (i, 0))
    vec = pl.BlockSpec((1, d), lambda i: (0, 0))
    v = lambda a: a.reshape(1, d)
    if nxt is None:
        return pl.pallas_call(
            _resid_kernel,
            grid=(m // tm,),
            in_specs=[row, row, vec, vec],
            out_specs=row,
            out_shape=jax.ShapeDtypeStruct((m, d), F32),
            compiler_params=_cparams("parallel"),
            name="resid",
        )(x, y, v(gate), v(gy))
    gx, shift, scale = nxt
    return pl.pallas_call(
        _resid_norm_kernel,
        grid=(m // tm,),
        in_specs=[row, row, vec, vec, vec, vec, vec],
        out_specs=[row, row],
        out_shape=[jax.ShapeDtypeStruct((m, d), F32), jax.ShapeDtypeStruct((m, d), BF)],
        compiler_params=_cparams("parallel"),
        name="resid_norm",
    )(x, y, v(gate), v(gy), v(gx), v(shift), v(scale))


def _mm_kernel(a_ref, w_ref, o_ref):
    o_ref[...] = jnp.dot(a_ref[...], w_ref[...], preferred_element_type=F32).astype(o_ref.dtype)


def _mm(a, w, out_dtype):
    m, k = a.shape
    n = w.shape[1]
    tm, tn = _pick(m, 1024), _pick(n, 1024)
    return pl.pallas_call(
        _mm_kernel,
        grid=(m // tm, n // tn),
        in_specs=[pl.BlockSpec((tm, k), lambda i, j: (i, 0)), pl.BlockSpec((k, tn), lambda i, j: (0, j))],
        out_specs=pl.BlockSpec((tm, tn), lambda i, j: (i, j)),
        out_shape=jax.ShapeDtypeStruct((m, n), out_dtype),
        compiler_params=_cparams("parallel", "parallel"),
        name="mm",
    )(a, w)


def _mm_heads_kernel(a_ref, w_ref, o_ref):
    r = jnp.dot(a_ref[...], w_ref[...], preferred_element_type=F32)
    for hh in range(o_ref.shape[0]):
        o_ref[hh] = r[:, hh * LANES:(hh + 1) * LANES].astype(o_ref.dtype)


def _mm_heads(a, w):
    m, k = a.shape
    n = w.shape[1]
    tm, tn = _pick(m, 1024), _pick(n, 1024)
    return pl.pallas_call(
        _mm_heads_kernel,
        grid=(m // tm, n // tn),
        in_specs=[pl.BlockSpec((tm, k), lambda i, j: (i, 0)), pl.BlockSpec((k, tn), lambda i, j: (0, j))],
        out_specs=pl.BlockSpec((tn // LANES, tm, LANES), lambda i, j: (j, i, 0)),
        out_shape=jax.ShapeDtypeStruct((n // LANES, m, LANES), BF),
        compiler_params=_cparams("parallel", "parallel"),
        name="proj",
    )(a, w)


def _scan_kernel(q_ref, v_ref, z_ref, lbl_ref, s0_ref, o_ref, sfin_ref,
                 st_ref, b_ref, qs_ref, ks_ref, vs_ref, os_ref, *, layer, reverse):
    c = SCAN_CHUNK
    t_tile = q_ref.shape[1]
    n_chunks = t_tile // c
    i = pl.program_id(1)

    @pl.when(i == 0)
    def _():
        st_ref[...] = s0_ref[0]

    lg = lbl_ref[0]
    e = jnp.exp(lg - jnp.max(lg, axis=0, keepdims=True))
    p = e / jnp.sum(e, axis=0, keepdims=True)
    lb = jnp.sum(p[:layer + 1], axis=0, keepdims=True)

    z = z_ref[0].astype(F32)
    f = lb + (1 - lb) * jax.nn.sigmoid(z)
    logf = jnp.log(f)
    ks_ref[...] = 1 - f
    qs_ref[...] = _silu(q_ref[0].astype(F32))
    vs_ref[...] = v_ref[0].astype(F32)

    r_io = lax.broadcasted_iota(jnp.int32, (t_tile, t_tile), 0)
    c_io = lax.broadcasted_iota(jnp.int32, (t_tile, t_tile), 1)
    shift = c.bit_length() - 1
    assert c == 1 << shift
    same = lax.shift_right_logical(r_io, shift) == lax.shift_right_logical(c_io, shift)
    tri = jnp.where(same & ((c_io >= r_io) if reverse else (c_io <= r_io)), 1.0, 0.0).astype(BF)
    hi, lo = _split_bf16(logf)
    b_ref[...] = (jnp.dot(tri, hi, preferred_element_type=F32)
                  + jnp.dot(tri, lo, preferred_element_type=F32))

    row = lax.broadcasted_iota(jnp.int32, (c, HEAD_DIM), 0)
    ones = jnp.ones((HEAD_DIM, HEAD_DIM), BF)
    last = 0 if reverse else c - 1

    def step(n, carry):
        ci = (n_chunks - 1 - n) if reverse else n
        sl = pl.ds(pl.multiple_of(ci * c, c), c)
        bc, qc, kc, vc = b_ref[sl, :], qs_ref[sl, :], ks_ref[sl, :], vs_ref[sl, :]
        blast = bc[last:last + 1]
        st = st_ref[...]
        qd = qc * jnp.exp(bc)
        o_inter = lax.dot_general(qd.astype(BF), st.astype(BF), (((1,), (1,)), ((), ())),
                                  preferred_element_type=F32)
        ws = []
        for s in range(c):
            valid = (row <= s) if reverse else (row >= s)
            dec = jnp.exp(jnp.where(valid, bc - bc[s:s + 1], NEG_INF))
            ws.append(qc * kc[s:s + 1] * dec)
        att = jnp.dot(jnp.concatenate(ws, axis=0).astype(BF), ones, preferred_element_type=F32)
        o = o_inter
        for s in range(c):
            o = o + att[s * c:(s + 1) * c] * vc[s:s + 1]
        os_ref[sl, :] = o
        kd = kc * jnp.exp(blast - bc)
        upd = lax.dot_general(vc.astype(BF), kd.astype(BF), (((0,), (0,)), ((), ())),
                              preferred_element_type=F32)
        st_ref[...] = st * jnp.exp(blast) + upd
        return carry

    lax.fori_loop(0, n_chunks, step, 0)
    o_ref[0] = os_ref[...].astype(o_ref.dtype)

    @pl.when(i == pl.num_programs(1) - 1)
    def _():
        sfin_ref[0] = st_ref[...]


def _scan(p, lb_logits, s0, *, n_heads, layer, reverse, z_off):
    _, l, _ = p.shape
    t_tile = _pick(l, 256)
    nt = l // t_tile
    tok = (lambda i: nt - 1 - i) if reverse else (lambda i: i)
    blk = lambda off: pl.BlockSpec((1, t_tile, HEAD_DIM), lambda h, i: (off + h, tok(i), 0))
    n_lb = lb_logits.shape[1]
    return pl.pallas_call(
        functools.partial(_scan_kernel, layer=layer, reverse=reverse),
        grid=(n_heads, nt),
        in_specs=[
            blk(0), blk(2 * n_heads), blk(z_off),
            pl.BlockSpec((1, n_lb, HEAD_DIM), lambda h, i: (h, 0, 0)),
            pl.BlockSpec((1, HEAD_DIM, HEAD_DIM), lambda h, i: (h, 0, 0)),
        ],
        out_specs=[
            pl.BlockSpec((1, t_tile, HEAD_DIM), lambda h, i: (h, tok(i), 0)),
            pl.BlockSpec((1, HEAD_DIM, HEAD_DIM), lambda h, i: (h, 0, 0)),
        ],
        out_shape=[
            jax.ShapeDtypeStruct((n_heads, l, HEAD_DIM), BF),
            jax.ShapeDtypeStruct((n_heads, HEAD_DIM, HEAD_DIM), F32),
        ],
        scratch_shapes=[pltpu.VMEM((HEAD_DIM, HEAD_DIM), F32)]
        + [pltpu.VMEM((t_tile, HEAD_DIM), F32) for _ in range(5)],
        compiler_params=_cparams("parallel", "arbitrary"),
        name="scan_bwd" if reverse else "scan_fwd",
    )(p, p, p, lb_logits, s0)


def _gate_kernel(of_ref, ob_ref, g_ref, ng_ref, a_ref):
    for h in range(of_ref.shape[0]):
        o = of_ref[h].astype(F32) + ob_ref[h].astype(F32)
        o = _rms(o) * ng_ref[h]
        a_ref[:, h * HEAD_DIM:(h + 1) * HEAD_DIM] = (o * _silu(g_ref[h].astype(F32))).astype(a_ref.dtype)


def _gate(o_f, o_b, p, norm_g):
    n_heads, l, _ = o_f.shape
    tm = _pick(l, 256)
    blk = lambda off: pl.BlockSpec((n_heads, tm, HEAD_DIM), lambda i: (off, i, 0))
    return pl.pallas_call(
        _gate_kernel,
        grid=(l // tm,),
        in_specs=[blk(0), blk(0), blk(1), pl.BlockSpec((n_heads, 1, HEAD_DIM), lambda i: (0, 0, 0))],
        out_specs=pl.BlockSpec((tm, n_heads * HEAD_DIM), lambda i: (i, 0)),
        out_shape=jax.ShapeDtypeStruct((l, n_heads * HEAD_DIM), BF),
        compiler_params=_cparams("parallel"),
        name="gate",
    )(o_f, o_b, p, norm_g.reshape(n_heads, 1, HEAD_DIM))


def _pool_body(win, prev_ref, cur_ref, next_ref, w_ref, scale_ref, o_ref, y_ref, d_ref, n_rows):
    i = pl.program_id(0)
    tile_rows = cur_ref.shape[0] // GRID_W
    halo = y_ref.shape[0] - tile_rows
    top = halo // 2
    half = win // 2
    r0 = i * tile_rows

    t_io = lax.broadcasted_iota(jnp.int32, (GRID_W, GRID_W), 0)
    s_io = lax.broadcasted_iota(jnp.int32, (GRID_W, GRID_W), 1)
    lo = jnp.maximum(t_io - half, 0)
    hi = jnp.minimum(t_io + win - half, GRID_W)
    a_c = jnp.where((s_io >= lo) & (s_io < hi), 1.0 / (hi - lo).astype(F32), 0.0)
    a_hi, a_lo = _split_bf16(a_c)

    def col_filter(src_ref, src_row, dst_row):
        xs = src_ref[pl.ds(pl.multiple_of(src_row * GRID_W, GRID_W), GRID_W), :]
        y = (jnp.dot(a_hi, xs, preferred_element_type=F32) + jnp.dot(a_lo, xs, preferred_element_type=F32))
        grow = jnp.full((1, y.shape[1]), r0 - top + dst_row, jnp.int32)
        ok = (grow >= 0) & (grow < n_rows)
        y_ref[dst_row] = jnp.where(ok, y, 0.0)

    def above(j, c):
        col_filter(prev_ref, tile_rows - half + j, top - half + j)
        return c

    def inside(j, c):
        col_filter(cur_ref, j, top + j)
        return c

    def below(j, c):
        col_filter(next_ref, j, top + tile_rows + j)
        return c

    lax.fori_loop(0, half, above, 0)
    lax.fori_loop(0, tile_rows, inside, 0)
    lax.fori_loop(0, half - 1, below, 0)

    def row_filter(r, c):
        grow = jnp.full((1, d_ref.shape[1]), r0 + r, jnp.int32)
        cnt = jnp.minimum(grow + win - half, n_rows) - jnp.maximum(grow - half, 0)
        z = y_ref[top + r - half]
        for j in range(1, win):
            z = z + y_ref[top + r - half + j]
        z = z * (1.0 / cnt.astype(F32))
        sl = pl.ds(pl.multiple_of(r * GRID_W, GRID_W), GRID_W)
        d_ref[sl, :] = (z - cur_ref[sl, :].astype(F32)).astype(d_ref.dtype)
        return c

    lax.fori_loop(0, tile_rows, row_filter, 0)
    o_ref[...] = jnp.dot(d_ref[...], w_ref[0], preferred_element_type=F32) * scale_ref[...]


def _pool_kernel(prev_ref, cur_ref, next_ref, w_ref, scale_ref, o_ref, y_ref, d_ref, *, n_rows):
    g = pl.program_id(1)
    for gi, win in enumerate(POOL_WINDOWS):
        @pl.when(g == gi)
        def _(win=win):
            _pool_body(win, prev_ref, cur_ref, next_ref, w_ref, scale_ref, o_ref, y_ref, d_ref, n_rows)


def _pool(h, w_pool, scale):
    l, d = h.shape
    n_groups, gc, _ = w_pool.shape
    n_rows = l // GRID_W
    tile_rows = _pick(n_rows, 16)
    halo = max(POOL_WINDOWS)
    assert tile_rows >= halo // 2 and len(POOL_WINDOWS) == n_groups
    tt = tile_rows * GRID_W
    nt = l // tt
    return pl.pallas_call(
        functools.partial(_pool_kernel, n_rows=n_rows),
        grid=(nt, n_groups),
        in_specs=[
            pl.BlockSpec((tt, gc), lambda i, g: (jnp.maximum(i - 1, 0), g)),
            pl.BlockSpec((tt, gc), lambda i, g: (i, g)),
            pl.BlockSpec((tt, gc), lambda i, g: (jnp.minimum(i + 1, nt - 1), g)),
            pl.BlockSpec((1, gc, gc), lambda i, g: (g, 0, 0)),
            pl.BlockSpec((1, gc), lambda i, g: (0, g)),
        ],
        out_specs=pl.BlockSpec((tt, gc), lambda i, g: (i, g)),
        out_shape=jax.ShapeDtypeStruct((l, d), F32),
        scratch_shapes=[pltpu.VMEM((tile_rows + halo, GRID_W, gc), F32), pltpu.VMEM((tt, gc), BF)],
        compiler_params=_cparams("parallel", "parallel"),
        name="pool",
    )(h, h, h, w_pool, scale.reshape(1, d))


def _router_kernel(h_ref, wrt_ref, bias_ref, o_ref):
    n_exp = wrt_ref.shape[0]
    tm = h_ref.shape[0]
    epg = n_exp // N_ROUTE_GROUPS
    logits = lax.dot_general(wrt_ref[...], h_ref[...], (((1,), (1,)), ((), ())), preferred_element_type=F32)
    scores = jax.nn.sigmoid(logits)
    shape3 = (N_ROUTE_GROUPS, epg, tm)
    sc3 = scores.reshape(shape3)
    s3 = (scores + bias_ref[...]).reshape(shape3)
    e_in = lax.broadcasted_iota(jnp.int32, shape3, 1).astype(F32)
    g_io = lax.broadcasted_iota(jnp.int32, shape3, 0).astype(F32)

    m1 = jnp.max(s3, axis=1, keepdims=True)
    i1 = jnp.min(jnp.where(s3 == m1, e_in, epg), axis=1, keepdims=True)
    m2 = jnp.max(jnp.where(e_in == i1, NEG_INF, s3), axis=1, keepdims=True)
    gs = jnp.broadcast_to(m1 + m2, shape3)

    keep = jnp.zeros(shape3, F32)
    cur = gs
    for _ in range(TOPK_ROUTE_GROUPS):
        m = jnp.max(cur, axis=0, keepdims=True)
        idx = jnp.min(jnp.where(cur == m, g_io, N_ROUTE_GROUPS), axis=0, keepdims=True)
        hit = g_io == idx
        keep = jnp.where(hit, 1.0, keep)
        cur = jnp.where(hit, NEG_INF, cur)

    e_io = g_io * epg + e_in
    chosen = jnp.zeros(shape3, F32)
    cur = jnp.where(keep > 0, s3, NEG_INF)
    for _ in range(TOP_K):
        m = jnp.max(jnp.max(cur, axis=0, keepdims=True), axis=1, keepdims=True)
        idx = jnp.where(cur == m, e_io, n_exp)
        idx = jnp.min(jnp.min(idx, axis=0, keepdims=True), axis=1, keepdims=True)
        hit = e_io == idx
        chosen = jnp.where(hit, 1.0, chosen)
        cur = jnp.where(hit, NEG_INF, cur)

    w = jnp.where(chosen > 0, sc3, 0.0)
    den = jnp.sum(jnp.sum(w, axis=0, keepdims=True), axis=1, keepdims=True)
    o_ref[...] = (w / den * ROUTED_SCALE).reshape(n_exp, tm)


def _router(h, w_r, b_r):
    t, d = h.shape
    n_exp = w_r.shape[1]
    tm = _pick(t, 512)
    return pl.pallas_call(
        _router_kernel,
        grid=(t // tm,),
        in_specs=[
            pl.BlockSpec((tm, d), lambda i: (i, 0)),
            pl.BlockSpec((n_exp, d), lambda i: (0, 0)),
            pl.BlockSpec((n_exp, 1), lambda i: (0, 0)),
        ],
        out_specs=pl.BlockSpec((n_exp, tm), lambda i: (0, i)),
        out_shape=jax.ShapeDtypeStruct((n_exp, t), F32),
        compiler_params=_cparams("parallel"),
        name="router",
    )(h, w_r.T.astype(BF), b_r.astype(F32).reshape(n_exp, 1))


def _moe_kernel(h_ref, comb_ref, wgu_ref, wd_ref, o_ref):
    e = pl.program_id(1)
    d_exp = wd_ref.shape[1]

    @pl.when(e == 0)
    def _():
        o_ref[...] = jnp.zeros_like(o_ref)

    gu = jnp.dot(h_ref[...], wgu_ref[0], preferred_element_type=F32)
    a, u = gu[:, :d_exp], gu[:, d_exp:]
    n_col = comb_ref.shape[1]
    onehot = jnp.where(lax.broadcasted_iota(jnp.int32, (n_col, d_exp), 0) == e, 1.0, 0.0).astype(BF)
    c_hi, c_lo = _split_bf16(comb_ref[...])
    cw = (jnp.dot(c_hi, onehot, preferred_element_type=F32) + jnp.dot(c_lo, onehot, preferred_element_type=F32))
    act = _silu(a) * u * cw
    o_ref[...] += jnp.dot(act.astype(BF), wd_ref[0], preferred_element_type=F32)


def _moe(h, comb, w_gu, w_down):
    t, d = h.shape
    n_e, _, f2 = w_gu.shape
    tm = _pick(t, 512)
    return pl.pallas_call(
        _moe_kernel,
        grid=(t // tm, n_e),
        in_specs=[
            pl.BlockSpec((tm, d), lambda i, e: (i, 0)),
            pl.BlockSpec((tm, comb.shape[1]), lambda i, e: (i, 0)),
            pl.BlockSpec((1, d, f2), lambda i, e: (e, 0, 0)),
            pl.BlockSpec((1, f2 // 2, d), lambda i, e: (e, 0, 0)),
        ],
        out_specs=pl.BlockSpec((tm, d), lambda i, e: (i, 0)),
        out_shape=jax.ShapeDtypeStruct((t, d), F32),
        compiler_params=_cparams("parallel", "arbitrary"),
        name="moe",
    )(h, comb, w_gu, w_down)


def _moe_layer(h, w_r, b_r, w_gu, w_down, s_gu, s_down):
    t = h.shape[0]
    n_exp = w_r.shape[1]
    comb_t = _router(h, w_r, b_r)
    comb = jnp.concatenate([comb_t.T, jnp.ones((t, 1), F32), jnp.zeros((t, LANES - n_exp - 1), F32)], axis=1)
    wgu_all = jnp.concatenate([w_gu, s_gu[None]], axis=0).astype(BF)
    wd_all = jnp.concatenate([w_down, s_down[None]], axis=0).astype(BF)
    return _moe(h, comb, wgu_all, wd_all)


def kernel(x, c, ctx, c_ctx, w_mod, b_mod, norm_g, hgrn_w_in, hgrn_lb_logits, hgrn_norm_g, hgrn_w_out,
           pool_w, pool_scale, router_w, router_bias, exp_w_gu, exp_w_down, shared_w_gu, shared_w_down):
    batch, seq, d = x.shape
    assert batch == 1 and c.shape[0] == 1
    depth = w_mod.shape[0]
    assert depth == 2 and w_mod.shape[2] == N_MOD * d
    n_heads = hgrn_w_in.shape[2] // 5 // HEAD_DIM
    x0 = x[0]

    cvecs = jnp.concatenate([c, c_ctx[None], jnp.zeros((6, d), F32)], axis=0)
    mods = _mods(cvecs, w_mod, b_mod)
    mod = lambda layer, who, j: mods[layer, who, j * d:(j + 1) * d]

    w_in = hgrn_w_in[0].astype(BF)
    hx = _normmod(x0, norm_g[0, 0], mod(0, 0, 0), mod(0, 0, 1))
    hc = _normmod(ctx[0], norm_g[0, 0], mod(0, 1, 0), mod(0, 1, 1))
    p = _mm_heads(hx, w_in)
    pc = _mm_heads(hc, w_in)
    n_lb = hgrn_lb_logits.shape[1]
    lbl = hgrn_lb_logits.astype(F32).reshape(2, n_lb, n_heads, HEAD_DIM).transpose(0, 2, 1, 3)
    zero_state = jnp.zeros((n_heads, HEAD_DIM, HEAD_DIM), F32)
    scan = functools.partial(_scan, n_heads=n_heads, layer=0)
    _, s_f = scan(pc, lbl[0], zero_state, reverse=False, z_off=3 * n_heads)
    _, s_b = scan(pc, lbl[1], zero_state, reverse=True, z_off=4 * n_heads)
    o_f, _ = scan(p, lbl[0], s_f, reverse=False, z_off=3 * n_heads)
    o_b, _ = scan(p, lbl[1], s_b, reverse=True, z_off=4 * n_heads)
    a = _gate(o_f, o_b, p, hgrn_norm_g[0])
    y = _mm(a, hgrn_w_out[0].astype(BF), F32)
    x1, h = _resid(x0, y, mod(0, 0, 2), norm_g[0, 1], (norm_g[0, 2], mod(0, 0, 3), mod(0, 0, 4)))
    f = _moe_layer(h, router_w[0], router_bias[0], exp_w_gu[0], exp_w_down[0], shared_w_gu[0], shared_w_down[0])
    x2, h = _resid(x1, f, mod(0, 0, 5), norm_g[0, 3], (norm_g[1, 0], mod(1, 0, 0), mod(1, 0, 1)))

    y = _pool(h, pool_w[0].astype(BF), pool_scale[0])
    x3, h = _resid(x2, y, mod(1, 0, 2), norm_g[1, 1], (norm_g[1, 2], mod(1, 0, 3), mod(1, 0, 4)))
    f = _moe_layer(h, router_w[1], router_bias[1], exp_w_gu[1], exp_w_down[1], shared_w_gu[1], shared_w_down[1])
    x4 = _resid(x3, f, mod(1, 0, 5), norm_g[1, 3])
    return x4[None]
```

```python
import functools

import jax
import jax.numpy as jnp
import numpy as np
from jax import lax
from jax.experimental import pallas as pl
from jax.experimental.pallas import tpu as pltpu

EPS = 1e-6
BF = jnp.bfloat16
F32 = jnp.float32
NEG_INF = float("-inf")

LANES = 128
HEAD_DIM = 128
SCAN_CHUNK = 128
SCAN_UNROLL = 4
SCAN_TILE = 1024
GRID_W = 64
POOL_WINDOWS = (2, 4, 8, 16)
N_ROUTE_GROUPS = 8
TOPK_ROUTE_GROUPS = 4
TOP_K = 8
ROUTED_SCALE = 2.5
N_MOD = 6
VMEM_LIMIT = 56 * 1024 * 1024


def _cparams(*sem):
    return pltpu.CompilerParams(dimension_semantics=sem, vmem_limit_bytes=VMEM_LIMIT)


def _silu(v):
    return v * jax.nn.sigmoid(v)


def _rms(v):
    return v * lax.rsqrt(jnp.mean(v * v, axis=-1, keepdims=True) + EPS)


def _split_bf16(v):
    hi = v.astype(BF)
    lo = (v - hi.astype(F32)).astype(BF)
    return hi, lo


def _pick(n, want):
    t = min(n, want)
    while n % t:
        t //= 2
    assert t >= 1
    return t


def _mods_kernel(s_ref, w_ref, b_ref, o_ref):
    s = _silu(s_ref[...])
    o_ref[0] = jnp.dot(s.astype(BF), w_ref[0].astype(BF), preferred_element_type=F32) + b_ref[0]


def _mods(cvecs, w_mod, b_mod):
    depth, d, n = w_mod.shape
    tn = _pick(n, 512)
    return pl.pallas_call(
        _mods_kernel,
        grid=(depth, n // tn),
        in_specs=[
            pl.BlockSpec((8, d), lambda l, j: (0, 0)),
            pl.BlockSpec((1, d, tn), lambda l, j: (l, 0, j)),
            pl.BlockSpec((1, 1, tn), lambda l, j: (l, 0, j)),
        ],
        out_specs=pl.BlockSpec((1, 8, tn), lambda l, j: (l, 0, j)),
        out_shape=jax.ShapeDtypeStruct((depth, 8, n), F32),
        compiler_params=_cparams("parallel", "parallel"),
        name="mods",
    )(cvecs, w_mod, b_mod.reshape(depth, 1, n))


def _normmod_kernel(x_ref, g_ref, sh_ref, sc_ref, o_ref):
    y = _rms(x_ref[...]) * g_ref[...]
    o_ref[...] = (y * (1 + sc_ref[...]) + sh_ref[...]).astype(o_ref.dtype)


def _normmod(x, g, shift, scale):
    m, d = x.shape
    tm = _pick(m, 256)
    vec = pl.BlockSpec((1, d), lambda i: (0, 0))
    return pl.pallas_call(
        _normmod_kernel,
        grid=(m // tm,),
        in_specs=[pl.BlockSpec((tm, d), lambda i: (i, 0)), vec, vec, vec],
        out_specs=pl.BlockSpec((tm, d), lambda i: (i, 0)),
        out_shape=jax.ShapeDtypeStruct((m, d), BF),
        compiler_params=_cparams("parallel"),
        name="normmod",
    )(x, g.reshape(1, d), shift.reshape(1, d), scale.reshape(1, d))


def _resid_kernel(x_ref, y_ref, gate_ref, gy_ref, xo_ref):
    yn = _rms(y_ref[...].astype(F32)) * gy_ref[...]
    xo_ref[...] = x_ref[...] + gate_ref[...] * yn


def _resid_norm_kernel(x_ref, y_ref, gate_ref, gy_ref, gx_ref, sh_ref, sc_ref, xo_ref, ho_ref):
    yn = _rms(y_ref[...].astype(F32)) * gy_ref[...]
    xn = x_ref[...] + gate_ref[...] * yn
    xo_ref[...] = xn
    hn = _rms(xn) * gx_ref[...]
    ho_ref[...] = (hn * (1 + sc_ref[...]) + sh_ref[...]).astype(ho_ref.dtype)


def _resid(x, y, gate, gy, nxt=None):
    m, d = x.shape
    tm = _pick(m, 256)
    row = pl.BlockSpec((tm, d), lambda i: (i, 0))
    vec = pl.BlockSpec((1, d), lambda i: (0, 0))
    v = lambda a: a.reshape(1, d)
    if nxt is None:
        return pl.pallas_call(
            _resid_kernel,
            grid=(m // tm,),
            in_specs=[row, row, vec, vec],
            out_specs=row,
            out_shape=jax.ShapeDtypeStruct((m, d), F32),
            compiler_params=_cparams("parallel"),
            name="resid",
        )(x, y, v(gate), v(gy))
    gx, shift, scale = nxt
    return pl.pallas_call(
        _resid_norm_kernel,
        grid=(m // tm,),
        in_specs=[row, row, vec, vec, vec, vec, vec],
        out_specs=[row, row],
        out_shape=[jax.ShapeDtypeStruct((m, d), F32), jax.ShapeDtypeStruct((m, d), BF)],
        compiler_params=_cparams("parallel"),
        name="resid_norm",
    )(x, y, v(gate), v(gy), v(gx), v(shift), v(scale))


def _mm_kernel(a_ref, w_ref, o_ref):
    o_ref[...] = jnp.dot(a_ref[...], w_ref[...], preferred_element_type=F32).astype(o_ref.dtype)


def _mm(a, w, out_dtype):
    m, k = a.shape
    n = w.shape[1]
    tm, tn = _pick(m, 1024), _pick(n, 1024)
    return pl.pallas_call(
        _mm_kernel,
        grid=(m // tm, n // tn),
        in_specs=[pl.BlockSpec((tm, k), lambda i, j: (i, 0)), pl.BlockSpec((k, tn), lambda i, j: (0, j))],
        out_specs=pl.BlockSpec((tm, tn), lambda i, j: (i, j)),
        out_shape=jax.ShapeDtypeStruct((m, n), out_dtype),
        compiler_params=_cparams("parallel", "parallel"),
        name="mm",
    )(a, w)


def _mm_heads_kernel(a_ref, w_ref, o_ref):
    r = jnp.dot(a_ref[...], w_ref[...], preferred_element_type=F32)
    for hh in range(o_ref.shape[0]):
        o_ref[hh] = r[:, hh * LANES:(hh + 1) * LANES].astype(o_ref.dtype)


def _mm_heads(a, w):
    m, k = a.shape
    n = w.shape[1]
    tm, tn = _pick(m, 1024), _pick(n, 1024)
    return pl.pallas_call(
        _mm_heads_kernel,
        grid=(m // tm, n // tn),
        in_specs=[pl.BlockSpec((tm, k), lambda i, j: (i, 0)), pl.BlockSpec((k, tn), lambda i, j: (0, j))],
        out_specs=pl.BlockSpec((tn // LANES, tm, LANES), lambda i, j: (j, i, 0)),
        out_shape=jax.ShapeDtypeStruct((n // LANES, m, LANES), BF),
        compiler_params=_cparams("parallel", "parallel"),
        name="proj",
    )(a, w)


def _scan_tables(c, reverse):
    n_levels = c.bit_length() - 1
    assert c == 1 << n_levels
    t = np.arange(c)[:, None]
    u = np.arange(c)[None, :]
    dmats = [u <= t, u > t]
    masks = [u == t]
    roles = []
    for lv in range(n_levels):
        size = c >> lv
        start = (t // size) * size
        boundary = start + size // 2 - 1
        later = t > boundary
        dmats.append(np.where(later, (u > boundary) & (u <= t), (u > t) & (u <= boundary)))
        masks.append(later & (u <= boundary.T) & (start == start.T))
        roles.append(np.broadcast_to(later, (c, HEAD_DIM)))
    dmat, mask, role = np.stack(dmats), np.stack(masks), np.stack(roles)
    if reverse:
        dmat, mask, role = dmat[:, ::-1, ::-1], mask[:, ::-1, ::-1], role[:, ::-1]
    return (jnp.asarray(dmat.reshape(-1, c), BF), jnp.asarray(mask, F32), jnp.asarray(role, F32))


def _scan_kernel(q_ref, v_ref, z_ref, lbl_ref, s0_ref, dmat_ref, mask_ref, roles_ref, o_ref, sfin_ref,
                 st_ref, *, layer, reverse):
    c = SCAN_CHUNK
    n_levels = roles_ref.shape[0]
    t_tile = q_ref.shape[1]
    span = min(SCAN_UNROLL * c, t_tile)
    n_spans = t_tile // span
    i = pl.program_id(1)

    @pl.when(i == 0)
    def _():
        st_ref[...] = s0_ref[0]

    lg = lbl_ref[0]
    e = jnp.exp(lg - jnp.max(lg, axis=0, keepdims=True))
    p = e / jnp.sum(e, axis=0, keepdims=True)
    lb = jnp.sum(p[:layer + 1], axis=0, keepdims=True)

    nt_dims = (((1,), (1,)), ((), ()))
    tn_dims = (((0,), (0,)), ((), ()))
    last = 0 if reverse else c - 1

    def chunk(st, lf, qc, kc, vc):
        ex = jnp.dot(dmat_ref[...], lf.astype(BF), preferred_element_type=F32)
        b = ex[0:c]
        qb, kb, vb = qc.astype(BF), kc.astype(BF), vc.astype(BF)
        att = jnp.where(mask_ref[0] > 0, lax.dot_general(qb, kb, nt_dims, preferred_element_type=F32), 0.0)
        for lv in range(n_levels):
            x = jnp.where(roles_ref[lv] > 0, qc, kc) * jnp.exp(ex[(2 + lv) * c:(3 + lv) * c])
            xb = x.astype(BF)
            gram = lax.dot_general(xb, xb, nt_dims, preferred_element_type=F32)
            att = att + jnp.where(mask_ref[1 + lv] > 0, gram, 0.0)
        qd = (qc * jnp.exp(b)).astype(BF)
        o = (jnp.dot(att.astype(BF), vb, preferred_element_type=F32)
             + lax.dot_general(qd, st.astype(BF), nt_dims, preferred_element_type=F32))
        kd = (kc * jnp.exp(ex[c:2 * c])).astype(BF)
        st_new = st * jnp.exp(b[last:last + 1]) + lax.dot_general(vb, kd, tn_dims, preferred_element_type=F32)
        return st_new, o

    def block(n, st):
        j = (n_spans - 1 - n) if reverse else n
        rows = pl.ds(pl.multiple_of(j * span, span), span)
        z = z_ref[0, rows, :].astype(F32)
        f = lb + (1 - lb) * jax.nn.sigmoid(z)
        logf = jnp.log(f)
        kk = 1 - f
        q = _silu(q_ref[0, rows, :].astype(F32))
        v = v_ref[0, rows, :].astype(F32)
        n_c = span // c
        outs = [None] * n_c
        for m in (range(n_c - 1, -1, -1) if reverse else range(n_c)):
            sl = slice(m * c, (m + 1) * c)
            st, outs[m] = chunk(st, logf[sl], q[sl], kk[sl], v[sl])
        o_ref[0, rows, :] = jnp.concatenate(outs, axis=0).astype(o_ref.dtype)
        return st

    st_ref[...] = lax.fori_loop(0, n_spans, block, st_ref[...])

    @pl.when(i == pl.num_programs(1) - 1)
    def _():
        sfin_ref[0] = st_ref[...]


def _scan(p, lb_logits, s0, *, n_heads, layer, reverse, z_off):
    _, l, _ = p.shape
    t_tile = _pick(l, SCAN_TILE)
    assert t_tile % SCAN_CHUNK == 0
    nt = l // t_tile
    tok = (lambda i: nt - 1 - i) if reverse else (lambda i: i)
    blk = lambda off: pl.BlockSpec((1, t_tile, HEAD_DIM), lambda h, i: (off + h, tok(i), 0))
    n_lb = lb_logits.shape[1]
    dmat, mask, roles = _scan_tables(SCAN_CHUNK, reverse)
    whole = lambda a: pl.BlockSpec(a.shape, lambda h, i: (0,) * a.ndim)
    return pl.pallas_call(
        functools.partial(_scan_kernel, layer=layer, reverse=reverse),
        grid=(n_heads, nt),
        in_specs=[
            blk(0), blk(2 * n_heads), blk(z_off),
            pl.BlockSpec((1, n_lb, HEAD_DIM), lambda h, i: (h, 0, 0)),
            pl.BlockSpec((1, HEAD_DIM, HEAD_DIM), lambda h, i: (h, 0, 0)),
            whole(dmat), whole(mask), whole(roles),
        ],
        out_specs=[
            pl.BlockSpec((1, t_tile, HEAD_DIM), lambda h, i: (h, tok(i), 0)),
            pl.BlockSpec((1, HEAD_DIM, HEAD_DIM), lambda h, i: (h, 0, 0)),
        ],
        out_shape=[
            jax.ShapeDtypeStruct((n_heads, l, HEAD_DIM), BF),
            jax.ShapeDtypeStruct((n_heads, HEAD_DIM, HEAD_DIM), F32),
        ],
        scratch_shapes=[pltpu.VMEM((HEAD_DIM, HEAD_DIM), F32)],
        compiler_params=_cparams("parallel", "arbitrary"),
        name="scan_bwd" if reverse else "scan_fwd",
    )(p, p, p, lb_logits, s0, dmat, mask, roles)


def _gate_kernel(of_ref, ob_ref, g_ref, ng_ref, a_ref):
    for h in range(of_ref.shape[0]):
        o = of_ref[h].astype(F32) + ob_ref[h].astype(F32)
        o = _rms(o) * ng_ref[h]
        a_ref[:, h * HEAD_DIM:(h + 1) * HEAD_DIM] = (o * _silu(g_ref[h].astype(F32))).astype(a_ref.dtype)


def _gate(o_f, o_b, p, norm_g):
    n_heads, l, _ = o_f.shape
    tm = _pick(l, 256)
    blk = lambda off: pl.BlockSpec((n_heads, tm, HEAD_DIM), lambda i: (off, i, 0))
    return pl.pallas_call(
        _gate_kernel,
        grid=(l // tm,),
        in_specs=[blk(0), blk(0), blk(1), pl.BlockSpec((n_heads, 1, HEAD_DIM), lambda i: (0, 0, 0))],
        out_specs=pl.BlockSpec((tm, n_heads * HEAD_DIM), lambda i: (i, 0)),
        out_shape=jax.ShapeDtypeStruct((l, n_heads * HEAD_DIM), BF),
        compiler_params=_cparams("parallel"),
        name="gate",
    )(o_f, o_b, p, norm_g.reshape(n_heads, 1, HEAD_DIM))


def _pool_body(win, prev_ref, cur_ref, next_ref, w_ref, scale_ref, o_ref, y_ref, d_ref, n_rows):
    i = pl.program_id(0)
    tile_rows = cur_ref.shape[0] // GRID_W
    halo = y_ref.shape[0] - tile_rows
    top = halo // 2
    half = win // 2
    r0 = i * tile_rows

    t_io = lax.broadcasted_iota(jnp.int32, (GRID_W, GRID_W), 0)
    s_io = lax.broadcasted_iota(jnp.int32, (GRID_W, GRID_W), 1)
    lo = jnp.maximum(t_io - half, 0)
    hi = jnp.minimum(t_io + win - half, GRID_W)
    a_c = jnp.where((s_io >= lo) & (s_io < hi), 1.0 / (hi - lo).astype(F32), 0.0)
    a_hi, a_lo = _split_bf16(a_c)

    def col_filter(src_ref, src_row, dst_row):
        xs = src_ref[pl.ds(pl.multiple_of(src_row * GRID_W, GRID_W), GRID_W), :]
        y = (jnp.dot(a_hi, xs, preferred_element_type=F32) + jnp.dot(a_lo, xs, preferred_element_type=F32))
        grow = jnp.full((1, y.shape[1]), r0 - top + dst_row, jnp.int32)
        ok = (grow >= 0) & (grow < n_rows)
        y_ref[dst_row] = jnp.where(ok, y, 0.0)

    def above(j, c):
        col_filter(prev_ref, tile_rows - half + j, top - half + j)
        return c

    def inside(j, c):
        col_filter(cur_ref, j, top + j)
        return c

    def below(j, c):
        col_filter(next_ref, j, top + tile_rows + j)
        return c

    lax.fori_loop(0, half, above, 0)
    lax.fori_loop(0, tile_rows, inside, 0)
    lax.fori_loop(0, half - 1, below, 0)

    def row_filter(r, c):
        grow = jnp.full((1, d_ref.shape[1]), r0 + r, jnp.int32)
        cnt = jnp.minimum(grow + win - half, n_rows) - jnp.maximum(grow - half, 0)
        z = y_ref[top + r - half]
        for j in range(1, win):
            z = z + y_ref[top + r - half + j]
        z = z * (1.0 / cnt.astype(F32))
        sl = pl.ds(pl.multiple_of(r * GRID_W, GRID_W), GRID_W)
        d_ref[sl, :] = (z - cur_ref[sl, :].astype(F32)).astype(d_ref.dtype)
        return c

    lax.fori_loop(0, tile_rows, row_filter, 0)
    o_ref[...] = jnp.dot(d_ref[...], w_ref[0], preferred_element_type=F32) * scale_ref[...]


def _pool_kernel(prev_ref, cur_ref, next_ref, w_ref, scale_ref, o_ref, y_ref, d_ref, *, n_rows):
    g = pl.program_id(1)
    for gi, win in enumerate(POOL_WINDOWS):
        @pl.when(g == gi)
        def _(win=win):
            _pool_body(win, prev_ref, cur_ref, next_ref, w_ref, scale_ref, o_ref, y_ref, d_ref, n_rows)


def _pool(h, w_pool, scale):
    l, d = h.shape
    n_groups, gc, _ = w_pool.shape
    n_rows = l // GRID_W
    tile_rows = _pick(n_rows, 16)
    halo = max(POOL_WINDOWS)
    assert tile_rows >= halo // 2 and len(POOL_WINDOWS) == n_groups
    tt = tile_rows * GRID_W
    nt = l // tt
    return pl.pallas_call(
        functools.partial(_pool_kernel, n_rows=n_rows),
        grid=(nt, n_groups),
        in_specs=[
            pl.BlockSpec((tt, gc), lambda i, g: (jnp.maximum(i - 1, 0), g)),
            pl.BlockSpec((tt, gc), lambda i, g: (i, g)),
            pl.BlockSpec((tt, gc), lambda i, g: (jnp.minimum(i + 1, nt - 1), g)),
            pl.BlockSpec((1, gc, gc), lambda i, g: (g, 0, 0)),
            pl.BlockSpec((1, gc), lambda i, g: (0, g)),
        ],
        out_specs=pl.BlockSpec((tt, gc), lambda i, g: (i, g)),
        out_shape=jax.ShapeDtypeStruct((l, d), F32),
        scratch_shapes=[pltpu.VMEM((tile_rows + halo, GRID_W, gc), F32), pltpu.VMEM((tt, gc), BF)],
        compiler_params=_cparams("parallel", "parallel"),
        name="pool",
    )(h, h, h, w_pool, scale.reshape(1, d))


def _router_kernel(h_ref, wrt_ref, bias_ref, o_ref):
    n_exp = wrt_ref.shape[0]
    tm = h_ref.shape[0]
    epg = n_exp // N_ROUTE_GROUPS
    logits = lax.dot_general(wrt_ref[...], h_ref[...], (((1,), (1,)), ((), ())), preferred_element_type=F32)
    scores = jax.nn.sigmoid(logits)
    shape3 = (N_ROUTE_GROUPS, epg, tm)
    sc3 = scores.reshape(shape3)
    s3 = (scores + bias_ref[...]).reshape(shape3)
    e_in = lax.broadcasted_iota(jnp.int32, shape3, 1).astype(F32)
    g_io = lax.broadcasted_iota(jnp.int32, shape3, 0).astype(F32)

    m1 = jnp.max(s3, axis=1, keepdims=True)
    i1 = jnp.min(jnp.where(s3 == m1, e_in, epg), axis=1, keepdims=True)
    m2 = jnp.max(jnp.where(e_in == i1, NEG_INF, s3), axis=1, keepdims=True)
    gs = jnp.broadcast_to(m1 + m2, shape3)

    keep = jnp.zeros(shape3, F32)
    cur = gs
    for _ in range(TOPK_ROUTE_GROUPS):
        m = jnp.max(cur, axis=0, keepdims=True)
        idx = jnp.min(jnp.where(cur == m, g_io, N_ROUTE_GROUPS), axis=0, keepdims=True)
        hit = g_io == idx
        keep = jnp.where(hit, 1.0, keep)
        cur = jnp.where(hit, NEG_INF, cur)

    e_io = g_io * epg + e_in
    chosen = jnp.zeros(shape3, F32)
    cur = jnp.where(keep > 0, s3, NEG_INF)
    for _ in range(TOP_K):
        m = jnp.max(jnp.max(cur, axis=0, keepdims=True), axis=1, keepdims=True)
        idx = jnp.where(cur == m, e_io, n_exp)
        idx = jnp.min(jnp.min(idx, axis=0, keepdims=True), axis=1, keepdims=True)
        hit = e_io == idx
        chosen = jnp.where(hit, 1.0, chosen)
        cur = jnp.where(hit, NEG_INF, cur)

    w = jnp.where(chosen > 0, sc3, 0.0)
    den = jnp.sum(jnp.sum(w, axis=0, keepdims=True), axis=1, keepdims=True)
    o_ref[...] = (w / den * ROUTED_SCALE).reshape(n_exp, tm)


def _router(h, w_r, b_r):
    t, d = h.shape
    n_exp = w_r.shape[1]
    tm = _pick(t, 512)
    return pl.pallas_call(
        _router_kernel,
        grid=(t // tm,),
        in_specs=[
            pl.BlockSpec((tm, d), lambda i: (i, 0)),
            pl.BlockSpec((n_exp, d), lambda i: (0, 0)),
            pl.BlockSpec((n_exp, 1), lambda i: (0, 0)),
        ],
        out_specs=pl.BlockSpec((n_exp, tm), lambda i: (0, i)),
        out_shape=jax.ShapeDtypeStruct((n_exp, t), F32),
        compiler_params=_cparams("parallel"),
        name="router",
    )(h, w_r.T.astype(BF), b_r.astype(F32).reshape(n_exp, 1))


def _moe_kernel(h_ref, comb_ref, wgu_ref, wd_ref, o_ref):
    e = pl.program_id(1)
    d_exp = wd_ref.shape[1]

    @pl.when(e == 0)
    def _():
        o_ref[...] = jnp.zeros_like(o_ref)

    gu = jnp.dot(h_ref[...], wgu_ref[0], preferred_element_type=F32)
    a, u = gu[:, :d_exp], gu[:, d_exp:]
    n_col = comb_ref.shape[1]
    onehot = jnp.where(lax.broadcasted_iota(jnp.int32, (n_col, d_exp), 0) == e, 1.0, 0.0).astype(BF)
    c_hi, c_lo = _split_bf16(comb_ref[...])
    cw = (jnp.dot(c_hi, onehot, preferred_element_type=F32) + jnp.dot(c_lo, onehot, preferred_element_type=F32))
    act = _silu(a) * u * cw
    o_ref[...] += jnp.dot(act.astype(BF), wd_ref[0], preferred_element_type=F32)


def _moe(h, comb, w_gu, w_down):
    t, d = h.shape
    n_e, _, f2 = w_gu.shape
    tm = _pick(t, 512)
    return pl.pallas_call(
        _moe_kernel,
        grid=(t // tm, n_e),
        in_specs=[
            pl.BlockSpec((tm, d), lambda i, e: (i, 0)),
            pl.BlockSpec((tm, comb.shape[1]), lambda i, e: (i, 0)),
            pl.BlockSpec((1, d, f2), lambda i, e: (e, 0, 0)),
            pl.BlockSpec((1, f2 // 2, d), lambda i, e: (e, 0, 0)),
        ],
        out_specs=pl.BlockSpec((tm, d), lambda i, e: (i, 0)),
        out_shape=jax.ShapeDtypeStruct((t, d), F32),
        compiler_params=_cparams("parallel", "arbitrary"),
        name="moe",
    )(h, comb, w_gu, w_down)


def _moe_layer(h, w_r, b_r, w_gu, w_down, s_gu, s_down):
    t = h.shape[0]
    n_exp = w_r.shape[1]
    comb_t = _router(h, w_r, b_r)
    comb = jnp.concatenate([comb_t.T, jnp.ones((t, 1), F32), jnp.zeros((t, LANES - n_exp - 1), F32)], axis=1)
    wgu_all = jnp.concatenate([w_gu, s_gu[None]], axis=0).astype(BF)
    wd_all = jnp.concatenate([w_down, s_down[None]], axis=0).astype(BF)
    return _moe(h, comb, wgu_all, wd_all)


def kernel(x, c, ctx, c_ctx, w_mod, b_mod, norm_g, hgrn_w_in, hgrn_lb_logits, hgrn_norm_g, hgrn_w_out,
           pool_w, pool_scale, router_w, router_bias, exp_w_gu, exp_w_down, shared_w_gu, shared_w_down):
    batch, seq, d = x.shape
    assert batch == 1 and c.shape[0] == 1
    depth = w_mod.shape[0]
    assert depth == 2 and w_mod.shape[2] == N_MOD * d
    n_heads = hgrn_w_in.shape[2] // 5 // HEAD_DIM
    x0 = x[0]

    cvecs = jnp.concatenate([c, c_ctx[None], jnp.zeros((6, d), F32)], axis=0)
    mods = _mods(cvecs, w_mod, b_mod)
    mod = lambda layer, who, j: mods[layer, who, j * d:(j + 1) * d]

    w_in = hgrn_w_in[0].astype(BF)
    hx = _normmod(x0, norm_g[0, 0], mod(0, 0, 0), mod(0, 0, 1))
    hc = _normmod(ctx[0], norm_g[0, 0], mod(0, 1, 0), mod(0, 1, 1))
    p = _mm_heads(hx, w_in)
    pc = _mm_heads(hc, w_in)
    n_lb = hgrn_lb_logits.shape[1]
    lbl = hgrn_lb_logits.astype(F32).reshape(2, n_lb, n_heads, HEAD_DIM).transpose(0, 2, 1, 3)
    zero_state = jnp.zeros((n_heads, HEAD_DIM, HEAD_DIM), F32)
    scan = functools.partial(_scan, n_heads=n_heads, layer=0)
    _, s_f = scan(pc, lbl[0], zero_state, reverse=False, z_off=3 * n_heads)
    _, s_b = scan(pc, lbl[1], zero_state, reverse=True, z_off=4 * n_heads)
    o_f, _ = scan(p, lbl[0], s_f, reverse=False, z_off=3 * n_heads)
    o_b, _ = scan(p, lbl[1], s_b, reverse=True, z_off=4 * n_heads)
    a = _gate(o_f, o_b, p, hgrn_norm_g[0])
    y = _mm(a, hgrn_w_out[0].astype(BF), F32)
    x1, h = _resid(x0, y, mod(0, 0, 2), norm_g[0, 1], (norm_g[0, 2], mod(0, 0, 3), mod(0, 0, 4)))
    f = _moe_layer(h, router_w[0], router_bias[0], exp_w_gu[0], exp_w_down[0], shared_w_gu[0], shared_w_down[0])
    x2, h = _resid(x1, f, mod(0, 0, 5), norm_g[0, 3], (norm_g[1, 0], mod(1, 0, 0), mod(1, 0, 1)))

    y = _pool(h, pool_w[0].astype(BF), pool_scale[0])
    x3, h = _resid(x2, y, mod(1, 0, 2), norm_g[1, 1], (norm_g[1, 2], mod(1, 0, 3), mod(1, 0, 4)))
    f = _moe_layer(h, router_w[1], router_bias[1], exp_w_gu[1], exp_w_down[1], shared_w_gu[1], shared_w_down[1])
    x4 = _resid(x3, f, mod(1, 0, 5), norm_g[1, 3])
    return x4[None]
```

```python
import functools

import jax
import jax.numpy as jnp
import numpy as np
from jax import lax
from jax.experimental import pallas as pl
from jax.experimental.pallas import tpu as pltpu

EPS = 1e-6
BF = jnp.bfloat16
F32 = jnp.float32
NEG_INF = float("-inf")

LANES = 128
HEAD_DIM = 128
SCAN_CHUNK = 128
SCAN_UNROLL = 4
SCAN_TILE = 1024
GRID_W = 64
POOL_WINDOWS = (2, 4, 8, 16)
N_ROUTE_GROUPS = 8
TOPK_ROUTE_GROUPS = 4
TOP_K = 8
ROUTED_SCALE = 2.5
N_MOD = 6
EXPERT_TILE = 512
VMEM_LIMIT = 56 * 1024 * 1024


def _cparams(*sem):
    return pltpu.CompilerParams(dimension_semantics=sem, vmem_limit_bytes=VMEM_LIMIT)


def _silu(v):
    return v * jax.nn.sigmoid(v)


def _rms(v):
    return v * lax.rsqrt(jnp.mean(v * v, axis=-1, keepdims=True) + EPS)


def _split_bf16(v):
    hi = v.astype(BF)
    lo = (v - hi.astype(F32)).astype(BF)
    return hi, lo


def _pick(n, want):
    t = min(n, want)
    while n % t:
        t //= 2
    assert t >= 1
    return t


def _mods_kernel(s_ref, w_ref, b_ref, o_ref):
    s = _silu(s_ref[...])
    o_ref[0] = jnp.dot(s.astype(BF), w_ref[0].astype(BF), preferred_element_type=F32) + b_ref[0]


def _mods(cvecs, w_mod, b_mod):
    depth, d, n = w_mod.shape
    tn = _pick(n, 512)
    return pl.pallas_call(
        _mods_kernel,
        grid=(depth, n // tn),
        in_specs=[
            pl.BlockSpec((8, d), lambda l, j: (0, 0)),
            pl.BlockSpec((1, d, tn), lambda l, j: (l, 0, j)),
            pl.BlockSpec((1, 1, tn), lambda l, j: (l, 0, j)),
        ],
        out_specs=pl.BlockSpec((1, 8, tn), lambda l, j: (l, 0, j)),
        out_shape=jax.ShapeDtypeStruct((depth, 8, n), F32),
        compiler_params=_cparams("parallel", "parallel"),
        name="mods",
    )(cvecs, w_mod, b_mod.reshape(depth, 1, n))


def _normmod_kernel(x_ref, g_ref, sh_ref, sc_ref, o_ref):
    y = _rms(x_ref[...]) * g_ref[...]
    o_ref[...] = (y * (1 + sc_ref[...]) + sh_ref[...]).astype(o_ref.dtype)


def _normmod(x, g, shift, scale):
    m, d = x.shape
    tm = _pick(m, 256)
    vec = pl.BlockSpec((1, d), lambda i: (0, 0))
    return pl.pallas_call(
        _normmod_kernel,
        grid=(m // tm,),
        in_specs=[pl.BlockSpec((tm, d), lambda i: (i, 0)), vec, vec, vec],
        out_specs=pl.BlockSpec((tm, d), lambda i: (i, 0)),
        out_shape=jax.ShapeDtypeStruct((m, d), BF),
        compiler_params=_cparams("parallel"),
        name="normmod",
    )(x, g.reshape(1, d), shift.reshape(1, d), scale.reshape(1, d))


def _resid_kernel(x_ref, y_ref, gate_ref, gy_ref, xo_ref):
    yn = _rms(y_ref[...].astype(F32)) * gy_ref[...]
    xo_ref[...] = x_ref[...] + gate_ref[...] * yn


def _resid_norm_kernel(x_ref, y_ref, gate_ref, gy_ref, gx_ref, sh_ref, sc_ref, xo_ref, ho_ref):
    yn = _rms(y_ref[...].astype(F32)) * gy_ref[...]
    xn = x_ref[...] + gate_ref[...] * yn
    xo_ref[...] = xn
    hn = _rms(xn) * gx_ref[...]
    ho_ref[...] = (hn * (1 + sc_ref[...]) + sh_ref[...]).astype(ho_ref.dtype)


def _resid(x, y, gate, gy, nxt=None):
    m, d = x.shape
    tm = _pick(m, 256)
    row = pl.BlockSpec((tm, d), lambda i: (i, 0))
    vec = pl.BlockSpec((1, d), lambda i: (0, 0))
    v = lambda a: a.reshape(1, d)
    if nxt is None:
        return pl.pallas_call(
            _resid_kernel,
            grid=(m // tm,),
            in_specs=[row, row, vec, vec],
            out_specs=row,
            out_shape=jax.ShapeDtypeStruct((m, d), F32),
            compiler_params=_cparams("parallel"),
            name="resid",
        )(x, y, v(gate), v(gy))
    gx, shift, scale = nxt
    return pl.pallas_call(
        _resid_norm_kernel,
        grid=(m // tm,),
        in_specs=[row, row, vec, vec, vec, vec, vec],
        out_specs=[row, row],
        out_shape=[jax.ShapeDtypeStruct((m, d), F32), jax.ShapeDtypeStruct((m, d), BF)],
        compiler_params=_cparams("parallel"),
        name="resid_norm",
    )(x, y, v(gate), v(gy), v(gx), v(shift), v(scale))


def _mm_kernel(a_ref, w_ref, o_ref):
    o_ref[...] = jnp.dot(a_ref[...], w_ref[...], preferred_element_type=F32).astype(o_ref.dtype)


def _mm(a, w, out_dtype):
    m, k = a.shape
    n = w.shape[1]
    tm, tn = _pick(m, 1024), _pick(n, 1024)
    return pl.pallas_call(
        _mm_kernel,
        grid=(m // tm, n // tn),
        in_specs=[pl.BlockSpec((tm, k), lambda i, j: (i, 0)), pl.BlockSpec((k, tn), lambda i, j: (0, j))],
        out_specs=pl.BlockSpec((tm, tn), lambda i, j: (i, j)),
        out_shape=jax.ShapeDtypeStruct((m, n), out_dtype),
        compiler_params=_cparams("parallel", "parallel"),
        name="mm",
    )(a, w)


def _mm_heads_kernel(a_ref, w_ref, o_ref):
    r = jnp.dot(a_ref[...], w_ref[...], preferred_element_type=F32)
    for hh in range(o_ref.shape[0]):
        o_ref[hh] = r[:, hh * LANES:(hh + 1) * LANES].astype(o_ref.dtype)


def _mm_heads(a, w):
    m, k = a.shape
    n = w.shape[1]
    tm, tn = _pick(m, 1024), _pick(n, 1024)
    return pl.pallas_call(
        _mm_heads_kernel,
        grid=(m // tm, n // tn),
        in_specs=[pl.BlockSpec((tm, k), lambda i, j: (i, 0)), pl.BlockSpec((k, tn), lambda i, j: (0, j))],
        out_specs=pl.BlockSpec((tn // LANES, tm, LANES), lambda i, j: (j, i, 0)),
        out_shape=jax.ShapeDtypeStruct((n // LANES, m, LANES), BF),
        compiler_params=_cparams("parallel", "parallel"),
        name="proj",
    )(a, w)


def _scan_tables(c, reverse):
    n_levels = c.bit_length() - 1
    assert c == 1 << n_levels
    t = np.arange(c)[:, None]
    u = np.arange(c)[None, :]
    dmats = [u <= t, u > t]
    masks = [u == t]
    roles = []
    for lv in range(n_levels):
        size = c >> lv
        start = (t // size) * size
        boundary = start + size // 2 - 1
        later = t > boundary
        dmats.append(np.where(later, (u > boundary) & (u <= t), (u > t) & (u <= boundary)))
        masks.append(later & (u <= boundary.T) & (start == start.T))
        roles.append(np.broadcast_to(later, (c, HEAD_DIM)))
    dmat, mask, role = np.stack(dmats), np.stack(masks), np.stack(roles)
    if reverse:
        dmat, mask, role = dmat[:, ::-1, ::-1], mask[:, ::-1, ::-1], role[:, ::-1]
    return (jnp.asarray(dmat.reshape(-1, c), BF), jnp.asarray(mask, F32), jnp.asarray(role, F32))


def _scan_kernel(q_ref, v_ref, z_ref, lbl_ref, s0_ref, dmat_ref, mask_ref, roles_ref, o_ref, sfin_ref,
                 st_ref, *, layer, reverse):
    c = SCAN_CHUNK
    n_levels = roles_ref.shape[0]
    t_tile = q_ref.shape[1]
    span = min(SCAN_UNROLL * c, t_tile)
    n_spans = t_tile // span
    i = pl.program_id(1)

    @pl.when(i == 0)
    def _():
        st_ref[...] = s0_ref[0]

    lg = lbl_ref[0]
    e = jnp.exp(lg - jnp.max(lg, axis=0, keepdims=True))
    p = e / jnp.sum(e, axis=0, keepdims=True)
    lb = jnp.sum(p[:layer + 1], axis=0, keepdims=True)

    nt_dims = (((1,), (1,)), ((), ()))
    tn_dims = (((0,), (0,)), ((), ()))
    last = 0 if reverse else c - 1

    def chunk(st, lf, qc, kc, vc):
        ex = jnp.dot(dmat_ref[...], lf.astype(BF), preferred_element_type=F32)
        b = ex[0:c]
        qb, kb, vb = qc.astype(BF), kc.astype(BF), vc.astype(BF)
        att = jnp.where(mask_ref[0] > 0, lax.dot_general(qb, kb, nt_dims, preferred_element_type=F32), 0.0)
        for lv in range(n_levels):
            x = jnp.where(roles_ref[lv] > 0, qc, kc) * jnp.exp(ex[(2 + lv) * c:(3 + lv) * c])
            xb = x.astype(BF)
            gram = lax.dot_general(xb, xb, nt_dims, preferred_element_type=F32)
            att = att + jnp.where(mask_ref[1 + lv] > 0, gram, 0.0)
        qd = (qc * jnp.exp(b)).astype(BF)
        o = (jnp.dot(att.astype(BF), vb, preferred_element_type=F32)
             + lax.dot_general(qd, st.astype(BF), nt_dims, preferred_element_type=F32))
        kd = (kc * jnp.exp(ex[c:2 * c])).astype(BF)
        st_new = st * jnp.exp(b[last:last + 1]) + lax.dot_general(vb, kd, tn_dims, preferred_element_type=F32)
        return st_new, o

    def block(n, st):
        j = (n_spans - 1 - n) if reverse else n
        rows = pl.ds(pl.multiple_of(j * span, span), span)
        z = z_ref[0, rows, :].astype(F32)
        f = lb + (1 - lb) * jax.nn.sigmoid(z)
        logf = jnp.log(f)
        kk = 1 - f
        q = _silu(q_ref[0, rows, :].astype(F32))
        v = v_ref[0, rows, :].astype(F32)
        n_c = span // c
        outs = [None] * n_c
        for m in (range(n_c - 1, -1, -1) if reverse else range(n_c)):
            sl = slice(m * c, (m + 1) * c)
            st, outs[m] = chunk(st, logf[sl], q[sl], kk[sl], v[sl])
        o_ref[0, rows, :] = jnp.concatenate(outs, axis=0).astype(o_ref.dtype)
        return st

    st_ref[...] = lax.fori_loop(0, n_spans, block, st_ref[...])

    @pl.when(i == pl.num_programs(1) - 1)
    def _():
        sfin_ref[0] = st_ref[...]


def _scan(p, lb_logits, s0, *, n_heads, layer, reverse, z_off):
    _, l, _ = p.shape
    t_tile = _pick(l, SCAN_TILE)
    assert t_tile % SCAN_CHUNK == 0
    nt = l // t_tile
    tok = (lambda i: nt - 1 - i) if reverse else (lambda i: i)
    blk = lambda off: pl.BlockSpec((1, t_tile, HEAD_DIM), lambda h, i: (off + h, tok(i), 0))
    n_lb = lb_logits.shape[1]
    dmat, mask, roles = _scan_tables(SCAN_CHUNK, reverse)
    whole = lambda a: pl.BlockSpec(a.shape, lambda h, i: (0,) * a.ndim)
    return pl.pallas_call(
        functools.partial(_scan_kernel, layer=layer, reverse=reverse),
        grid=(n_heads, nt),
        in_specs=[
            blk(0), blk(2 * n_heads), blk(z_off),
            pl.BlockSpec((1, n_lb, HEAD_DIM), lambda h, i: (h, 0, 0)),
            pl.BlockSpec((1, HEAD_DIM, HEAD_DIM), lambda h, i: (h, 0, 0)),
            whole(dmat), whole(mask), whole(roles),
        ],
        out_specs=[
            pl.BlockSpec((1, t_tile, HEAD_DIM), lambda h, i: (h, tok(i), 0)),
            pl.BlockSpec((1, HEAD_DIM, HEAD_DIM), lambda h, i: (h, 0, 0)),
        ],
        out_shape=[
            jax.ShapeDtypeStruct((n_heads, l, HEAD_DIM), BF),
            jax.ShapeDtypeStruct((n_heads, HEAD_DIM, HEAD_DIM), F32),
        ],
        scratch_shapes=[pltpu.VMEM((HEAD_DIM, HEAD_DIM), F32)],
        compiler_params=_cparams("parallel", "arbitrary"),
        name="scan_bwd" if reverse else "scan_fwd",
    )(p, p, p, lb_logits, s0, dmat, mask, roles)


def _gate_kernel(of_ref, ob_ref, g_ref, ng_ref, a_ref):
    for h in range(of_ref.shape[0]):
        o = of_ref[h].astype(F32) + ob_ref[h].astype(F32)
        o = _rms(o) * ng_ref[h]
        a_ref[:, h * HEAD_DIM:(h + 1) * HEAD_DIM] = (o * _silu(g_ref[h].astype(F32))).astype(a_ref.dtype)


def _gate(o_f, o_b, p, norm_g):
    n_heads, l, _ = o_f.shape
    tm = _pick(l, 256)
    blk = lambda off: pl.BlockSpec((n_heads, tm, HEAD_DIM), lambda i: (off, i, 0))
    return pl.pallas_call(
        _gate_kernel,
        grid=(l // tm,),
        in_specs=[blk(0), blk(0), blk(1), pl.BlockSpec((n_heads, 1, HEAD_DIM), lambda i: (0, 0, 0))],
        out_specs=pl.BlockSpec((tm, n_heads * HEAD_DIM), lambda i: (i, 0)),
        out_shape=jax.ShapeDtypeStruct((l, n_heads * HEAD_DIM), BF),
        compiler_params=_cparams("parallel"),
        name="gate",
    )(o_f, o_b, p, norm_g.reshape(n_heads, 1, HEAD_DIM))


def _pool_body(win, prev_ref, cur_ref, next_ref, w_ref, scale_ref, o_ref, y_ref, d_ref, n_rows):
    i = pl.program_id(0)
    tile_rows = cur_ref.shape[0] // GRID_W
    halo = y_ref.shape[0] - tile_rows
    top = halo // 2
    half = win // 2
    r0 = i * tile_rows

    t_io = lax.broadcasted_iota(jnp.int32, (GRID_W, GRID_W), 0)
    s_io = lax.broadcasted_iota(jnp.int32, (GRID_W, GRID_W), 1)
    lo = jnp.maximum(t_io - half, 0)
    hi = jnp.minimum(t_io + win - half, GRID_W)
    a_c = jnp.where((s_io >= lo) & (s_io < hi), 1.0 / (hi - lo).astype(F32), 0.0)
    a_hi, a_lo = _split_bf16(a_c)

    def col_filter(src_ref, src_row, dst_row):
        xs = src_ref[pl.ds(pl.multiple_of(src_row * GRID_W, GRID_W), GRID_W), :]
        y = (jnp.dot(a_hi, xs, preferred_element_type=F32) + jnp.dot(a_lo, xs, preferred_element_type=F32))
        grow = jnp.full((1, y.shape[1]), r0 - top + dst_row, jnp.int32)
        ok = (grow >= 0) & (grow < n_rows)
        y_ref[dst_row] = jnp.where(ok, y, 0.0)

    def above(j, c):
        col_filter(prev_ref, tile_rows - half + j, top - half + j)
        return c

    def inside(j, c):
        col_filter(cur_ref, j, top + j)
        return c

    def below(j, c):
        col_filter(next_ref, j, top + tile_rows + j)
        return c

    lax.fori_loop(0, half, above, 0)
    lax.fori_loop(0, tile_rows, inside, 0)
    lax.fori_loop(0, half - 1, below, 0)

    def row_filter(r, c):
        grow = jnp.full((1, d_ref.shape[1]), r0 + r, jnp.int32)
        cnt = jnp.minimum(grow + win - half, n_rows) - jnp.maximum(grow - half, 0)
        z = y_ref[top + r - half]
        for j in range(1, win):
            z = z + y_ref[top + r - half + j]
        z = z * (1.0 / cnt.astype(F32))
        sl = pl.ds(pl.multiple_of(r * GRID_W, GRID_W), GRID_W)
        d_ref[sl, :] = (z - cur_ref[sl, :].astype(F32)).astype(d_ref.dtype)
        return c

    lax.fori_loop(0, tile_rows, row_filter, 0)
    o_ref[...] = jnp.dot(d_ref[...], w_ref[0], preferred_element_type=F32) * scale_ref[...]


def _pool_kernel(prev_ref, cur_ref, next_ref, w_ref, scale_ref, o_ref, y_ref, d_ref, *, n_rows):
    g = pl.program_id(1)
    for gi, win in enumerate(POOL_WINDOWS):
        @pl.when(g == gi)
        def _(win=win):
            _pool_body(win, prev_ref, cur_ref, next_ref, w_ref, scale_ref, o_ref, y_ref, d_ref, n_rows)


def _pool(h, w_pool, scale):
    l, d = h.shape
    n_groups, gc, _ = w_pool.shape
    n_rows = l // GRID_W
    tile_rows = _pick(n_rows, 16)
    halo = max(POOL_WINDOWS)
    assert tile_rows >= halo // 2 and len(POOL_WINDOWS) == n_groups
    tt = tile_rows * GRID_W
    nt = l // tt
    return pl.pallas_call(
        functools.partial(_pool_kernel, n_rows=n_rows),
        grid=(nt, n_groups),
        in_specs=[
            pl.BlockSpec((tt, gc), lambda i, g: (jnp.maximum(i - 1, 0), g)),
            pl.BlockSpec((tt, gc), lambda i, g: (i, g)),
            pl.BlockSpec((tt, gc), lambda i, g: (jnp.minimum(i + 1, nt - 1), g)),
            pl.BlockSpec((1, gc, gc), lambda i, g: (g, 0, 0)),
            pl.BlockSpec((1, gc), lambda i, g: (0, g)),
        ],
        out_specs=pl.BlockSpec((tt, gc), lambda i, g: (i, g)),
        out_shape=jax.ShapeDtypeStruct((l, d), F32),
        scratch_shapes=[pltpu.VMEM((tile_rows + halo, GRID_W, gc), F32), pltpu.VMEM((tt, gc), BF)],
        compiler_params=_cparams("parallel", "parallel"),
        name="pool",
    )(h, h, h, w_pool, scale.reshape(1, d))


def _router_kernel(h_ref, wrt_ref, bias_ref, upper_ref, eidx_ref, wts_ref, rank_ref, cnt_ref, carry_ref):
    n_exp = wrt_ref.shape[0]
    tm = h_ref.shape[0]

    @pl.when(pl.program_id(0) == 0)
    def _():
        carry_ref[...] = jnp.zeros_like(carry_ref)

    epg = n_exp // N_ROUTE_GROUPS
    logits = lax.dot_general(wrt_ref[...], h_ref[...], (((1,), (1,)), ((), ())), preferred_element_type=F32)
    scores = jax.nn.sigmoid(logits)
    shape3 = (N_ROUTE_GROUPS, epg, tm)
    sc3 = scores.reshape(shape3)
    s3 = (scores + bias_ref[...]).reshape(shape3)
    e_in = lax.broadcasted_iota(jnp.int32, shape3, 1).astype(F32)
    g_io = lax.broadcasted_iota(jnp.int32, shape3, 0).astype(F32)

    m1 = jnp.max(s3, axis=1, keepdims=True)
    i1 = jnp.min(jnp.where(s3 == m1, e_in, epg), axis=1, keepdims=True)
    m2 = jnp.max(jnp.where(e_in == i1, NEG_INF, s3), axis=1, keepdims=True)
    gs = jnp.broadcast_to(m1 + m2, shape3)

    keep = jnp.zeros(shape3, F32)
    cur = gs
    for _ in range(TOPK_ROUTE_GROUPS):
        m = jnp.max(cur, axis=0, keepdims=True)
        idx = jnp.min(jnp.where(cur == m, g_io, N_ROUTE_GROUPS), axis=0, keepdims=True)
        hit = g_io == idx
        keep = jnp.where(hit, 1.0, keep)
        cur = jnp.where(hit, NEG_INF, cur)

    e_io = g_io * epg + e_in
    chosen = jnp.zeros(shape3, F32)
    cur = jnp.where(keep > 0, s3, NEG_INF)
    picks = []
    for _ in range(TOP_K):
        m = jnp.max(jnp.max(cur, axis=0, keepdims=True), axis=1, keepdims=True)
        idx = jnp.where(cur == m, e_io, n_exp)
        idx = jnp.min(jnp.min(idx, axis=0, keepdims=True), axis=1, keepdims=True)
        hit = e_io == idx
        chosen = jnp.where(hit, 1.0, chosen)
        cur = jnp.where(hit, NEG_INF, cur)
        picks.append(idx)

    w = jnp.where(chosen > 0, sc3, 0.0)
    den = jnp.sum(jnp.sum(w, axis=0, keepdims=True), axis=1, keepdims=True)
    w = w / den * ROUTED_SCALE

    ch = chosen.reshape(n_exp, tm).astype(BF)
    carry = carry_ref[...]
    rank = (jnp.dot(ch, upper_ref[...], preferred_element_type=F32)
            + jnp.concatenate([carry] * (tm // LANES), axis=1)).reshape(shape3)
    carry = carry + jnp.dot(ch, jnp.ones((tm, LANES), BF), preferred_element_type=F32)
    carry_ref[...] = carry
    cnt_ref[...] = carry

    def per_pick(vals, idx):
        sel = jnp.where(e_io == idx, vals, 0.0)
        return jnp.sum(jnp.sum(sel, axis=0, keepdims=True), axis=1, keepdims=True).reshape(1, tm)

    eidx_ref[...] = jnp.concatenate([idx.reshape(1, tm) for idx in picks], axis=0).astype(jnp.int32)
    wts_ref[...] = jnp.concatenate([per_pick(w, idx) for idx in picks], axis=0)
    rank_ref[...] = jnp.concatenate([per_pick(rank, idx) for idx in picks], axis=0).astype(jnp.int32)


def _router(h, w_r, b_r):
    t, d = h.shape
    n_exp = w_r.shape[1]
    tm = _pick(t, 512)
    assert tm % LANES == 0
    upper = jnp.asarray(np.triu(np.ones((tm, tm), np.float32), 1), BF)
    pick = pl.BlockSpec((TOP_K, tm), lambda i: (0, i))
    eidx, wts, rank, cnt = pl.pallas_call(
        _router_kernel,
        grid=(t // tm,),
        in_specs=[
            pl.BlockSpec((tm, d), lambda i: (i, 0)),
            pl.BlockSpec((n_exp, d), lambda i: (0, 0)),
            pl.BlockSpec((n_exp, 1), lambda i: (0, 0)),
            pl.BlockSpec((tm, tm), lambda i: (0, 0)),
        ],
        out_specs=[pick, pick, pick, pl.BlockSpec((n_exp, LANES), lambda i: (0, 0))],
        out_shape=[
            jax.ShapeDtypeStruct((TOP_K, t), jnp.int32),
            jax.ShapeDtypeStruct((TOP_K, t), F32),
            jax.ShapeDtypeStruct((TOP_K, t), jnp.int32),
            jax.ShapeDtypeStruct((n_exp, LANES), F32),
        ],
        scratch_shapes=[pltpu.VMEM((n_exp, LANES), F32)],
        compiler_params=_cparams("arbitrary"),
        name="router",
    )(h, w_r.T.astype(BF), b_r.astype(F32).reshape(n_exp, 1), upper)
    return eidx, wts, rank, cnt[:, 0].astype(jnp.int32)


HI_MASK = 0xFFFF0000


def _pack_rows(v):
    half = v.shape[1] // 2
    bits = lambda a: pltpu.bitcast(a.astype(BF).astype(F32), jnp.uint32)
    return lax.shift_right_logical(bits(v[:, :half]), jnp.uint32(16)) | (bits(v[:, half:]) & jnp.uint32(HI_MASK))


def _unpack_rows(w):
    lo = pltpu.bitcast(lax.shift_left(w, jnp.uint32(16)), F32)
    hi = pltpu.bitcast(w & jnp.uint32(HI_MASK), F32)
    return lo, hi


def _packh_kernel(h_ref, o_ref):
    o_ref[...] = _pack_rows(h_ref[...].astype(F32))


def _packh(h):
    t, d = h.shape
    tm = _pick(t, 512)
    return pl.pallas_call(
        _packh_kernel,
        grid=(t // tm,),
        in_specs=[pl.BlockSpec((tm, d), lambda i: (i, 0))],
        out_specs=pl.BlockSpec((tm, d // 2), lambda i: (i, 0)),
        out_shape=jax.ShapeDtypeStruct((t, d // 2), jnp.uint32),
        compiler_params=_cparams("parallel"),
        name="packh",
    )(h)


def _dispatch_kernel(pos_ref, h_hbm, xs_hbm, sem):
    i = pl.program_id(0)
    n_k, tm = pos_ref.shape

    def row_copy(t, k, slot):
        return pltpu.make_async_copy(h_hbm.at[pl.ds(i * tm + t, 1)], xs_hbm.at[pl.ds(pos_ref[k, t], 1)],
                                     sem.at[slot])

    def issue(t, c):
        for k in range(n_k):
            row_copy(t, k, 0).start()
        return c

    lax.fori_loop(0, tm, issue, 0, unroll=8)

    def drain(t, c):
        for k in range(n_k):
            pltpu.make_async_copy(h_hbm.at[pl.ds(0, 1)], xs_hbm.at[pl.ds(0, 1)], sem.at[0]).wait()
        return c

    lax.fori_loop(0, tm, drain, 0, unroll=8)


def _dispatch(hp, pos):
    t, dw = hp.shape
    n_k = pos.shape[0]
    tm = _pick(t, 128)
    return pl.pallas_call(
        _dispatch_kernel,
        grid=(t // tm,),
        in_specs=[
            pl.BlockSpec((n_k, tm), lambda i: (0, i), memory_space=pltpu.SMEM),
            pl.BlockSpec(memory_space=pl.ANY),
        ],
        out_specs=pl.BlockSpec(memory_space=pl.ANY),
        out_shape=jax.ShapeDtypeStruct((n_k * t, dw), jnp.uint32),
        scratch_shapes=[pltpu.SemaphoreType.DMA((1,))],
        compiler_params=_cparams("arbitrary"),
        name="dispatch",
    )(pos, hp)


def _experts_kernel(tile_ref, exp_ref, lo_ref, hi_ref, xs_ref, wgu_ref, wd_ref, ys_ref, acc_ref):
    w = pl.program_id(0)
    tm = xs_ref.shape[0]
    half = xs_ref.shape[1]
    d_exp = wd_ref.shape[1]
    first = jnp.logical_or(w == 0, tile_ref[w] != tile_ref[jnp.maximum(w - 1, 0)])

    @pl.when(first)
    def _():
        acc_ref[...] = jnp.zeros_like(acc_ref)

    @pl.when(hi_ref[w] > lo_ref[w])
    def _():
        x_lo, x_hi = _unpack_rows(xs_ref[...])
        gu = (jnp.dot(x_lo.astype(BF), wgu_ref[0, :half, :], preferred_element_type=F32)
              + jnp.dot(x_hi.astype(BF), wgu_ref[0, half:, :], preferred_element_type=F32))
        row = lax.broadcasted_iota(jnp.int32, (tm, d_exp), 0)
        mine = (row >= lo_ref[w]) & (row < hi_ref[w])
        act = jnp.where(mine, _silu(gu[:, :d_exp]) * gu[:, d_exp:], 0.0)
        acc_ref[...] += jnp.dot(act.astype(BF), wd_ref[0], preferred_element_type=F32)

    ys_ref[...] = _pack_rows(acc_ref[...])


def _experts(xs, items, w_gu, w_down, tm):
    p, dw = xs.shape
    _, d, f2 = w_gu.shape
    n_items = items[0].shape[0]
    grid_spec = pltpu.PrefetchScalarGridSpec(
        num_scalar_prefetch=4,
        grid=(n_items,),
        in_specs=[
            pl.BlockSpec((tm, dw), lambda w, tile, exp, lo, hi: (tile[w], 0)),
            pl.BlockSpec((1, d, f2), lambda w, tile, exp, lo, hi: (exp[w], 0, 0)),
            pl.BlockSpec((1, f2 // 2, d), lambda w, tile, exp, lo, hi: (exp[w], 0, 0)),
        ],
        out_specs=pl.BlockSpec((tm, dw), lambda w, tile, exp, lo, hi: (tile[w], 0)),
        scratch_shapes=[pltpu.VMEM((tm, d), F32)],
    )
    return pl.pallas_call(
        _experts_kernel,
        grid_spec=grid_spec,
        out_shape=jax.ShapeDtypeStruct((p, dw), jnp.uint32),
        compiler_params=_cparams("arbitrary"),
        name="experts",
    )(*items, xs, w_gu, w_down)


def _expert_items(cnt, tm, n_tiles):
    n_exp = cnt.shape[0]
    n_items = n_tiles + n_exp - 1
    end = jnp.cumsum(cnt)
    start = end - cnt
    first_tile = start // tm
    n_e = jnp.where(cnt > 0, (end - 1) // tm - first_tile + 1, 0)
    item_end = jnp.cumsum(n_e)
    w = jnp.arange(n_items, dtype=jnp.int32)
    e = jnp.minimum(jnp.searchsorted(item_end, w, side="right"), n_exp - 1).astype(jnp.int32)
    valid = w < item_end[-1]
    tile = jnp.where(valid, first_tile[e] + (w - (item_end[e] - n_e[e])), n_tiles - 1).astype(jnp.int32)
    lo = jnp.where(valid, jnp.maximum(start[e], tile * tm) - tile * tm, 0).astype(jnp.int32)
    hi = jnp.where(valid, jnp.minimum(end[e], (tile + 1) * tm) - tile * tm, 0).astype(jnp.int32)
    e = jnp.where(valid, e, e[jnp.maximum(item_end[-1] - 1, 0)])
    return tile, e, lo, hi


def _shared_kernel(h_ref, wgu_ref, wd_ref, o_ref):
    d_exp = wd_ref.shape[0]
    gu = jnp.dot(h_ref[...], wgu_ref[...], preferred_element_type=F32)
    act = _silu(gu[:, :d_exp]) * gu[:, d_exp:]
    o_ref[...] = jnp.dot(act.astype(BF), wd_ref[...], preferred_element_type=F32)


def _shared(h, w_gu, w_down):
    t, d = h.shape
    f2 = w_gu.shape[1]
    tm = _pick(t, 512)
    return pl.pallas_call(
        _shared_kernel,
        grid=(t // tm,),
        in_specs=[
            pl.BlockSpec((tm, d), lambda i: (i, 0)),
            pl.BlockSpec((d, f2), lambda i: (0, 0)),
            pl.BlockSpec((f2 // 2, d), lambda i: (0, 0)),
        ],
        out_specs=pl.BlockSpec((tm, d), lambda i: (i, 0)),
        out_shape=jax.ShapeDtypeStruct((t, d), F32),
        compiler_params=_cparams("parallel"),
        name="shared",
    )(h, w_gu, w_down)


def _combine_kernel(pos_ref, sh_ref, wts_ref, ys_hbm, o_ref, buf_ref, sem):
    i = pl.program_id(0)
    n = pl.num_programs(0)
    n_k, tm = pos_ref.shape[1], pos_ref.shape[2]
    half = buf_ref.shape[3]

    def row_copy(j, t, k, slot):
        return pltpu.make_async_copy(ys_hbm.at[pl.ds(pos_ref[j, k, t], 1)], buf_ref.at[slot, k, pl.ds(t, 1)],
                                     sem.at[slot])

    def gather(j, slot):
        def issue(t, c):
            for k in range(n_k):
                row_copy(j, t, k, slot).start()
            return c
        lax.fori_loop(0, tm, issue, 0, unroll=8)

    @pl.when(i == 0)
    def _():
        gather(0, 0)

    @pl.when(i + 1 < n)
    def _():
        gather(1, (i + 1) % 2)

    slot = i % 2

    def drain(t, c):
        for k in range(n_k):
            pltpu.make_async_copy(ys_hbm.at[pl.ds(0, 1)], buf_ref.at[slot, 0, pl.ds(0, 1)], sem.at[slot]).wait()
        return c

    lax.fori_loop(0, tm, drain, 0, unroll=8)

    acc_lo = sh_ref[:, :half]
    acc_hi = sh_ref[:, half:]
    for k in range(n_k):
        y_lo, y_hi = _unpack_rows(buf_ref[slot, k])
        wk = wts_ref[:, k:k + 1]
        acc_lo = acc_lo + wk * y_lo
        acc_hi = acc_hi + wk * y_hi
    o_ref[:, :half] = acc_lo
    o_ref[:, half:] = acc_hi


def _combine(ys, pos, wts, sh):
    t, d = sh.shape
    n_k = pos.shape[0]
    tm = _pick(t, 64)
    nt = t // tm
    pos3 = pos.reshape(n_k, nt, tm).transpose(1, 0, 2)
    pos_pair = jnp.stack([pos3, jnp.concatenate([pos3[1:], pos3[-1:]], axis=0)], axis=1)
    return pl.pallas_call(
        _combine_kernel,
        grid=(nt,),
        in_specs=[
            pl.BlockSpec((None, 2, n_k, tm), lambda i: (i, 0, 0, 0), memory_space=pltpu.SMEM),
            pl.BlockSpec((tm, d), lambda i: (i, 0)),
            pl.BlockSpec((tm, n_k), lambda i: (i, 0)),
            pl.BlockSpec(memory_space=pl.ANY),
        ],
        out_specs=pl.BlockSpec((tm, d), lambda i: (i, 0)),
        out_shape=jax.ShapeDtypeStruct((t, d), F32),
        scratch_shapes=[pltpu.VMEM((2, n_k, tm, d // 2), jnp.uint32), pltpu.SemaphoreType.DMA((2,))],
        compiler_params=_cparams("arbitrary"),
        name="combine",
    )(pos_pair, sh, wts, ys)


def _moe_layer(h, w_r, b_r, w_gu, w_down, s_gu, s_down):
    t = h.shape[0]
    eidx, wts, rank, cnt = _router(h, w_r, b_r)
    start = jnp.cumsum(cnt) - cnt
    pos = start[eidx] + rank
    tm = _pick(TOP_K * t, EXPERT_TILE)
    items = _expert_items(cnt, tm, TOP_K * t // tm)
    xs = _dispatch(_packh(h), pos)
    ys = _experts(xs, items, w_gu.astype(BF), w_down.astype(BF), tm)
    sh = _shared(h, s_gu.astype(BF), s_down.astype(BF))
    return _combine(ys, pos, wts.T, sh)


def kernel(x, c, ctx, c_ctx, w_mod, b_mod, norm_g, hgrn_w_in, hgrn_lb_logits, hgrn_norm_g, hgrn_w_out,
           pool_w, pool_scale, router_w, router_bias, exp_w_gu, exp_w_down, shared_w_gu, shared_w_down):
    batch, seq, d = x.shape
    assert batch == 1 and c.shape[0] == 1
    depth = w_mod.shape[0]
    assert depth == 2 and w_mod.shape[2] == N_MOD * d
    n_heads = hgrn_w_in.shape[2] // 5 // HEAD_DIM
    x0 = x[0]

    cvecs = jnp.concatenate([c, c_ctx[None], jnp.zeros((6, d), F32)], axis=0)
    mods = _mods(cvecs, w_mod, b_mod)
    mod = lambda layer, who, j: mods[layer, who, j * d:(j + 1) * d]

    w_in = hgrn_w_in[0].astype(BF)
    hx = _normmod(x0, norm_g[0, 0], mod(0, 0, 0), mod(0, 0, 1))
    hc = _normmod(ctx[0], norm_g[0, 0], mod(0, 1, 0), mod(0, 1, 1))
    p = _mm_heads(hx, w_in)
    pc = _mm_heads(hc, w_in)
    n_lb = hgrn_lb_logits.shape[1]
    lbl = hgrn_lb_logits.astype(F32).reshape(2, n_lb, n_heads, HEAD_DIM).transpose(0, 2, 1, 3)
    zero_state = jnp.zeros((n_heads, HEAD_DIM, HEAD_DIM), F32)
    scan = functools.partial(_scan, n_heads=n_heads, layer=0)
    _, s_f = scan(pc, lbl[0], zero_state, reverse=False, z_off=3 * n_heads)
    _, s_b = scan(pc, lbl[1], zero_state, reverse=True, z_off=4 * n_heads)
    o_f, _ = scan(p, lbl[0], s_f, reverse=False, z_off=3 * n_heads)
    o_b, _ = scan(p, lbl[1], s_b, reverse=True, z_off=4 * n_heads)
    a = _gate(o_f, o_b, p, hgrn_norm_g[0])
    y = _mm(a, hgrn_w_out[0].astype(BF), F32)
    x1, h = _resid(x0, y, mod(0, 0, 2), norm_g[0, 1], (norm_g[0, 2], mod(0, 0, 3), mod(0, 0, 4)))
    f = _moe_layer(h, router_w[0], router_bias[0], exp_w_gu[0], exp_w_down[0], shared_w_gu[0], shared_w_down[0])
    x2, h = _resid(x1, f, mod(0, 0, 5), norm_g[0, 3], (norm_g[1, 0], mod(1, 0, 0), mod(1, 0, 1)))

    y = _pool(h, pool_w[0].astype(BF), pool_scale[0])
    x3, h = _resid(x2, y, mod(1, 0, 2), norm_g[1, 1], (norm_g[1, 2], mod(1, 0, 3), mod(1, 0, 4)))
    f = _moe_layer(h, router_w[1], router_bias[1], exp_w_gu[1], exp_w_down[1], shared_w_gu[1], shared_w_down[1])
    x4 = _resid(x3, f, mod(1, 0, 5), norm_g[1, 3])
    return x4[None]
```

```python
import functools

import jax
import jax.numpy as jnp
import numpy as np
from jax import lax
from jax.experimental import pallas as pl
from jax.experimental.pallas import tpu as pltpu

EPS = 1e-6
BF = jnp.bfloat16
F32 = jnp.float32
NEG_INF = float("-inf")

LANES = 128
HEAD_DIM = 128
SCAN_CHUNK = 128
SCAN_UNROLL = 4
SCAN_TILE = 1024
GRID_W = 64
POOL_WINDOWS = (2, 4, 8, 16)
N_ROUTE_GROUPS = 8
TOPK_ROUTE_GROUPS = 4
TOP_K = 8
ROUTED_SCALE = 2.5
N_MOD = 6
EXPERT_TILE = 512
VMEM_LIMIT = 56 * 1024 * 1024


def _cparams(*sem):
    return pltpu.CompilerParams(dimension_semantics=sem, vmem_limit_bytes=VMEM_LIMIT)


def _silu(v):
    return v * jax.nn.sigmoid(v)


def _rms(v):
    return v * lax.rsqrt(jnp.mean(v * v, axis=-1, keepdims=True) + EPS)


def _split_bf16(v):
    hi = v.astype(BF)
    lo = (v - hi.astype(F32)).astype(BF)
    return hi, lo


def _pick(n, want):
    t = min(n, want)
    while n % t:
        t //= 2
    assert t >= 1
    return t


def _mods_kernel(s_ref, w_ref, b_ref, o_ref):
    s = _silu(s_ref[...])
    o_ref[0] = jnp.dot(s.astype(BF), w_ref[0].astype(BF), preferred_element_type=F32) + b_ref[0]


def _mods(cvecs, w_mod, b_mod):
    depth, d, n = w_mod.shape
    tn = _pick(n, 512)
    return pl.pallas_call(
        _mods_kernel,
        grid=(depth, n // tn),
        in_specs=[
            pl.BlockSpec((8, d), lambda l, j: (0, 0)),
            pl.BlockSpec((1, d, tn), lambda l, j: (l, 0, j)),
            pl.BlockSpec((1, 1, tn), lambda l, j: (l, 0, j)),
        ],
        out_specs=pl.BlockSpec((1, 8, tn), lambda l, j: (l, 0, j)),
        out_shape=jax.ShapeDtypeStruct((depth, 8, n), F32),
        compiler_params=_cparams("parallel", "parallel"),
        name="mods",
    )(cvecs, w_mod, b_mod.reshape(depth, 1, n))


def _normmod_kernel(x_ref, g_ref, sh_ref, sc_ref, o_ref):
    y = _rms(x_ref[...]) * g_ref[...]
    o_ref[...] = (y * (1 + sc_ref[...]) + sh_ref[...]).astype(o_ref.dtype)


def _normmod(x, g, shift, scale):
    m, d = x.shape
    tm = _pick(m, 256)
    vec = pl.BlockSpec((1, d), lambda i: (0, 0))
    return pl.pallas_call(
        _normmod_kernel,
        grid=(m // tm,),
        in_specs=[pl.BlockSpec((tm, d), lambda i: (i, 0)), vec, vec, vec],
        out_specs=pl.BlockSpec((tm, d), lambda i: (i, 0)),
        out_shape=jax.ShapeDtypeStruct((m, d), BF),
        compiler_params=_cparams("parallel"),
        name="normmod",
    )(x, g.reshape(1, d), shift.reshape(1, d), scale.reshape(1, d))


def _resid_kernel(x_ref, y_ref, gate_ref, gy_ref, xo_ref):
    yn = _rms(y_ref[...].astype(F32)) * gy_ref[...]
    xo_ref[...] = x_ref[...] + gate_ref[...] * yn


def _resid_norm_kernel(x_ref, y_ref, gate_ref, gy_ref, gx_ref, sh_ref, sc_ref, xo_ref, ho_ref):
    yn = _rms(y_ref[...].astype(F32)) * gy_ref[...]
    xn = x_ref[...] + gate_ref[...] * yn
    xo_ref[...] = xn
    hn = _rms(xn) * gx_ref[...]
    ho_ref[...] = (hn * (1 + sc_ref[...]) + sh_ref[...]).astype(ho_ref.dtype)


def _resid(x, y, gate, gy, nxt=None):
    m, d = x.shape
    tm = _pick(m, 256)
    row = pl.BlockSpec((tm, d), lambda i: (i, 0))
    vec = pl.BlockSpec((1, d), lambda i: (0, 0))
    v = lambda a: a.reshape(1, d)
    if nxt is None:
        return pl.pallas_call(
            _resid_kernel,
            grid=(m // tm,),
            in_specs=[row, row, vec, vec],
            out_specs=row,
            out_shape=jax.ShapeDtypeStruct((m, d), F32),
            compiler_params=_cparams("parallel"),
            name="resid",
        )(x, y, v(gate), v(gy))
    gx, shift, scale = nxt
    return pl.pallas_call(
        _resid_norm_kernel,
        grid=(m // tm,),
        in_specs=[row, row, vec, vec, vec, vec, vec],
        out_specs=[row, row],
        out_shape=[jax.ShapeDtypeStruct((m, d), F32), jax.ShapeDtypeStruct((m, d), BF)],
        compiler_params=_cparams("parallel"),
        name="resid_norm",
    )(x, y, v(gate), v(gy), v(gx), v(shift), v(scale))


def _mm_kernel(a_ref, w_ref, o_ref):
    o_ref[...] = jnp.dot(a_ref[...], w_ref[...], preferred_element_type=F32).astype(o_ref.dtype)


def _mm(a, w, out_dtype):
    m, k = a.shape
    n = w.shape[1]
    tm, tn = _pick(m, 1024), _pick(n, 1024)
    return pl.pallas_call(
        _mm_kernel,
        grid=(m // tm, n // tn),
        in_specs=[pl.BlockSpec((tm, k), lambda i, j: (i, 0)), pl.BlockSpec((k, tn), lambda i, j: (0, j))],
        out_specs=pl.BlockSpec((tm, tn), lambda i, j: (i, j)),
        out_shape=jax.ShapeDtypeStruct((m, n), out_dtype),
        compiler_params=_cparams("parallel", "parallel"),
        name="mm",
    )(a, w)


def _mm_heads_kernel(a_ref, w_ref, o_ref):
    r = jnp.dot(a_ref[...], w_ref[...], preferred_element_type=F32)
    for hh in range(o_ref.shape[0]):
        o_ref[hh] = r[:, hh * LANES:(hh + 1) * LANES].astype(o_ref.dtype)


def _mm_heads(a, w):
    m, k = a.shape
    n = w.shape[1]
    tm, tn = _pick(m, 1024), _pick(n, 1024)
    return pl.pallas_call(
        _mm_heads_kernel,
        grid=(m // tm, n // tn),
        in_specs=[pl.BlockSpec((tm, k), lambda i, j: (i, 0)), pl.BlockSpec((k, tn), lambda i, j: (0, j))],
        out_specs=pl.BlockSpec((tn // LANES, tm, LANES), lambda i, j: (j, i, 0)),
        out_shape=jax.ShapeDtypeStruct((n // LANES, m, LANES), BF),
        compiler_params=_cparams("parallel", "parallel"),
        name="proj",
    )(a, w)


def _scan_tables(c, reverse):
    n_levels = c.bit_length() - 1
    assert c == 1 << n_levels
    t = np.arange(c)[:, None]
    u = np.arange(c)[None, :]
    dmats = [u <= t, u > t]
    masks = [u == t]
    roles = []
    for lv in range(n_levels):
        size = c >> lv
        start = (t // size) * size
        boundary = start + size // 2 - 1
        later = t > boundary
        dmats.append(np.where(later, (u > boundary) & (u <= t), (u > t) & (u <= boundary)))
        masks.append(later & (u <= boundary.T) & (start == start.T))
        roles.append(np.broadcast_to(later, (c, HEAD_DIM)))
    dmat, mask, role = np.stack(dmats), np.stack(masks), np.stack(roles)
    if reverse:
        dmat, mask, role = dmat[:, ::-1, ::-1], mask[:, ::-1, ::-1], role[:, ::-1]
    return (jnp.asarray(dmat.reshape(-1, c), BF), jnp.asarray(mask, F32), jnp.asarray(role, F32))


def _scan_kernel(q_ref, v_ref, z_ref, lbl_ref, s0_ref, dmat_ref, mask_ref, roles_ref, o_ref, sfin_ref,
                 st_ref, *, layer, reverse):
    c = SCAN_CHUNK
    n_levels = roles_ref.shape[0]
    t_tile = q_ref.shape[1]
    span = min(SCAN_UNROLL * c, t_tile)
    n_spans = t_tile // span
    i = pl.program_id(1)

    @pl.when(i == 0)
    def _():
        st_ref[...] = s0_ref[0]

    lg = lbl_ref[0]
    e = jnp.exp(lg - jnp.max(lg, axis=0, keepdims=True))
    p = e / jnp.sum(e, axis=0, keepdims=True)
    lb = jnp.sum(p[:layer + 1], axis=0, keepdims=True)

    nt_dims = (((1,), (1,)), ((), ()))
    tn_dims = (((0,), (0,)), ((), ()))
    last = 0 if reverse else c - 1

    def chunk(st, lf, qc, kc, vc):
        ex = jnp.dot(dmat_ref[...], lf.astype(BF), preferred_element_type=F32)
        b = ex[0:c]
        qb, kb, vb = qc.astype(BF), kc.astype(BF), vc.astype(BF)
        att = jnp.where(mask_ref[0] > 0, lax.dot_general(qb, kb, nt_dims, preferred_element_type=F32), 0.0)
        for lv in range(n_levels):
            x = jnp.where(roles_ref[lv] > 0, qc, kc) * jnp.exp(ex[(2 + lv) * c:(3 + lv) * c])
            xb = x.astype(BF)
            gram = lax.dot_general(xb, xb, nt_dims, preferred_element_type=F32)
            att = att + jnp.where(mask_ref[1 + lv] > 0, gram, 0.0)
        qd = (qc * jnp.exp(b)).astype(BF)
        o = (jnp.dot(att.astype(BF), vb, preferred_element_type=F32)
             + lax.dot_general(qd, st.astype(BF), nt_dims, preferred_element_type=F32))
        kd = (kc * jnp.exp(ex[c:2 * c])).astype(BF)
        st_new = st * jnp.exp(b[last:last + 1]) + lax.dot_general(vb, kd, tn_dims, preferred_element_type=F32)
        return st_new, o

    def block(n, st):
        j = (n_spans - 1 - n) if reverse else n
        rows = pl.ds(pl.multiple_of(j * span, span), span)
        z = z_ref[0, rows, :].astype(F32)
        f = lb + (1 - lb) * jax.nn.sigmoid(z)
        logf = jnp.log(f)
        kk = 1 - f
        q = _silu(q_ref[0, rows, :].astype(F32))
        v = v_ref[0, rows, :].astype(F32)
        n_c = span // c
        outs = [None] * n_c
        for m in (range(n_c - 1, -1, -1) if reverse else range(n_c)):
            sl = slice(m * c, (m + 1) * c)
            st, outs[m] = chunk(st, logf[sl], q[sl], kk[sl], v[sl])
        o_ref[0, rows, :] = jnp.concatenate(outs, axis=0).astype(o_ref.dtype)
        return st

    st_ref[...] = lax.fori_loop(0, n_spans, block, st_ref[...])

    @pl.when(i == pl.num_programs(1) - 1)
    def _():
        sfin_ref[0] = st_ref[...]


def _scan(p, lb_logits, s0, *, n_heads, layer, reverse, z_off):
    _, l, _ = p.shape
    t_tile = _pick(l, SCAN_TILE)
    assert t_tile % SCAN_CHUNK == 0
    nt = l // t_tile
    tok = (lambda i: nt - 1 - i) if reverse else (lambda i: i)
    blk = lambda off: pl.BlockSpec((1, t_tile, HEAD_DIM), lambda h, i: (off + h, tok(i), 0))
    n_lb = lb_logits.shape[1]
    dmat, mask, roles = _scan_tables(SCAN_CHUNK, reverse)
    whole = lambda a: pl.BlockSpec(a.shape, lambda h, i: (0,) * a.ndim)
    return pl.pallas_call(
        functools.partial(_scan_kernel, layer=layer, reverse=reverse),
        grid=(n_heads, nt),
        in_specs=[
            blk(0), blk(2 * n_heads), blk(z_off),
            pl.BlockSpec((1, n_lb, HEAD_DIM), lambda h, i: (h, 0, 0)),
            pl.BlockSpec((1, HEAD_DIM, HEAD_DIM), lambda h, i: (h, 0, 0)),
            whole(dmat), whole(mask), whole(roles),
        ],
        out_specs=[
            pl.BlockSpec((1, t_tile, HEAD_DIM), lambda h, i: (h, tok(i), 0)),
            pl.BlockSpec((1, HEAD_DIM, HEAD_DIM), lambda h, i: (h, 0, 0)),
        ],
        out_shape=[
            jax.ShapeDtypeStruct((n_heads, l, HEAD_DIM), BF),
            jax.ShapeDtypeStruct((n_heads, HEAD_DIM, HEAD_DIM), F32),
        ],
        scratch_shapes=[pltpu.VMEM((HEAD_DIM, HEAD_DIM), F32)],
        compiler_params=_cparams("parallel", "arbitrary"),
        name="scan_bwd" if reverse else "scan_fwd",
    )(p, p, p, lb_logits, s0, dmat, mask, roles)


def _gate_kernel(of_ref, ob_ref, g_ref, ng_ref, a_ref):
    for h in range(of_ref.shape[0]):
        o = of_ref[h].astype(F32) + ob_ref[h].astype(F32)
        o = _rms(o) * ng_ref[h]
        a_ref[:, h * HEAD_DIM:(h + 1) * HEAD_DIM] = (o * _silu(g_ref[h].astype(F32))).astype(a_ref.dtype)


def _gate(o_f, o_b, p, norm_g):
    n_heads, l, _ = o_f.shape
    tm = _pick(l, 256)
    blk = lambda off: pl.BlockSpec((n_heads, tm, HEAD_DIM), lambda i: (off, i, 0))
    return pl.pallas_call(
        _gate_kernel,
        grid=(l // tm,),
        in_specs=[blk(0), blk(0), blk(1), pl.BlockSpec((n_heads, 1, HEAD_DIM), lambda i: (0, 0, 0))],
        out_specs=pl.BlockSpec((tm, n_heads * HEAD_DIM), lambda i: (i, 0)),
        out_shape=jax.ShapeDtypeStruct((l, n_heads * HEAD_DIM), BF),
        compiler_params=_cparams("parallel"),
        name="gate",
    )(o_f, o_b, p, norm_g.reshape(n_heads, 1, HEAD_DIM))


def _pool_body(win, prev_ref, cur_ref, next_ref, w_ref, scale_ref, o_ref, y_ref, d_ref, n_rows):
    i = pl.program_id(0)
    tile_rows = cur_ref.shape[0] // GRID_W
    halo = y_ref.shape[0] - tile_rows
    top = halo // 2
    half = win // 2
    r0 = i * tile_rows

    t_io = lax.broadcasted_iota(jnp.int32, (GRID_W, GRID_W), 0)
    s_io = lax.broadcasted_iota(jnp.int32, (GRID_W, GRID_W), 1)
    lo = jnp.maximum(t_io - half, 0)
    hi = jnp.minimum(t_io + win - half, GRID_W)
    a_c = jnp.where((s_io >= lo) & (s_io < hi), 1.0 / (hi - lo).astype(F32), 0.0)
    a_hi, a_lo = _split_bf16(a_c)

    def col_filter(src_ref, src_row, dst_row):
        xs = src_ref[pl.ds(pl.multiple_of(src_row * GRID_W, GRID_W), GRID_W), :]
        y = (jnp.dot(a_hi, xs, preferred_element_type=F32) + jnp.dot(a_lo, xs, preferred_element_type=F32))
        grow = jnp.full((1, y.shape[1]), r0 - top + dst_row, jnp.int32)
        ok = (grow >= 0) & (grow < n_rows)
        y_ref[dst_row] = jnp.where(ok, y, 0.0)

    def above(j, c):
        col_filter(prev_ref, tile_rows - half + j, top - half + j)
        return c

    def inside(j, c):
        col_filter(cur_ref, j, top + j)
        return c

    def below(j, c):
        col_filter(next_ref, j, top + tile_rows + j)
        return c

    lax.fori_loop(0, half, above, 0)
    lax.fori_loop(0, tile_rows, inside, 0)
    lax.fori_loop(0, half - 1, below, 0)

    def row_filter(r, c):
        grow = jnp.full((1, d_ref.shape[1]), r0 + r, jnp.int32)
        cnt = jnp.minimum(grow + win - half, n_rows) - jnp.maximum(grow - half, 0)
        z = y_ref[top + r - half]
        for j in range(1, win):
            z = z + y_ref[top + r - half + j]
        z = z * (1.0 / cnt.astype(F32))
        sl = pl.ds(pl.multiple_of(r * GRID_W, GRID_W), GRID_W)
        d_ref[sl, :] = (z - cur_ref[sl, :].astype(F32)).astype(d_ref.dtype)
        return c

    lax.fori_loop(0, tile_rows, row_filter, 0)
    o_ref[...] = jnp.dot(d_ref[...], w_ref[0], preferred_element_type=F32) * scale_ref[...]


def _pool_kernel(prev_ref, cur_ref, next_ref, w_ref, scale_ref, o_ref, y_ref, d_ref, *, n_rows):
    g = pl.program_id(1)
    for gi, win in enumerate(POOL_WINDOWS):
        @pl.when(g == gi)
        def _(win=win):
            _pool_body(win, prev_ref, cur_ref, next_ref, w_ref, scale_ref, o_ref, y_ref, d_ref, n_rows)


def _pool(h, w_pool, scale):
    l, d = h.shape
    n_groups, gc, _ = w_pool.shape
    n_rows = l // GRID_W
    tile_rows = _pick(n_rows, 16)
    halo = max(POOL_WINDOWS)
    assert tile_rows >= halo // 2 and len(POOL_WINDOWS) == n_groups
    tt = tile_rows * GRID_W
    nt = l // tt
    return pl.pallas_call(
        functools.partial(_pool_kernel, n_rows=n_rows),
        grid=(nt, n_groups),
        in_specs=[
            pl.BlockSpec((tt, gc), lambda i, g: (jnp.maximum(i - 1, 0), g)),
            pl.BlockSpec((tt, gc), lambda i, g: (i, g)),
            pl.BlockSpec((tt, gc), lambda i, g: (jnp.minimum(i + 1, nt - 1), g)),
            pl.BlockSpec((1, gc, gc), lambda i, g: (g, 0, 0)),
            pl.BlockSpec((1, gc), lambda i, g: (0, g)),
        ],
        out_specs=pl.BlockSpec((tt, gc), lambda i, g: (i, g)),
        out_shape=jax.ShapeDtypeStruct((l, d), F32),
        scratch_shapes=[pltpu.VMEM((tile_rows + halo, GRID_W, gc), F32), pltpu.VMEM((tt, gc), BF)],
        compiler_params=_cparams("parallel", "parallel"),
        name="pool",
    )(h, h, h, w_pool, scale.reshape(1, d))


def _router_kernel(h_ref, wrt_ref, bias_ref, upper_ref, eidx_ref, wts_ref, rank_ref, cnt_ref, carry_ref):
    n_exp = wrt_ref.shape[0]
    tm = h_ref.shape[0]

    @pl.when(pl.program_id(0) == 0)
    def _():
        carry_ref[...] = jnp.zeros_like(carry_ref)

    epg = n_exp // N_ROUTE_GROUPS
    logits = lax.dot_general(wrt_ref[...], h_ref[...], (((1,), (1,)), ((), ())), preferred_element_type=F32)
    scores = jax.nn.sigmoid(logits)
    shape3 = (N_ROUTE_GROUPS, epg, tm)
    sc3 = scores.reshape(shape3)
    s3 = (scores + bias_ref[...]).reshape(shape3)
    e_in = lax.broadcasted_iota(jnp.int32, shape3, 1).astype(F32)
    g_io = lax.broadcasted_iota(jnp.int32, shape3, 0).astype(F32)

    m1 = jnp.max(s3, axis=1, keepdims=True)
    i1 = jnp.min(jnp.where(s3 == m1, e_in, epg), axis=1, keepdims=True)
    m2 = jnp.max(jnp.where(e_in == i1, NEG_INF, s3), axis=1, keepdims=True)
    gs = jnp.broadcast_to(m1 + m2, shape3)

    keep = jnp.zeros(shape3, F32)
    cur = gs
    for _ in range(TOPK_ROUTE_GROUPS):
        m = jnp.max(cur, axis=0, keepdims=True)
        idx = jnp.min(jnp.where(cur == m, g_io, N_ROUTE_GROUPS), axis=0, keepdims=True)
        hit = g_io == idx
        keep = jnp.where(hit, 1.0, keep)
        cur = jnp.where(hit, NEG_INF, cur)

    e_io = g_io * epg + e_in
    chosen = jnp.zeros(shape3, F32)
    cur = jnp.where(keep > 0, s3, NEG_INF)
    picks = []
    for _ in range(TOP_K):
        m = jnp.max(jnp.max(cur, axis=0, keepdims=True), axis=1, keepdims=True)
        idx = jnp.where(cur == m, e_io, n_exp)
        idx = jnp.min(jnp.min(idx, axis=0, keepdims=True), axis=1, keepdims=True)
        hit = e_io == idx
        chosen = jnp.where(hit, 1.0, chosen)
        cur = jnp.where(hit, NEG_INF, cur)
        picks.append(idx)

    w = jnp.where(chosen > 0, sc3, 0.0)
    den = jnp.sum(jnp.sum(w, axis=0, keepdims=True), axis=1, keepdims=True)
    w = w / den * ROUTED_SCALE

    ch = chosen.reshape(n_exp, tm).astype(BF)
    carry = carry_ref[...]
    rank = (jnp.dot(ch, upper_ref[...], preferred_element_type=F32)
            + jnp.concatenate([carry] * (tm // LANES), axis=1)).reshape(shape3)
    carry = carry + jnp.dot(ch, jnp.ones((tm, LANES), BF), preferred_element_type=F32)
    carry_ref[...] = carry
    cnt_ref[...] = carry

    def per_pick(vals, idx):
        sel = jnp.where(e_io == idx, vals, 0.0)
        return jnp.sum(jnp.sum(sel, axis=0, keepdims=True), axis=1, keepdims=True).reshape(1, tm)

    eidx_ref[...] = jnp.concatenate([idx.reshape(1, tm) for idx in picks], axis=0).astype(jnp.int32)
    wts_ref[...] = jnp.concatenate([per_pick(w, idx) for idx in picks], axis=0)
    rank_ref[...] = jnp.concatenate([per_pick(rank, idx) for idx in picks], axis=0).astype(jnp.int32)


def _router(h, w_r, b_r):
    t, d = h.shape
    n_exp = w_r.shape[1]
    tm = _pick(t, 512)
    assert tm % LANES == 0
    upper = jnp.asarray(np.triu(np.ones((tm, tm), np.float32), 1), BF)
    pick = pl.BlockSpec((TOP_K, tm), lambda i: (0, i))
    eidx, wts, rank, cnt = pl.pallas_call(
        _router_kernel,
        grid=(t // tm,),
        in_specs=[
            pl.BlockSpec((tm, d), lambda i: (i, 0)),
            pl.BlockSpec((n_exp, d), lambda i: (0, 0)),
            pl.BlockSpec((n_exp, 1), lambda i: (0, 0)),
            pl.BlockSpec((tm, tm), lambda i: (0, 0)),
        ],
        out_specs=[pick, pick, pick, pl.BlockSpec((n_exp, LANES), lambda i: (0, 0))],
        out_shape=[
            jax.ShapeDtypeStruct((TOP_K, t), jnp.int32),
            jax.ShapeDtypeStruct((TOP_K, t), F32),
            jax.ShapeDtypeStruct((TOP_K, t), jnp.int32),
            jax.ShapeDtypeStruct((n_exp, LANES), F32),
        ],
        scratch_shapes=[pltpu.VMEM((n_exp, LANES), F32)],
        compiler_params=_cparams("arbitrary"),
        name="router",
    )(h, w_r.T.astype(BF), b_r.astype(F32).reshape(n_exp, 1), upper)
    return eidx, wts, rank, cnt[:, 0].astype(jnp.int32)


HI_MASK = 0xFFFF0000


def _pack_rows(v):
    half = v.shape[1] // 2
    bits = lambda a: pltpu.bitcast(a.astype(BF).astype(F32), jnp.uint32)
    return lax.shift_right_logical(bits(v[:, :half]), jnp.uint32(16)) | (bits(v[:, half:]) & jnp.uint32(HI_MASK))


def _unpack_rows(w):
    lo = pltpu.bitcast(lax.shift_left(w, jnp.uint32(16)), F32)
    hi = pltpu.bitcast(w & jnp.uint32(HI_MASK), F32)
    return lo, hi


def _packh_kernel(h_ref, o_ref):
    o_ref[...] = _pack_rows(h_ref[...].astype(F32))


def _packh(h):
    t, d = h.shape
    tm = _pick(t, 512)
    return pl.pallas_call(
        _packh_kernel,
        grid=(t // tm,),
        in_specs=[pl.BlockSpec((tm, d), lambda i: (i, 0))],
        out_specs=pl.BlockSpec((tm, d // 2), lambda i: (i, 0)),
        out_shape=jax.ShapeDtypeStruct((t, d // 2), jnp.uint32),
        compiler_params=_cparams("parallel"),
        name="packh",
    )(h)


def _dispatch_kernel(pos_ref, h_ref, xs_hbm, sem):
    n_k, tm = pos_ref.shape

    def issue(t, c):
        for k in range(n_k):
            pltpu.make_async_copy(h_ref.at[pl.ds(t, 1)], xs_hbm.at[pl.ds(pos_ref[k, t], 1)], sem.at[0]).start()
        return c

    lax.fori_loop(0, tm, issue, 0, unroll=8)

    def drain(t, c):
        for k in range(n_k):
            pltpu.make_async_copy(h_ref.at[pl.ds(0, 1)], xs_hbm.at[pl.ds(0, 1)], sem.at[0]).wait()
        return c

    lax.fori_loop(0, tm, drain, 0, unroll=8)


def _dispatch(hp, pos):
    t, dw = hp.shape
    n_k = pos.shape[0]
    tm = _pick(t, 256)
    return pl.pallas_call(
        _dispatch_kernel,
        grid=(t // tm,),
        in_specs=[
            pl.BlockSpec((n_k, tm), lambda i: (0, i), memory_space=pltpu.SMEM),
            pl.BlockSpec((tm, dw), lambda i: (i, 0)),
        ],
        out_specs=pl.BlockSpec(memory_space=pl.ANY),
        out_shape=jax.ShapeDtypeStruct((n_k * t, dw), jnp.uint32),
        scratch_shapes=[pltpu.SemaphoreType.DMA((1,))],
        compiler_params=_cparams("arbitrary"),
        name="dispatch",
    )(pos, hp)


def _experts_kernel(tile_ref, exp_ref, lo_ref, hi_ref, xs_ref, wgu_ref, wd_ref, ys_ref,
                    acc_ref, wgu_bf_ref, wd_bf_ref):
    w = pl.program_id(0)
    prev = jnp.maximum(w - 1, 0)
    tm = xs_ref.shape[0]
    half = xs_ref.shape[1]
    d_exp = wd_ref.shape[0]

    @pl.when(jnp.logical_or(w == 0, exp_ref[w] != exp_ref[prev]))
    def _():
        wgu_bf_ref[...] = wgu_ref[...].astype(BF)
        wd_bf_ref[...] = wd_ref[...].astype(BF)

    @pl.when(jnp.logical_or(w == 0, tile_ref[w] != tile_ref[prev]))
    def _():
        acc_ref[...] = jnp.zeros_like(acc_ref)

    @pl.when(hi_ref[w] > lo_ref[w])
    def _():
        x_lo, x_hi = _unpack_rows(xs_ref[...])
        gu = (jnp.dot(x_lo.astype(BF), wgu_bf_ref[:half, :], preferred_element_type=F32)
              + jnp.dot(x_hi.astype(BF), wgu_bf_ref[half:, :], preferred_element_type=F32))
        row = lax.broadcasted_iota(jnp.int32, (tm, d_exp), 0)
        mine = (row >= lo_ref[w]) & (row < hi_ref[w])
        act = jnp.where(mine, _silu(gu[:, :d_exp]) * gu[:, d_exp:], 0.0)
        acc_ref[...] += jnp.dot(act.astype(BF), wd_bf_ref[...], preferred_element_type=F32)

    ys_ref[...] = _pack_rows(acc_ref[...])


def _experts(xs, items, w_gu, w_down, layer, tm):
    p, dw = xs.shape
    _, _, d, f2 = w_gu.shape
    n_items = items[0].shape[0]
    grid_spec = pltpu.PrefetchScalarGridSpec(
        num_scalar_prefetch=4,
        grid=(n_items,),
        in_specs=[
            pl.BlockSpec((tm, dw), lambda w, tile, exp, lo, hi: (tile[w], 0)),
            pl.BlockSpec((None, None, d, f2), lambda w, tile, exp, lo, hi: (layer, exp[w], 0, 0)),
            pl.BlockSpec((None, None, f2 // 2, d), lambda w, tile, exp, lo, hi: (layer, exp[w], 0, 0)),
        ],
        out_specs=pl.BlockSpec((tm, dw), lambda w, tile, exp, lo, hi: (tile[w], 0)),
        scratch_shapes=[pltpu.VMEM((tm, d), F32), pltpu.VMEM((d, f2), BF), pltpu.VMEM((f2 // 2, d), BF)],
    )
    return pl.pallas_call(
        _experts_kernel,
        grid_spec=grid_spec,
        out_shape=jax.ShapeDtypeStruct((p, dw), jnp.uint32),
        compiler_params=_cparams("arbitrary"),
        name="experts",
    )(*items, xs, w_gu, w_down)


def _expert_items(cnt, tm, n_tiles):
    n_exp = cnt.shape[0]
    n_items = n_tiles + n_exp - 1
    end = jnp.cumsum(cnt)
    start = end - cnt
    first_tile = start // tm
    n_e = jnp.where(cnt > 0, (end - 1) // tm - first_tile + 1, 0)
    item_end = jnp.cumsum(n_e)
    w = jnp.arange(n_items, dtype=jnp.int32)
    e = jnp.minimum(jnp.searchsorted(item_end, w, side="right"), n_exp - 1).astype(jnp.int32)
    valid = w < item_end[-1]
    tile = jnp.where(valid, first_tile[e] + (w - (item_end[e] - n_e[e])), n_tiles - 1).astype(jnp.int32)
    lo = jnp.where(valid, jnp.maximum(start[e], tile * tm) - tile * tm, 0).astype(jnp.int32)
    hi = jnp.where(valid, jnp.minimum(end[e], (tile + 1) * tm) - tile * tm, 0).astype(jnp.int32)
    e = jnp.where(valid, e, e[jnp.maximum(item_end[-1] - 1, 0)])
    return tile, e, lo, hi


def _shared_kernel(h_ref, wgu_ref, wd_ref, o_ref):
    d_exp = wd_ref.shape[0]
    gu = jnp.dot(h_ref[...], wgu_ref[...], preferred_element_type=F32)
    act = _silu(gu[:, :d_exp]) * gu[:, d_exp:]
    o_ref[...] = jnp.dot(act.astype(BF), wd_ref[...], preferred_element_type=F32)


def _shared(h, w_gu, w_down):
    t, d = h.shape
    f2 = w_gu.shape[1]
    tm = _pick(t, 512)
    return pl.pallas_call(
        _shared_kernel,
        grid=(t // tm,),
        in_specs=[
            pl.BlockSpec((tm, d), lambda i: (i, 0)),
            pl.BlockSpec((d, f2), lambda i: (0, 0)),
            pl.BlockSpec((f2 // 2, d), lambda i: (0, 0)),
        ],
        out_specs=pl.BlockSpec((tm, d), lambda i: (i, 0)),
        out_shape=jax.ShapeDtypeStruct((t, d), F32),
        compiler_params=_cparams("parallel"),
        name="shared",
    )(h, w_gu, w_down)


def _combine_kernel(pos_ref, sh_ref, wts_ref, ys_hbm, o_ref, buf_ref, sem):
    i = pl.program_id(0)
    n = pl.num_programs(0)
    n_k, tm = pos_ref.shape[1], pos_ref.shape[2]
    half = buf_ref.shape[3]

    def row_copy(j, t, k, slot):
        return pltpu.make_async_copy(ys_hbm.at[pl.ds(pos_ref[j, k, t], 1)], buf_ref.at[slot, k, pl.ds(t, 1)],
                                     sem.at[slot])

    def gather(j, slot):
        def issue(t, c):
            for k in range(n_k):
                row_copy(j, t, k, slot).start()
            return c
        lax.fori_loop(0, tm, issue, 0, unroll=8)

    @pl.when(i == 0)
    def _():
        gather(0, 0)

    @pl.when(i + 1 < n)
    def _():
        gather(1, (i + 1) % 2)

    slot = i % 2

    def drain(t, c):
        for k in range(n_k):
            pltpu.make_async_copy(ys_hbm.at[pl.ds(0, 1)], buf_ref.at[slot, 0, pl.ds(0, 1)], sem.at[slot]).wait()
        return c

    lax.fori_loop(0, tm, drain, 0, unroll=8)

    acc_lo = sh_ref[:, :half]
    acc_hi = sh_ref[:, half:]
    for k in range(n_k):
        y_lo, y_hi = _unpack_rows(buf_ref[slot, k])
        wk = wts_ref[:, k:k + 1]
        acc_lo = acc_lo + wk * y_lo
        acc_hi = acc_hi + wk * y_hi
    o_ref[:, :half] = acc_lo
    o_ref[:, half:] = acc_hi


def _combine(ys, pos, wts, sh):
    t, d = sh.shape
    n_k = pos.shape[0]
    tm = _pick(t, 64)
    nt = t // tm
    pos3 = pos.reshape(n_k, nt, tm).transpose(1, 0, 2)
    pos_pair = jnp.stack([pos3, jnp.concatenate([pos3[1:], pos3[-1:]], axis=0)], axis=1)
    return pl.pallas_call(
        _combine_kernel,
        grid=(nt,),
        in_specs=[
            pl.BlockSpec((None, 2, n_k, tm), lambda i: (i, 0, 0, 0), memory_space=pltpu.SMEM),
            pl.BlockSpec((tm, d), lambda i: (i, 0)),
            pl.BlockSpec((tm, n_k), lambda i: (i, 0)),
            pl.BlockSpec(memory_space=pl.ANY),
        ],
        out_specs=pl.BlockSpec((tm, d), lambda i: (i, 0)),
        out_shape=jax.ShapeDtypeStruct((t, d), F32),
        scratch_shapes=[pltpu.VMEM((2, n_k, tm, d // 2), jnp.uint32), pltpu.SemaphoreType.DMA((2,))],
        compiler_params=_cparams("arbitrary"),
        name="combine",
    )(pos_pair, sh, wts, ys)


def _moe_layer(h, layer, router_w, router_bias, exp_w_gu, exp_w_down, shared_w_gu, shared_w_down):
    t = h.shape[0]
    s_gu, s_down = shared_w_gu[layer], shared_w_down[layer]
    eidx, wts, rank, cnt = _router(h, router_w[layer], router_bias[layer])
    start = jnp.cumsum(cnt) - cnt
    n_exp = cnt.shape[0]
    is_e = eidx[None] == jnp.arange(n_exp, dtype=jnp.int32)[:, None, None]
    pos = jnp.sum(jnp.where(is_e, start[:, None, None], 0), axis=0) + rank
    tm = _pick(TOP_K * t, EXPERT_TILE)
    items = _expert_items(cnt, tm, TOP_K * t // tm)
    xs = _dispatch(_packh(h), pos)
    ys = _experts(xs, items, exp_w_gu, exp_w_down, layer, tm)
    sh = _shared(h, s_gu.astype(BF), s_down.astype(BF))
    return _combine(ys, pos, wts.T, sh)


def kernel(x, c, ctx, c_ctx, w_mod, b_mod, norm_g, hgrn_w_in, hgrn_lb_logits, hgrn_norm_g, hgrn_w_out,
           pool_w, pool_scale, router_w, router_bias, exp_w_gu, exp_w_down, shared_w_gu, shared_w_down):
    batch, seq, d = x.shape
    assert batch == 1 and c.shape[0] == 1
    depth = w_mod.shape[0]
    assert depth == 2 and w_mod.shape[2] == N_MOD * d
    n_heads = hgrn_w_in.shape[2] // 5 // HEAD_DIM
    x0 = x[0]

    cvecs = jnp.concatenate([c, c_ctx[None], jnp.zeros((6, d), F32)], axis=0)
    mods = _mods(cvecs, w_mod, b_mod)
    mod = lambda layer, who, j: mods[layer, who, j * d:(j + 1) * d]

    w_in = hgrn_w_in[0].astype(BF)
    hx = _normmod(x0, norm_g[0, 0], mod(0, 0, 0), mod(0, 0, 1))
    hc = _normmod(ctx[0], norm_g[0, 0], mod(0, 1, 0), mod(0, 1, 1))
    p = _mm_heads(hx, w_in)
    pc = _mm_heads(hc, w_in)
    n_lb = hgrn_lb_logits.shape[1]
    lbl = hgrn_lb_logits.astype(F32).reshape(2, n_lb, n_heads, HEAD_DIM).transpose(0, 2, 1, 3)
    zero_state = jnp.zeros((n_heads, HEAD_DIM, HEAD_DIM), F32)
    scan = functools.partial(_scan, n_heads=n_heads, layer=0)
    _, s_f = scan(pc, lbl[0], zero_state, reverse=False, z_off=3 * n_heads)
    _, s_b = scan(pc, lbl[1], zero_state, reverse=True, z_off=4 * n_heads)
    o_f, _ = scan(p, lbl[0], s_f, reverse=False, z_off=3 * n_heads)
    o_b, _ = scan(p, lbl[1], s_b, reverse=True, z_off=4 * n_heads)
    a = _gate(o_f, o_b, p, hgrn_norm_g[0])
    y = _mm(a, hgrn_w_out[0].astype(BF), F32)
    x1, h = _resid(x0, y, mod(0, 0, 2), norm_g[0, 1], (norm_g[0, 2], mod(0, 0, 3), mod(0, 0, 4)))
    moe = functools.partial(_moe_layer, router_w=router_w, router_bias=router_bias, exp_w_gu=exp_w_gu,
                            exp_w_down=exp_w_down, shared_w_gu=shared_w_gu, shared_w_down=shared_w_down)
    f = moe(h, 0)
    x2, h = _resid(x1, f, mod(0, 0, 5), norm_g[0, 3], (norm_g[1, 0], mod(1, 0, 0), mod(1, 0, 1)))

    y = _pool(h, pool_w[0].astype(BF), pool_scale[0])
    x3, h = _resid(x2, y, mod(1, 0, 2), norm_g[1, 1], (norm_g[1, 2], mod(1, 0, 3), mod(1, 0, 4)))
    f = moe(h, 1)
    x4 = _resid(x3, f, mod(1, 0, 5), norm_g[1, 3])
    return x4[None]
```

```python
import functools

import jax
import jax.numpy as jnp
import numpy as np
from jax import lax
from jax.experimental import pallas as pl
from jax.experimental.pallas import tpu as pltpu

EPS = 1e-6
BF = jnp.bfloat16
F32 = jnp.float32
NEG_INF = float("-inf")

LANES = 128
HEAD_DIM = 128
SCAN_CHUNK = 128
SCAN_UNROLL = 4
SCAN_TILE = 1024
SCAN_HEADS = 2
GRID_W = 64
POOL_WINDOWS = (2, 4, 8, 16)
N_ROUTE_GROUPS = 8
TOPK_ROUTE_GROUPS = 4
TOP_K = 8
ROUTED_SCALE = 2.5
N_MOD = 6
EXPERT_TILE = 512
DOWN_CHUNK = 512
VMEM_LIMIT =56 * 1024 * 1024


def _cparams(*sem):
    return pltpu.CompilerParams(dimension_semantics=sem, vmem_limit_bytes=VMEM_LIMIT)


def _silu(v):
    return v * jax.nn.sigmoid(v)


def _rms(v):
    return v * lax.rsqrt(jnp.mean(v * v, axis=-1, keepdims=True) + EPS)


def _split_bf16(v):
    hi = v.astype(BF)
    lo = (v - hi.astype(F32)).astype(BF)
    return hi, lo


def _pick(n, want):
    t = min(n, want)
    while n % t:
        t //= 2
    assert t >= 1
    return t


def _mods_kernel(s_ref, w_ref, b_ref, o_ref):
    s = _silu(s_ref[...])
    o_ref[0] = jnp.dot(s.astype(BF), w_ref[0].astype(BF), preferred_element_type=F32) + b_ref[0]


def _mods(cvecs, w_mod, b_mod):
    depth, d, n = w_mod.shape
    tn = _pick(n, 512)
    return pl.pallas_call(
        _mods_kernel,
        grid=(depth, n // tn),
        in_specs=[
            pl.BlockSpec((8, d), lambda l, j: (0, 0)),
            pl.BlockSpec((1, d, tn), lambda l, j: (l, 0, j)),
            pl.BlockSpec((1, 1, tn), lambda l, j: (l, 0, j)),
        ],
        out_specs=pl.BlockSpec((1, 8, tn), lambda l, j: (l, 0, j)),
        out_shape=jax.ShapeDtypeStruct((depth, 8, n), F32),
        compiler_params=_cparams("parallel", "parallel"),
        name="mods",
    )(cvecs, w_mod, b_mod.reshape(depth, 1, n))


def _normmod_kernel(x_ref, g_ref, sh_ref, sc_ref, o_ref):
    y = _rms(x_ref[...]) * g_ref[...]
    o_ref[...] = (y * (1 + sc_ref[...]) + sh_ref[...]).astype(o_ref.dtype)


def _normmod(x, g, shift, scale):
    m, d = x.shape
    tm = _pick(m, 256)
    vec = pl.BlockSpec((1, d), lambda i: (0, 0))
    return pl.pallas_call(
        _normmod_kernel,
        grid=(m // tm,),
        in_specs=[pl.BlockSpec((tm, d), lambda i: (i, 0)), vec, vec, vec],
        out_specs=pl.BlockSpec((tm, d), lambda i: (i, 0)),
        out_shape=jax.ShapeDtypeStruct((m, d), BF),
        compiler_params=_cparams("parallel"),
        name="normmod",
    )(x, g.reshape(1, d), shift.reshape(1, d), scale.reshape(1, d))


def _resid_kernel(x_ref, y_ref, gate_ref, gy_ref, xo_ref):
    yn = _rms(y_ref[...].astype(F32)) * gy_ref[...]
    xo_ref[...] = x_ref[...] + gate_ref[...] * yn


def _resid_norm_kernel(x_ref, y_ref, gate_ref, gy_ref, gx_ref, sh_ref, sc_ref, xo_ref, ho_ref):
    yn = _rms(y_ref[...].astype(F32)) * gy_ref[...]
    xn = x_ref[...] + gate_ref[...] * yn
    xo_ref[...] = xn
    hn = _rms(xn) * gx_ref[...]
    ho_ref[...] = (hn * (1 + sc_ref[...]) + sh_ref[...]).astype(ho_ref.dtype)


def _resid(x, y, gate, gy, nxt=None):
    m, d = x.shape
    tm = _pick(m, 256)
    row = pl.BlockSpec((tm, d), lambda i: (i, 0))
    vec = pl.BlockSpec((1, d), lambda i: (0, 0))
    v = lambda a: a.reshape(1, d)
    if nxt is None:
        return pl.pallas_call(
            _resid_kernel,
            grid=(m // tm,),
            in_specs=[row, row, vec, vec],
            out_specs=row,
            out_shape=jax.ShapeDtypeStruct((m, d), F32),
            compiler_params=_cparams("parallel"),
            name="resid",
        )(x, y, v(gate), v(gy))
    gx, shift, scale = nxt
    return pl.pallas_call(
        _resid_norm_kernel,
        grid=(m // tm,),
        in_specs=[row, row, vec, vec, vec, vec, vec],
        out_specs=[row, row],
        out_shape=[jax.ShapeDtypeStruct((m, d), F32), jax.ShapeDtypeStruct((m, d), BF)],
        compiler_params=_cparams("parallel"),
        name="resid_norm",
    )(x, y, v(gate), v(gy), v(gx), v(shift), v(scale))


def _mm_kernel(a_ref, w_ref, o_ref):
    o_ref[...] = jnp.dot(a_ref[...], w_ref[...], preferred_element_type=F32).astype(o_ref.dtype)


def _mm(a, w, out_dtype):
    m, k = a.shape
    n = w.shape[1]
    tm, tn = _pick(m, 1024), _pick(n, 1024)
    return pl.pallas_call(
        _mm_kernel,
        grid=(m // tm, n // tn),
        in_specs=[pl.BlockSpec((tm, k), lambda i, j: (i, 0)), pl.BlockSpec((k, tn), lambda i, j: (0, j))],
        out_specs=pl.BlockSpec((tm, tn), lambda i, j: (i, j)),
        out_shape=jax.ShapeDtypeStruct((m, n), out_dtype),
        compiler_params=_cparams("parallel", "parallel"),
        name="mm",
    )(a, w)


def _mm_heads_kernel(a_ref, w_ref, o_ref):
    r = jnp.dot(a_ref[...], w_ref[...], preferred_element_type=F32)
    for hh in range(o_ref.shape[0]):
        o_ref[hh] = r[:, hh * LANES:(hh + 1) * LANES].astype(o_ref.dtype)


def _mm_heads(a, w):
    m, k = a.shape
    n = w.shape[1]
    tm, tn = _pick(m, 1024), _pick(n, 1024)
    return pl.pallas_call(
        _mm_heads_kernel,
        grid=(m // tm, n // tn),
        in_specs=[pl.BlockSpec((tm, k), lambda i, j: (i, 0)), pl.BlockSpec((k, tn), lambda i, j: (0, j))],
        out_specs=pl.BlockSpec((tn // LANES, tm, LANES), lambda i, j: (j, i, 0)),
        out_shape=jax.ShapeDtypeStruct((n // LANES, m, LANES), BF),
        compiler_params=_cparams("parallel", "parallel"),
        name="proj",
    )(a, w)


def _scan_tables(c, reverse):
    n_levels = c.bit_length() - 1
    assert c == 1 << n_levels
    t = np.arange(c)[:, None]
    u = np.arange(c)[None, :]
    dmats = [u <= t, u > t]
    masks = [u == t]
    roles = []
    for lv in range(n_levels):
        size = c >> lv
        start = (t // size) * size
        boundary = start + size // 2 - 1
        later = t > boundary
        dmats.append(np.where(later, (u > boundary) & (u <= t), (u > t) & (u <= boundary)))
        masks.append(later & (u <= boundary.T) & (start == start.T))
        roles.append(np.broadcast_to(later, (c, HEAD_DIM)))
    dmat, mask, role = np.stack(dmats), np.stack(masks), np.stack(roles)
    if reverse:
        dmat, mask, role = dmat[:, ::-1, ::-1], mask[:, ::-1, ::-1], role[:, ::-1]
    return (jnp.asarray(dmat.reshape(-1, c), BF), jnp.asarray(mask, F32), jnp.asarray(role, F32))


def _scan_kernel(q_ref, v_ref, z_ref, lbl_ref, s0_ref, dmat_ref, mask_ref, roles_ref, o_ref, sfin_ref,
                 st_ref, *, layer, reverse):
    c = SCAN_CHUNK
    n_levels = roles_ref.shape[0]
    n_h, t_tile = q_ref.shape[0], q_ref.shape[1]
    span = min(SCAN_UNROLL * c, t_tile)
    n_spans = t_tile // span
    n_c = span // c
    i = pl.program_id(1)

    @pl.when(i == 0)
    def _():
        st_ref[...] = s0_ref[...]

    lbs = []
    for h in range(n_h):
        lg = lbl_ref[h]
        e = jnp.exp(lg - jnp.max(lg, axis=0, keepdims=True))
        p = e / jnp.sum(e, axis=0, keepdims=True)
        lbs.append(jnp.sum(p[:layer + 1], axis=0, keepdims=True))

    nt_dims = (((1,), (1,)), ((), ()))
    tn_dims = (((0,), (0,)), ((), ()))
    last = 0 if reverse else c - 1

    def level_rows(lv, qc, kc):
        size = c >> lv
        half = size // 2
        if half % 8:
            return jnp.where(roles_ref[lv] > 0, qc, kc)
        first, second = (qc, kc) if reverse else (kc, qc)
        parts = []
        for a in range(0, c, size):
            parts += [first[a:a + half], second[a + half:a + size]]
        return jnp.concatenate(parts, axis=0)

    def chunk(st, lf, qc, kc, vc):
        ex = jnp.dot(dmat_ref[...], lf.astype(BF), preferred_element_type=F32)
        b = ex[0:c]
        qb, kb, vb = qc.astype(BF), kc.astype(BF), vc.astype(BF)
        att = lax.dot_general(qb, kb, nt_dims, preferred_element_type=F32) * mask_ref[0]
        for lv in range(n_levels):
            xb = (level_rows(lv, qc, kc) * jnp.exp(ex[(2 + lv) * c:(3 + lv) * c])).astype(BF)
            gram = lax.dot_general(xb, xb, nt_dims, preferred_element_type=F32)
            att = att + gram * mask_ref[1 + lv]
        qd = (qc * jnp.exp(b)).astype(BF)
        o = (jnp.dot(att.astype(BF), vb, preferred_element_type=F32)
             + lax.dot_general(qd, st.astype(BF), nt_dims, preferred_element_type=F32))
        kd = (kc * jnp.exp(ex[c:2 * c])).astype(BF)
        st_new = st * jnp.exp(b[last:last + 1]) + lax.dot_general(vb, kd, tn_dims, preferred_element_type=F32)
        return st_new, o

    def block(n, sts):
        j = (n_spans - 1 - n) if reverse else n
        rows = pl.ds(pl.multiple_of(j * span, span), span)
        sts = list(sts)
        pre = []
        for h in range(n_h):
            z = z_ref[h, rows, :].astype(F32)
            f = lbs[h] + (1 - lbs[h]) * jax.nn.sigmoid(z)
            pre.append((jnp.log(f), _silu(q_ref[h, rows, :].astype(F32)), 1 - f, v_ref[h, rows, :].astype(F32)))
        outs = [[None] * n_c for _ in range(n_h)]
        for m in (range(n_c - 1, -1, -1) if reverse else range(n_c)):
            sl = slice(m * c, (m + 1) * c)
            for h in range(n_h):
                logf, q, kk, v = pre[h]
                sts[h], outs[h][m] = chunk(sts[h], logf[sl], q[sl], kk[sl], v[sl])
        for h in range(n_h):
            o_ref[h, rows, :] = jnp.concatenate(outs[h], axis=0).astype(o_ref.dtype)
        return tuple(sts)

    sts = lax.fori_loop(0, n_spans, block, tuple(st_ref[h] for h in range(n_h)))
    for h in range(n_h):
        st_ref[h] = sts[h]

    @pl.when(i == pl.num_programs(1) - 1)
    def _():
        sfin_ref[...] = st_ref[...]


def _scan(p, lb_logits, s0, *, n_heads, layer, reverse, z_off):
    _, l, _ = p.shape
    t_tile = _pick(l, SCAN_TILE)
    assert t_tile % SCAN_CHUNK == 0
    nt = l // t_tile
    tok = (lambda i: nt - 1 - i) if reverse else (lambda i: i)
    hp = _pick(n_heads, SCAN_HEADS)
    blk = lambda off: pl.BlockSpec((hp, t_tile, HEAD_DIM), lambda h, i: (off // hp + h, tok(i), 0))
    assert n_heads % hp == 0 and z_off % hp == 0
    n_lb = lb_logits.shape[1]
    dmat, mask, roles = _scan_tables(SCAN_CHUNK, reverse)
    whole = lambda a: pl.BlockSpec(a.shape, lambda h, i: (0,) * a.ndim)
    return pl.pallas_call(
        functools.partial(_scan_kernel, layer=layer, reverse=reverse),
        grid=(n_heads // hp, nt),
        in_specs=[
            blk(0), blk(2 * n_heads), blk(z_off),
            pl.BlockSpec((hp, n_lb, HEAD_DIM), lambda h, i: (h, 0, 0)),
            pl.BlockSpec((hp, HEAD_DIM, HEAD_DIM), lambda h, i: (h, 0, 0)),
            whole(dmat), whole(mask), whole(roles),
        ],
        out_specs=[
            pl.BlockSpec((hp, t_tile, HEAD_DIM), lambda h, i: (h, tok(i), 0)),
            pl.BlockSpec((hp, HEAD_DIM, HEAD_DIM), lambda h, i: (h, 0, 0)),
        ],
        out_shape=[
            jax.ShapeDtypeStruct((n_heads, l, HEAD_DIM), BF),
            jax.ShapeDtypeStruct((n_heads, HEAD_DIM, HEAD_DIM), F32),
        ],
        scratch_shapes=[pltpu.VMEM((hp, HEAD_DIM, HEAD_DIM), F32)],
        compiler_params=_cparams("parallel", "arbitrary"),
        name="scan_bwd" if reverse else "scan_fwd",
    )(p, p, p, lb_logits, s0, dmat, mask, roles)


def _gate_kernel(of_ref, ob_ref, g_ref, ng_ref, a_ref):
    for h in range(of_ref.shape[0]):
        o = of_ref[h].astype(F32) + ob_ref[h].astype(F32)
        o = _rms(o) * ng_ref[h]
        a_ref[:, h * HEAD_DIM:(h + 1) * HEAD_DIM] = (o * _silu(g_ref[h].astype(F32))).astype(a_ref.dtype)


def _gate(o_f, o_b, p, norm_g):
    n_heads, l, _ = o_f.shape
    tm = _pick(l, 256)
    blk = lambda off: pl.BlockSpec((n_heads, tm, HEAD_DIM), lambda i: (off, i, 0))
    return pl.pallas_call(
        _gate_kernel,
        grid=(l // tm,),
        in_specs=[blk(0), blk(0), blk(1), pl.BlockSpec((n_heads, 1, HEAD_DIM), lambda i: (0, 0, 0))],
        out_specs=pl.BlockSpec((tm, n_heads * HEAD_DIM), lambda i: (i, 0)),
        out_shape=jax.ShapeDtypeStruct((l, n_heads * HEAD_DIM), BF),
        compiler_params=_cparams("parallel"),
        name="gate",
    )(o_f, o_b, p, norm_g.reshape(n_heads, 1, HEAD_DIM))


def _pool_body(win, prev_ref, cur_ref, next_ref, w_ref, scale_ref, o_ref, y_ref, d_ref, n_rows):
    i = pl.program_id(0)
    tile_rows = cur_ref.shape[0] // GRID_W
    halo = y_ref.shape[0] - tile_rows
    top = halo // 2
    half = win // 2
    r0 = i * tile_rows

    t_io = lax.broadcasted_iota(jnp.int32, (GRID_W, GRID_W), 0)
    s_io = lax.broadcasted_iota(jnp.int32, (GRID_W, GRID_W), 1)
    lo = jnp.maximum(t_io - half, 0)
    hi = jnp.minimum(t_io + win - half, GRID_W)
    a_c = jnp.where((s_io >= lo) & (s_io < hi), 1.0 / (hi - lo).astype(F32), 0.0)
    a_hi, a_lo = _split_bf16(a_c)

    def col_filter(src_ref, src_row, dst_row):
        xs = src_ref[pl.ds(pl.multiple_of(src_row * GRID_W, GRID_W), GRID_W), :]
        y = (jnp.dot(a_hi, xs, preferred_element_type=F32) + jnp.dot(a_lo, xs, preferred_element_type=F32))
        grow = jnp.full((1, y.shape[1]), r0 - top + dst_row, jnp.int32)
        ok = (grow >= 0) & (grow < n_rows)
        y_ref[dst_row] = jnp.where(ok, y, 0.0)

    def above(j, c):
        col_filter(prev_ref, tile_rows - half + j, top - half + j)
        return c

    def inside(j, c):
        col_filter(cur_ref, j, top + j)
        return c

    def below(j, c):
        col_filter(next_ref, j, top + tile_rows + j)
        return c

    lax.fori_loop(0, half, above, 0)
    lax.fori_loop(0, tile_rows, inside, 0)
    lax.fori_loop(0, half - 1, below, 0)

    def row_filter(r, c):
        grow = jnp.full((1, d_ref.shape[1]), r0 + r, jnp.int32)
        cnt = jnp.minimum(grow + win - half, n_rows) - jnp.maximum(grow - half, 0)
        z = y_ref[top + r - half]
        for j in range(1, win):
            z = z + y_ref[top + r - half + j]
        z = z * (1.0 / cnt.astype(F32))
        sl = pl.ds(pl.multiple_of(r * GRID_W, GRID_W), GRID_W)
        d_ref[sl, :] = (z - cur_ref[sl, :].astype(F32)).astype(d_ref.dtype)
        return c

    lax.fori_loop(0, tile_rows, row_filter, 0)
    o_ref[...] = jnp.dot(d_ref[...], w_ref[0], preferred_element_type=F32) * scale_ref[...]


def _pool_kernel(prev_ref, cur_ref, next_ref, w_ref, scale_ref, o_ref, y_ref, d_ref, *, n_rows):
    g = pl.program_id(1)
    for gi, win in enumerate(POOL_WINDOWS):
        @pl.when(g == gi)
        def _(win=win):
            _pool_body(win, prev_ref, cur_ref, next_ref, w_ref, scale_ref, o_ref, y_ref, d_ref, n_rows)


def _pool(h, w_pool, scale):
    l, d = h.shape
    n_groups, gc, _ = w_pool.shape
    n_rows = l // GRID_W
    tile_rows = _pick(n_rows, 16)
    halo = max(POOL_WINDOWS)
    assert tile_rows >= halo // 2 and len(POOL_WINDOWS) == n_groups
    tt = tile_rows * GRID_W
    nt = l // tt
    return pl.pallas_call(
        functools.partial(_pool_kernel, n_rows=n_rows),
        grid=(nt, n_groups),
        in_specs=[
            pl.BlockSpec((tt, gc), lambda i, g: (jnp.maximum(i - 1, 0), g)),
            pl.BlockSpec((tt, gc), lambda i, g: (i, g)),
            pl.BlockSpec((tt, gc), lambda i, g: (jnp.minimum(i + 1, nt - 1), g)),
            pl.BlockSpec((1, gc, gc), lambda i, g: (g, 0, 0)),
            pl.BlockSpec((1, gc), lambda i, g: (0, g)),
        ],
        out_specs=pl.BlockSpec((tt, gc), lambda i, g: (i, g)),
        out_shape=jax.ShapeDtypeStruct((l, d), F32),
        scratch_shapes=[pltpu.VMEM((tile_rows + halo, GRID_W, gc), F32), pltpu.VMEM((tt, gc), BF)],
        compiler_params=_cparams("parallel", "parallel"),
        name="pool",
    )(h, h, h, w_pool, scale.reshape(1, d))


def _router_kernel(h_ref, wrt_ref, bias_ref, upper_ref, eidx_ref, wts_ref, rank_ref, cnt_ref, carry_ref):
    n_exp = wrt_ref.shape[0]
    tm = h_ref.shape[0]

    @pl.when(pl.program_id(0) == 0)
    def _():
        carry_ref[...] = jnp.zeros_like(carry_ref)

    epg = n_exp // N_ROUTE_GROUPS
    logits = lax.dot_general(wrt_ref[...], h_ref[...], (((1,), (1,)), ((), ())), preferred_element_type=F32)
    scores = jax.nn.sigmoid(logits)
    shape3 = (N_ROUTE_GROUPS, epg, tm)
    sc3 = scores.reshape(shape3)
    s3 = (scores + bias_ref[...]).reshape(shape3)
    e_in = lax.broadcasted_iota(jnp.int32, shape3, 1).astype(F32)
    g_io = lax.broadcasted_iota(jnp.int32, shape3, 0).astype(F32)

    m1 = jnp.max(s3, axis=1, keepdims=True)
    i1 = jnp.min(jnp.where(s3 == m1, e_in, epg), axis=1, keepdims=True)
    m2 = jnp.max(jnp.where(e_in == i1, NEG_INF, s3), axis=1, keepdims=True)
    gs = jnp.broadcast_to(m1 + m2, shape3)

    keep = jnp.zeros(shape3, F32)
    cur = gs
    for _ in range(TOPK_ROUTE_GROUPS):
        m = jnp.max(cur, axis=0, keepdims=True)
        idx = jnp.min(jnp.where(cur == m, g_io, N_ROUTE_GROUPS), axis=0, keepdims=True)
        hit = g_io == idx
        keep = jnp.where(hit, 1.0, keep)
        cur = jnp.where(hit, NEG_INF, cur)

    e_io = g_io * epg + e_in
    chosen = jnp.zeros(shape3, F32)
    cur = jnp.where(keep > 0, s3, NEG_INF)
    picks = []
    for _ in range(TOP_K):
        m = jnp.max(jnp.max(cur, axis=0, keepdims=True), axis=1, keepdims=True)
        idx = jnp.where(cur == m, e_io, n_exp)
        idx = jnp.min(jnp.min(idx, axis=0, keepdims=True), axis=1, keepdims=True)
        hit = e_io == idx
        chosen = jnp.where(hit, 1.0, chosen)
        cur = jnp.where(hit, NEG_INF, cur)
        picks.append(idx)

    w = jnp.where(chosen > 0, sc3, 0.0)
    den = jnp.sum(jnp.sum(w, axis=0, keepdims=True), axis=1, keepdims=True)
    w = w / den * ROUTED_SCALE

    ch = chosen.reshape(n_exp, tm).astype(BF)
    carry = carry_ref[...]
    rank = (jnp.dot(ch, upper_ref[...], preferred_element_type=F32)
            + jnp.concatenate([carry] * (tm // LANES), axis=1)).reshape(shape3)
    carry = carry + jnp.dot(ch, jnp.ones((tm, LANES), BF), preferred_element_type=F32)
    carry_ref[...] = carry
    cnt_ref[...] = carry

    def per_pick(vals, idx):
        sel = jnp.where(e_io == idx, vals, 0.0)
        return jnp.sum(jnp.sum(sel, axis=0, keepdims=True), axis=1, keepdims=True).reshape(1, tm)

    eidx_ref[...] = jnp.concatenate([idx.reshape(1, tm) for idx in picks], axis=0).astype(jnp.int32)
    wts_ref[...] = jnp.concatenate([per_pick(w, idx) for idx in picks], axis=0)
    rank_ref[...] = jnp.concatenate([per_pick(rank, idx) for idx in picks], axis=0).astype(jnp.int32)


def _router(h, w_r, b_r):
    t, d = h.shape
    n_exp = w_r.shape[1]
    tm = _pick(t, 512)
    assert tm % LANES == 0
    upper = jnp.asarray(np.triu(np.ones((tm, tm), np.float32), 1), BF)
    pick = pl.BlockSpec((TOP_K, tm), lambda i: (0, i))
    eidx, wts, rank, cnt = pl.pallas_call(
        _router_kernel,
        grid=(t // tm,),
        in_specs=[
            pl.BlockSpec((tm, d), lambda i: (i, 0)),
            pl.BlockSpec((n_exp, d), lambda i: (0, 0)),
            pl.BlockSpec((n_exp, 1), lambda i: (0, 0)),
            pl.BlockSpec((tm, tm), lambda i: (0, 0)),
        ],
        out_specs=[pick, pick, pick, pl.BlockSpec((n_exp, LANES), lambda i: (0, 0))],
        out_shape=[
            jax.ShapeDtypeStruct((TOP_K, t), jnp.int32),
            jax.ShapeDtypeStruct((TOP_K, t), F32),
            jax.ShapeDtypeStruct((TOP_K, t), jnp.int32),
            jax.ShapeDtypeStruct((n_exp, LANES), F32),
        ],
        scratch_shapes=[pltpu.VMEM((n_exp, LANES), F32)],
        compiler_params=_cparams("arbitrary"),
        name="router",
    )(h, w_r.T.astype(BF), b_r.astype(F32).reshape(n_exp, 1), upper)
    return eidx, wts, rank, cnt[:, 0].astype(jnp.int32)


HI_MASK = 0xFFFF0000


def _pack_halves(lo, hi):
    bits = lambda a: pltpu.bitcast(a.astype(BF).astype(F32), jnp.uint32)
    return lax.shift_right_logical(bits(lo), jnp.uint32(16)) | (bits(hi) & jnp.uint32(HI_MASK))


def _pack_rows(v):
    half = v.shape[1] // 2
    return _pack_halves(v[:, :half], v[:, half:])


def _unpack_rows(w):
    lo = pltpu.bitcast(lax.shift_left(w, jnp.uint32(16)), F32)
    hi = pltpu.bitcast(w & jnp.uint32(HI_MASK), F32)
    return lo, hi


def _packh_kernel(h_ref, o_ref):
    o_ref[...] = _pack_rows(h_ref[...].astype(F32))


def _packh(h):
    t, d = h.shape
    tm = _pick(t, 512)
    return pl.pallas_call(
        _packh_kernel,
        grid=(t // tm,),
        in_specs=[pl.BlockSpec((tm, d), lambda i: (i, 0))],
        out_specs=pl.BlockSpec((tm, d // 2), lambda i: (i, 0)),
        out_shape=jax.ShapeDtypeStruct((t, d // 2), jnp.uint32),
        compiler_params=_cparams("parallel"),
        name="packh",
    )(h)


def _dispatch_kernel(pos_ref, h_ref, xs_hbm, sem):
    n_k, tm = pos_ref.shape

    def issue(t, c):
        for k in range(n_k):
            pltpu.make_async_copy(h_ref.at[pl.ds(t, 1)], xs_hbm.at[pl.ds(pos_ref[k, t], 1)],
                                  sem.at[0]).start(priority=k % 2)
        return c

    lax.fori_loop(0, tm, issue, 0, unroll=8)

    def drain(t, c):
        for k in range(n_k):
            pltpu.make_async_copy(h_ref.at[pl.ds(0, 1)], xs_hbm.at[pl.ds(0, 1)], sem.at[0]).wait()
        return c

    lax.fori_loop(0, tm, drain, 0, unroll=8)


def _dispatch(hp, pos):
    t, dw = hp.shape
    n_k = pos.shape[0]
    tm = _pick(t, 256)
    return pl.pallas_call(
        _dispatch_kernel,
        grid=(t // tm,),
        in_specs=[
            pl.BlockSpec((n_k, tm), lambda i: (0, i), memory_space=pltpu.SMEM),
            pl.BlockSpec((tm, dw), lambda i: (i, 0)),
        ],
        out_specs=pl.BlockSpec(memory_space=pl.ANY),
        out_shape=jax.ShapeDtypeStruct((n_k * t, dw), jnp.uint32),
        scratch_shapes=[pltpu.SemaphoreType.DMA((1,))],
        compiler_params=_cparams("arbitrary"),
        name="dispatch",
    )(pos, hp)


def _experts_kernel(tile_ref, exp_ref, lo_ref, hi_ref, xs_ref, wgu_ref, wd_ref, ys_ref,
                    acc_ref, act_ref, wgu_bf_ref, wd_bf_ref):
    w = pl.program_id(0)
    prev = jnp.maximum(w - 1, 0)
    tm = xs_ref.shape[0]
    half = xs_ref.shape[1]
    d_exp = wd_ref.shape[0]

    @pl.when(jnp.logical_or(w == 0, exp_ref[w] != exp_ref[prev]))
    def _():
        wgu_bf_ref[...] = wgu_ref[...].astype(BF)
        wd_bf_ref[...] = wd_ref[...].astype(BF)

    live = hi_ref[w] > lo_ref[w]
    first = jnp.logical_or(w == 0, tile_ref[w] != tile_ref[prev])

    @pl.when(live)
    def _():
        x_lo, x_hi = _unpack_rows(xs_ref[...])
        gu = (jnp.dot(x_lo.astype(BF), wgu_bf_ref[:half, :], preferred_element_type=F32)
              + jnp.dot(x_hi.astype(BF), wgu_bf_ref[half:, :], preferred_element_type=F32))
        row = lax.broadcasted_iota(jnp.int32, (tm, d_exp), 0)
        mine = (row >= lo_ref[w]) & (row < hi_ref[w])
        act_ref[...] = jnp.where(mine, _silu(gu[:, :d_exp]) * gu[:, d_exp:], 0.0).astype(BF)

    def down(accumulate):
        act = act_ref[...]
        chunk = min(DOWN_CHUNK, half)
        assert half % chunk == 0
        for j in range(half // chunk):
            c_lo = slice(j * chunk, (j + 1) * chunk)
            c_hi = slice(half + j * chunk, half + (j + 1) * chunk)
            y_lo = jnp.dot(act, wd_bf_ref[:, c_lo], preferred_element_type=F32)
            y_hi = jnp.dot(act, wd_bf_ref[:, c_hi], preferred_element_type=F32)
            if accumulate:
                y_lo = y_lo + acc_ref[:, c_lo]
                y_hi = y_hi + acc_ref[:, c_hi]
            acc_ref[:, c_lo] = y_lo
            acc_ref[:, c_hi] = y_hi
            ys_ref[:, c_lo] = _pack_halves(y_lo, y_hi)

    @pl.when(jnp.logical_and(live, first))
    def _():
        down(False)

    @pl.when(jnp.logical_and(live, jnp.logical_not(first)))
    def _():
        down(True)


def _experts(xs, items, w_gu, w_down, layer, tm):
    p, dw = xs.shape
    _, _, d, f2 = w_gu.shape
    n_items = items[0].shape[0]
    grid_spec = pltpu.PrefetchScalarGridSpec(
        num_scalar_prefetch=4,
        grid=(n_items,),
        in_specs=[
            pl.BlockSpec((tm, dw), lambda w, tile, exp, lo, hi: (tile[w], 0)),
            pl.BlockSpec((None, None, d, f2), lambda w, tile, exp, lo, hi: (layer, exp[w], 0, 0)),
            pl.BlockSpec((None, None, f2 // 2, d), lambda w, tile, exp, lo, hi: (layer, exp[w], 0, 0)),
        ],
        out_specs=pl.BlockSpec((tm, dw), lambda w, tile, exp, lo, hi: (tile[w], 0)),
        scratch_shapes=[pltpu.VMEM((tm, d), F32), pltpu.VMEM((tm, f2 // 2), BF),
                        pltpu.VMEM((d, f2), BF), pltpu.VMEM((f2 // 2, d), BF)],
    )
    return pl.pallas_call(
        _experts_kernel,
        grid_spec=grid_spec,
        out_shape=jax.ShapeDtypeStruct((p, dw), jnp.uint32),
        compiler_params=_cparams("arbitrary"),
        name="experts",
    )(*items, xs, w_gu, w_down)


def _expert_items(cnt, tm, n_tiles):
    n_exp = cnt.shape[0]
    n_items = n_tiles + n_exp - 1
    end = jnp.cumsum(cnt)
    start = end - cnt
    first_tile = start // tm
    n_e = jnp.where(cnt > 0, (end - 1) // tm - first_tile + 1, 0)
    item_end = jnp.cumsum(n_e)
    w = jnp.arange(n_items, dtype=jnp.int32)
    e = jnp.minimum(jnp.searchsorted(item_end, w, side="right"), n_exp - 1).astype(jnp.int32)
    valid = w < item_end[-1]
    tile = jnp.where(valid, first_tile[e] + (w - (item_end[e] - n_e[e])), n_tiles - 1).astype(jnp.int32)
    lo = jnp.where(valid, jnp.maximum(start[e], tile * tm) - tile * tm, 0).astype(jnp.int32)
    hi = jnp.where(valid, jnp.minimum(end[e], (tile + 1) * tm) - tile * tm, 0).astype(jnp.int32)
    e = jnp.where(valid, e, e[jnp.maximum(item_end[-1] - 1, 0)])
    return tile, e, lo, hi


def _shared_kernel(h_ref, wgu_ref, wd_ref, o_ref):
    d_exp = wd_ref.shape[0]
    gu = jnp.dot(h_ref[...], wgu_ref[...], preferred_element_type=F32)
    act = _silu(gu[:, :d_exp]) * gu[:, d_exp:]
    o_ref[...] = jnp.dot(act.astype(BF), wd_ref[...], preferred_element_type=F32)


def _shared(h, w_gu, w_down):
    t, d = h.shape
    f2 = w_gu.shape[1]
    tm = _pick(t, 512)
    return pl.pallas_call(
        _shared_kernel,
        grid=(t // tm,),
        in_specs=[
            pl.BlockSpec((tm, d), lambda i: (i, 0)),
            pl.BlockSpec((d, f2), lambda i: (0, 0)),
            pl.BlockSpec((f2 // 2, d), lambda i: (0, 0)),
        ],
        out_specs=pl.BlockSpec((tm, d), lambda i: (i, 0)),
        out_shape=jax.ShapeDtypeStruct((t, d), F32),
        compiler_params=_cparams("parallel"),
        name="shared",
    )(h, w_gu, w_down)


def _combine_kernel(pos_ref, sh_ref, wts_ref, x_ref, gate_ref, gy_ref, *rest, with_next):
    if with_next:
        gx_ref, shift_ref, scale_ref, ys_hbm, xo_ref, ho_ref, buf_ref, sem = rest
    else:
        ys_hbm, xo_ref, buf_ref, sem = rest
    i = pl.program_id(0)
    n = pl.num_programs(0)
    n_k, tm = pos_ref.shape[1], pos_ref.shape[2]
    half = buf_ref.shape[3]

    def row_copy(j, t, k, slot):
        return pltpu.make_async_copy(ys_hbm.at[pl.ds(pos_ref[j, k, t], 1)], buf_ref.at[slot, k, pl.ds(t, 1)],
                                     sem.at[slot])

    def gather(j, slot):
        def issue(t, c):
            for k in range(n_k):
                row_copy(j, t, k, slot).start(priority=k % 2)
            return c
        lax.fori_loop(0, tm, issue, 0, unroll=8)

    @pl.when(i == 0)
    def _():
        gather(0, 0)

    slot = i % 2
    nxt_slot = 1 - slot

    def drain(s):
        def body(t, c):
            for k in range(n_k):
                pltpu.make_async_copy(ys_hbm.at[pl.ds(0, 1)], buf_ref.at[s, 0, pl.ds(0, 1)], sem.at[s]).wait()
            return c
        lax.fori_loop(0, tm, body, 0, unroll=8)

    drain(slot)

    acc_lo = sh_ref[:, :half]
    acc_hi = sh_ref[:, half:]
    for k in range(n_k):
        y_lo, y_hi = _unpack_rows(buf_ref[slot, k])
        wk = wts_ref[:, k:k + 1]
        acc_lo = acc_lo + wk * y_lo
        acc_hi = acc_hi + wk * y_hi
        for t in range(tm):
            row_copy(1, t, k, nxt_slot).start(priority=t % 2)

    @pl.when(i == n - 1)
    def _():
        drain(nxt_slot)

    f = jnp.concatenate([acc_lo, acc_hi], axis=1)
    xn = x_ref[...] + gate_ref[...] * (_rms(f) * gy_ref[...])
    xo_ref[...] = xn
    if with_next:
        hn = _rms(xn) * gx_ref[...]
        ho_ref[...] = (hn * (1 + scale_ref[...]) + shift_ref[...]).astype(ho_ref.dtype)


def _combine(ys, pos, wts, sh, x, gate, gy, nxt=None):
    t, d = sh.shape
    n_k = pos.shape[0]
    tm = _pick(t, 64)
    nt = t // tm
    pos3 = pos.reshape(n_k, nt, tm).transpose(1, 0, 2)
    pos_pair = jnp.stack([pos3, jnp.concatenate([pos3[1:], pos3[-1:]], axis=0)], axis=1)
    row = pl.BlockSpec((tm, d), lambda i: (i, 0))
    vec = pl.BlockSpec((1, d), lambda i: (0, 0))
    v = lambda a: a.reshape(1, d)
    vecs = [v(gate), v(gy)] + ([v(a) for a in nxt] if nxt is not None else [])
    out_specs = [row, row] if nxt is not None else row
    out_shape = [jax.ShapeDtypeStruct((t, d), F32)] + ([jax.ShapeDtypeStruct((t, d), BF)] if nxt is not None else [])
    return pl.pallas_call(
        functools.partial(_combine_kernel, with_next=nxt is not None),
        grid=(nt,),
        in_specs=[
            pl.BlockSpec((None, 2, n_k, tm), lambda i: (i, 0, 0, 0), memory_space=pltpu.SMEM),
            row,
            pl.BlockSpec((tm, n_k), lambda i: (i, 0)),
            row,
        ] + [vec] * len(vecs) + [pl.BlockSpec(memory_space=pl.ANY)],
        out_specs=out_specs,
        out_shape=out_shape if nxt is not None else out_shape[0],
        scratch_shapes=[pltpu.VMEM((2, n_k, tm, d // 2), jnp.uint32), pltpu.SemaphoreType.DMA((2,))],
        compiler_params=_cparams("arbitrary"),
        name="combine",
    )(pos_pair, sh, wts, x, *vecs, ys)


def _moe_layer(h, x, gate, gy, nxt, layer, router_w, router_bias, exp_w_gu, exp_w_down, shared_w_gu, shared_w_down):
    t = h.shape[0]
    s_gu, s_down = shared_w_gu[layer], shared_w_down[layer]
    eidx, wts, rank, cnt = _router(h, router_w[layer], router_bias[layer])
    start = jnp.cumsum(cnt) - cnt
    n_exp = cnt.shape[0]
    is_e = eidx[None] == jnp.arange(n_exp, dtype=jnp.int32)[:, None, None]
    pos = jnp.sum(jnp.where(is_e, start[:, None, None], 0), axis=0) + rank
    tm = _pick(TOP_K * t, EXPERT_TILE)
    items = _expert_items(cnt, tm, TOP_K * t // tm)
    xs = _dispatch(_packh(h), pos)
    ys = _experts(xs, items, exp_w_gu, exp_w_down, layer, tm)
    sh = _shared(h, s_gu.astype(BF), s_down.astype(BF))
    return _combine(ys, pos, wts.T, sh, x, gate, gy, nxt)


def kernel(x, c, ctx, c_ctx, w_mod, b_mod, norm_g, hgrn_w_in, hgrn_lb_logits, hgrn_norm_g, hgrn_w_out,
           pool_w, pool_scale, router_w, router_bias, exp_w_gu, exp_w_down, shared_w_gu, shared_w_down):
    batch, seq, d = x.shape
    assert batch == 1 and c.shape[0] == 1
    depth = w_mod.shape[0]
    assert depth == 2 and w_mod.shape[2] == N_MOD * d
    n_heads = hgrn_w_in.shape[2] // 5 // HEAD_DIM
    x0 = x[0]

    cvecs = jnp.concatenate([c, c_ctx[None], jnp.zeros((6, d), F32)], axis=0)
    mods = _mods(cvecs, w_mod, b_mod)
    mod = lambda layer, who, j: mods[layer, who, j * d:(j + 1) * d]

    w_in = hgrn_w_in[0].astype(BF)
    hx = _normmod(x0, norm_g[0, 0], mod(0, 0, 0), mod(0, 0, 1))
    hc = _normmod(ctx[0], norm_g[0, 0], mod(0, 1, 0), mod(0, 1, 1))
    p = _mm_heads(hx, w_in)
    pc = _mm_heads(hc, w_in)
    n_lb = hgrn_lb_logits.shape[1]
    lbl = hgrn_lb_logits.astype(F32).reshape(2, n_lb, n_heads, HEAD_DIM).transpose(0, 2, 1, 3)
    zero_state = jnp.zeros((n_heads, HEAD_DIM, HEAD_DIM), F32)
    scan = functools.partial(_scan, n_heads=n_heads, layer=0)
    _, s_f = scan(pc, lbl[0], zero_state, reverse=False, z_off=3 * n_heads)
    _, s_b = scan(pc, lbl[1], zero_state, reverse=True, z_off=4 * n_heads)
    o_f, _ = scan(p, lbl[0], s_f, reverse=False, z_off=3 * n_heads)
    o_b, _ = scan(p, lbl[1], s_b, reverse=True, z_off=4 * n_heads)
    a = _gate(o_f, o_b, p, hgrn_norm_g[0])
    y = _mm(a, hgrn_w_out[0].astype(BF), F32)
    x1, h = _resid(x0, y, mod(0, 0, 2), norm_g[0, 1], (norm_g[0, 2], mod(0, 0, 3), mod(0, 0, 4)))
    moe = functools.partial(_moe_layer, router_w=router_w, router_bias=router_bias, exp_w_gu=exp_w_gu,
                            exp_w_down=exp_w_down, shared_w_gu=shared_w_gu, shared_w_down=shared_w_down)
    x2, h = moe(h, x1, mod(0, 0, 5), norm_g[0, 3], (norm_g[1, 0], mod(1, 0, 0), mod(1, 0, 1)), 0)

    y = _pool(h, pool_w[0].astype(BF), pool_scale[0])
    x3, h = _resid(x2, y, mod(1, 0, 2), norm_g[1, 1], (norm_g[1, 2], mod(1, 0, 3), mod(1, 0, 4)))
    x4 = moe(h, x3, mod(1, 0, 5), norm_g[1, 3], None, 1)
    return x4[None]
```

```python
import functools

import jax
import jax.numpy as jnp
import numpy as np
from jax import lax
from jax.experimental import pallas as pl
from jax.experimental.pallas import tpu as pltpu

EPS = 1e-6
BF = jnp.bfloat16
F32 = jnp.float32
NEG_INF = float("-inf")

LANES = 128
HEAD_DIM = 128
SCAN_CHUNK = 128
SCAN_UNROLL = 4
SCAN_TILE = 1024
SCAN_HEADS = 2
GRID_W = 64
POOL_WINDOWS = (2, 4, 8, 16)
POOL_ROW_GROUP = 4
N_ROUTE_GROUPS = 8
TOPK_ROUTE_GROUPS = 4
TOP_K = 8
ROUTED_SCALE = 2.5
N_MOD = 6
EXPERT_TILE = 512
DOWN_CHUNK = 512
VMEM_LIMIT =56 * 1024 * 1024


def _cparams(*sem):
    return pltpu.CompilerParams(dimension_semantics=sem, vmem_limit_bytes=VMEM_LIMIT)


def _silu(v):
    return v * jax.nn.sigmoid(v)


def _rms(v):
    return v * lax.rsqrt(jnp.mean(v * v, axis=-1, keepdims=True) + EPS)


def _split_bf16(v):
    hi = v.astype(BF)
    lo = (v - hi.astype(F32)).astype(BF)
    return hi, lo


def _pick(n, want):
    t = min(n, want)
    while n % t:
        t //= 2
    assert t >= 1
    return t


def _mods_kernel(s_ref, w_ref, b_ref, o_ref):
    s = _silu(s_ref[...])
    o_ref[0] = jnp.dot(s.astype(BF), w_ref[0].astype(BF), preferred_element_type=F32) + b_ref[0]


def _mods(cvecs, w_mod, b_mod):
    depth, d, n = w_mod.shape
    tn = _pick(n, 512)
    return pl.pallas_call(
        _mods_kernel,
        grid=(depth, n // tn),
        in_specs=[
            pl.BlockSpec((8, d), lambda l, j: (0, 0)),
            pl.BlockSpec((1, d, tn), lambda l, j: (l, 0, j)),
            pl.BlockSpec((1, 1, tn), lambda l, j: (l, 0, j)),
        ],
        out_specs=pl.BlockSpec((1, 8, tn), lambda l, j: (l, 0, j)),
        out_shape=jax.ShapeDtypeStruct((depth, 8, n), F32),
        compiler_params=_cparams("parallel", "parallel"),
        name="mods",
    )(cvecs, w_mod, b_mod.reshape(depth, 1, n))


def _normmod_kernel(x_ref, g_ref, sh_ref, sc_ref, o_ref):
    y = _rms(x_ref[...]) * g_ref[...]
    o_ref[...] = (y * (1 + sc_ref[...]) + sh_ref[...]).astype(o_ref.dtype)


def _normmod(x, g, shift, scale):
    m, d = x.shape
    tm = _pick(m, 256)
    vec = pl.BlockSpec((1, d), lambda i: (0, 0))
    return pl.pallas_call(
        _normmod_kernel,
        grid=(m // tm,),
        in_specs=[pl.BlockSpec((tm, d), lambda i: (i, 0)), vec, vec, vec],
        out_specs=pl.BlockSpec((tm, d), lambda i: (i, 0)),
        out_shape=jax.ShapeDtypeStruct((m, d), BF),
        compiler_params=_cparams("parallel"),
        name="normmod",
    )(x, g.reshape(1, d), shift.reshape(1, d), scale.reshape(1, d))


def _resid_norm_kernel(x_ref, y_ref, gate_ref, gy_ref, gx_ref, sh_ref, sc_ref, xo_ref, ho_ref):
    yn = _rms(y_ref[...].astype(F32)) * gy_ref[...]
    xn = x_ref[...] + gate_ref[...] * yn
    xo_ref[...] = xn
    hn = _rms(xn) * gx_ref[...]
    ho_ref[...] = (hn * (1 + sc_ref[...]) + sh_ref[...]).astype(ho_ref.dtype)


def _resid(x, y, gate, gy, nxt):
    m, d = x.shape
    tm = _pick(m, 256)
    row = pl.BlockSpec((tm, d), lambda i: (i, 0))
    vec = pl.BlockSpec((1, d), lambda i: (0, 0))
    v = lambda a: a.reshape(1, d)
    gx, shift, scale = nxt
    return pl.pallas_call(
        _resid_norm_kernel,
        grid=(m // tm,),
        in_specs=[row, row, vec, vec, vec, vec, vec],
        out_specs=[row, row],
        out_shape=[jax.ShapeDtypeStruct((m, d), F32), jax.ShapeDtypeStruct((m, d), BF)],
        compiler_params=_cparams("parallel"),
        name="resid_norm",
    )(x, y, v(gate), v(gy), v(gx), v(shift), v(scale))


def _mm_kernel(a_ref, w_ref, o_ref):
    o_ref[...] = jnp.dot(a_ref[...], w_ref[...], preferred_element_type=F32).astype(o_ref.dtype)


def _mm(a, w, out_dtype):
    m, k = a.shape
    n = w.shape[1]
    tm, tn = _pick(m, 1024), _pick(n, 1024)
    return pl.pallas_call(
        _mm_kernel,
        grid=(m // tm, n // tn),
        in_specs=[pl.BlockSpec((tm, k), lambda i, j: (i, 0)), pl.BlockSpec((k, tn), lambda i, j: (0, j))],
        out_specs=pl.BlockSpec((tm, tn), lambda i, j: (i, j)),
        out_shape=jax.ShapeDtypeStruct((m, n), out_dtype),
        compiler_params=_cparams("parallel", "parallel"),
        name="mm",
    )(a, w)


def _mm_heads_kernel(a_ref, w_ref, o_ref):
    r = jnp.dot(a_ref[...], w_ref[...], preferred_element_type=F32)
    for hh in range(o_ref.shape[0]):
        o_ref[hh] = r[:, hh * LANES:(hh + 1) * LANES].astype(o_ref.dtype)


def _mm_heads(a, w):
    m, k = a.shape
    n = w.shape[1]
    tm, tn = _pick(m, 1024), _pick(n, 1024)
    return pl.pallas_call(
        _mm_heads_kernel,
        grid=(m // tm, n // tn),
        in_specs=[pl.BlockSpec((tm, k), lambda i, j: (i, 0)), pl.BlockSpec((k, tn), lambda i, j: (0, j))],
        out_specs=pl.BlockSpec((tn // LANES, tm, LANES), lambda i, j: (j, i, 0)),
        out_shape=jax.ShapeDtypeStruct((n // LANES, m, LANES), BF),
        compiler_params=_cparams("parallel", "parallel"),
        name="proj",
    )(a, w)


def _scan_tables(c, reverse):
    n_levels = c.bit_length() - 1
    assert c == 1 << n_levels
    t = np.arange(c)[:, None]
    u = np.arange(c)[None, :]
    dmats = [u <= t, u > t]
    masks = [u == t]
    roles = []
    for lv in range(n_levels):
        size = c >> lv
        start = (t // size) * size
        boundary = start + size // 2 - 1
        later = t > boundary
        dmats.append(np.where(later, (u > boundary) & (u <= t), (u > t) & (u <= boundary)))
        masks.append(later & (u <= boundary.T) & (start == start.T))
        roles.append(np.broadcast_to(later, (c, HEAD_DIM)))
    dmat, mask, role = np.stack(dmats), np.stack(masks), np.stack(roles)
    if reverse:
        dmat, mask, role = dmat[:, ::-1, ::-1], mask[:, ::-1, ::-1], role[:, ::-1]
    return (jnp.asarray(dmat.reshape(-1, c), BF), jnp.asarray(mask, F32), jnp.asarray(role, F32))


def _scan_kernel(q_ref, v_ref, z_ref, lbl_ref, s0_ref, dmat_ref, mask_ref, roles_ref, o_ref, sfin_ref,
                 st_ref, *, layer, reverse):
    c = SCAN_CHUNK
    n_levels = roles_ref.shape[0]
    n_h, t_tile = q_ref.shape[0], q_ref.shape[1]
    span = min(SCAN_UNROLL * c, t_tile)
    n_spans = t_tile // span
    n_c = span // c
    i = pl.program_id(1)

    @pl.when(i == 0)
    def _():
        st_ref[...] = s0_ref[...]

    lbs = []
    for h in range(n_h):
        lg = lbl_ref[h]
        e = jnp.exp(lg - jnp.max(lg, axis=0, keepdims=True))
        p = e / jnp.sum(e, axis=0, keepdims=True)
        lbs.append(jnp.sum(p[:layer + 1], axis=0, keepdims=True))

    nt_dims = (((1,), (1,)), ((), ()))
    tn_dims = (((0,), (0,)), ((), ()))
    last = 0 if reverse else c - 1

    def level_rows(lv, qc, kc):
        size = c >> lv
        half = size // 2
        if half % 8:
            return jnp.where(roles_ref[lv] > 0, qc, kc)
        first, second = (qc, kc) if reverse else (kc, qc)
        parts = []
        for a in range(0, c, size):
            parts += [first[a:a + half], second[a + half:a + size]]
        return jnp.concatenate(parts, axis=0)

    def chunk(st, lf, qc, kc, vc):
        ex = jnp.dot(dmat_ref[...], lf.astype(BF), preferred_element_type=F32)
        b = ex[0:c]
        qb, kb, vb = qc.astype(BF), kc.astype(BF), vc.astype(BF)
        att = lax.dot_general(qb, kb, nt_dims, preferred_element_type=F32) * mask_ref[0]
        for lv in range(n_levels):
            xb = (level_rows(lv, qc, kc) * jnp.exp(ex[(2 + lv) * c:(3 + lv) * c])).astype(BF)
            gram = lax.dot_general(xb, xb, nt_dims, preferred_element_type=F32)
            att = att + gram * mask_ref[1 + lv]
        qd = (qc * jnp.exp(b)).astype(BF)
        o = (jnp.dot(att.astype(BF), vb, preferred_element_type=F32)
             + lax.dot_general(qd, st.astype(BF), nt_dims, preferred_element_type=F32))
        kd = (kc * jnp.exp(ex[c:2 * c])).astype(BF)
        st_new = st * jnp.exp(b[last:last + 1]) + lax.dot_general(vb, kd, tn_dims, preferred_element_type=F32)
        return st_new, o

    def block(n, sts):
        j = (n_spans - 1 - n) if reverse else n
        rows = pl.ds(pl.multiple_of(j * span, span), span)
        sts = list(sts)
        pre = []
        for h in range(n_h):
            z = z_ref[h, rows, :].astype(F32)
            f = lbs[h] + (1 - lbs[h]) * jax.nn.sigmoid(z)
            pre.append((jnp.log(f), _silu(q_ref[h, rows, :].astype(F32)), 1 - f, v_ref[h, rows, :].astype(F32)))
        outs = [[None] * n_c for _ in range(n_h)]
        for m in (range(n_c - 1, -1, -1) if reverse else range(n_c)):
            sl = slice(m * c, (m + 1) * c)
            for h in range(n_h):
                logf, q, kk, v = pre[h]
                sts[h], outs[h][m] = chunk(sts[h], logf[sl], q[sl], kk[sl], v[sl])
        for h in range(n_h):
            o_ref[h, rows, :] = jnp.concatenate(outs[h], axis=0).astype(o_ref.dtype)
        return tuple(sts)

    sts = lax.fori_loop(0, n_spans, block, tuple(st_ref[h] for h in range(n_h)))
    for h in range(n_h):
        st_ref[h] = sts[h]

    @pl.when(i == pl.num_programs(1) - 1)
    def _():
        sfin_ref[...] = st_ref[...]


def _scan(p, lb_logits, s0, *, n_heads, layer, reverse, z_off):
    _, l, _ = p.shape
    t_tile = _pick(l, SCAN_TILE)
    assert t_tile % SCAN_CHUNK == 0
    nt = l // t_tile
    tok = (lambda i: nt - 1 - i) if reverse else (lambda i: i)
    hp = _pick(n_heads, SCAN_HEADS)
    blk = lambda off: pl.BlockSpec((hp, t_tile, HEAD_DIM), lambda h, i: (off // hp + h, tok(i), 0))
    assert n_heads % hp == 0 and z_off % hp == 0
    n_lb = lb_logits.shape[1]
    dmat, mask, roles = _scan_tables(SCAN_CHUNK, reverse)
    whole = lambda a: pl.BlockSpec(a.shape, lambda h, i: (0,) * a.ndim)
    return pl.pallas_call(
        functools.partial(_scan_kernel, layer=layer, reverse=reverse),
        grid=(n_heads // hp, nt),
        in_specs=[
            blk(0), blk(2 * n_heads), blk(z_off),
            pl.BlockSpec((hp, n_lb, HEAD_DIM), lambda h, i: (h, 0, 0)),
            pl.BlockSpec((hp, HEAD_DIM, HEAD_DIM), lambda h, i: (h, 0, 0)),
            whole(dmat), whole(mask), whole(roles),
        ],
        out_specs=[
            pl.BlockSpec((hp, t_tile, HEAD_DIM), lambda h, i: (h, tok(i), 0)),
            pl.BlockSpec((hp, HEAD_DIM, HEAD_DIM), lambda h, i: (h, 0, 0)),
        ],
        out_shape=[
            jax.ShapeDtypeStruct((n_heads, l, HEAD_DIM), BF),
            jax.ShapeDtypeStruct((n_heads, HEAD_DIM, HEAD_DIM), F32),
        ],
        scratch_shapes=[pltpu.VMEM((hp, HEAD_DIM, HEAD_DIM), F32)],
        compiler_params=_cparams("parallel", "arbitrary"),
        name="scan_bwd" if reverse else "scan_fwd",
    )(p, p, p, lb_logits, s0, dmat, mask, roles)


def _gate_kernel(of_ref, ob_ref, g_ref, ng_ref, a_ref):
    for h in range(of_ref.shape[0]):
        o = of_ref[h].astype(F32) + ob_ref[h].astype(F32)
        o = _rms(o) * ng_ref[h]
        a_ref[:, h * HEAD_DIM:(h + 1) * HEAD_DIM] = (o * _silu(g_ref[h].astype(F32))).astype(a_ref.dtype)


def _gate(o_f, o_b, p, norm_g):
    n_heads, l, _ = o_f.shape
    tm = _pick(l, 256)
    blk = lambda off: pl.BlockSpec((n_heads, tm, HEAD_DIM), lambda i: (off, i, 0))
    return pl.pallas_call(
        _gate_kernel,
        grid=(l // tm,),
        in_specs=[blk(0), blk(0), blk(1), pl.BlockSpec((n_heads, 1, HEAD_DIM), lambda i: (0, 0, 0))],
        out_specs=pl.BlockSpec((tm, n_heads * HEAD_DIM), lambda i: (i, 0)),
        out_shape=jax.ShapeDtypeStruct((l, n_heads * HEAD_DIM), BF),
        compiler_params=_cparams("parallel"),
        name="gate",
    )(o_f, o_b, p, norm_g.reshape(n_heads, 1, HEAD_DIM))


def _pool_body(win, prev_ref, cur_ref, next_ref, w_ref, scale_ref, o_ref, y_ref, d_ref, n_rows):
    i = pl.program_id(0)
    tile_rows = cur_ref.shape[0] // GRID_W
    halo = y_ref.shape[0] - tile_rows
    top = halo // 2
    half = win // 2
    r0 = i * tile_rows

    rg = POOL_ROW_GROUP
    gt = rg * GRID_W
    shift = GRID_W.bit_length() - 1
    assert GRID_W == 1 << shift and tile_rows % rg == 0 and top % rg == 0
    t_io = lax.broadcasted_iota(jnp.int32, (gt, gt), 0)
    s_io = lax.broadcasted_iota(jnp.int32, (gt, gt), 1)
    same_row = lax.shift_right_logical(t_io, shift) == lax.shift_right_logical(s_io, shift)
    off = (s_io & (GRID_W - 1)) - (t_io & (GRID_W - 1))
    a01 = jnp.where(same_row & (off >= -half) & (off < win - half), 1.0, 0.0).astype(BF)

    def col_filter(src_ref, src_row, dst_row):
        xs = src_ref[pl.ds(pl.multiple_of(src_row * GRID_W, gt), gt), :]
        y = jnp.dot(a01, xs, preferred_element_type=F32)
        grow = jnp.full((1, y.shape[1]), r0 - top + dst_row, jnp.int32)
        ok = (grow >= 0) & (grow < n_rows)
        y_ref[pl.ds(dst_row, rg)] = jnp.where(ok, y, 0.0).reshape(rg, GRID_W, y.shape[1])

    n_above = -(-half // rg)
    n_below = -(-(half - 1) // rg)

    def above(j, c):
        col_filter(prev_ref, tile_rows - (n_above - j) * rg, top - (n_above - j) * rg)
        return c

    def inside(j, c):
        col_filter(cur_ref, j * rg, top + j * rg)
        return c

    def below(j, c):
        col_filter(next_ref, j * rg, top + tile_rows + j * rg)
        return c

    lax.fori_loop(0, n_above, above, 0)
    lax.fori_loop(0, tile_rows // rg, inside, 0)
    lax.fori_loop(0, n_below, below, 0)

    c_io = lax.broadcasted_iota(jnp.int32, (GRID_W, d_ref.shape[1]), 0)
    cnt_c = jnp.minimum(c_io + win - half, GRID_W) - jnp.maximum(c_io - half, 0)
    inv_c = 1.0 / cnt_c.astype(F32)

    def row_filter(r, c):
        grow = jnp.full((1, d_ref.shape[1]), r0 + r, jnp.int32)
        cnt = jnp.minimum(grow + win - half, n_rows) - jnp.maximum(grow - half, 0)
        z = y_ref[top + r - half]
        for j in range(1, win):
            z = z + y_ref[top + r - half + j]
        z = z * (inv_c * (1.0 / cnt.astype(F32)))
        sl = pl.ds(pl.multiple_of(r * GRID_W, GRID_W), GRID_W)
        d_ref[sl, :] = (z - cur_ref[sl, :].astype(F32)).astype(d_ref.dtype)
        return c

    lax.fori_loop(0, tile_rows, row_filter, 0)
    o_ref[...] = jnp.dot(d_ref[...], w_ref[0], preferred_element_type=F32) * scale_ref[...]


def _pool_kernel(prev_ref, cur_ref, next_ref, w_ref, scale_ref, o_ref, y_ref, d_ref, *, n_rows):
    g = pl.program_id(1)
    for gi, win in enumerate(POOL_WINDOWS):
        @pl.when(g == gi)
        def _(win=win):
            _pool_body(win, prev_ref, cur_ref, next_ref, w_ref, scale_ref, o_ref, y_ref, d_ref, n_rows)


def _pool(h, w_pool, scale):
    l, d = h.shape
    n_groups, gc, _ = w_pool.shape
    n_rows = l // GRID_W
    tile_rows = _pick(n_rows, 16)
    halo = max(POOL_WINDOWS)
    assert tile_rows >= halo // 2 and len(POOL_WINDOWS) == n_groups
    tt = tile_rows * GRID_W
    nt = l // tt
    return pl.pallas_call(
        functools.partial(_pool_kernel, n_rows=n_rows),
        grid=(nt, n_groups),
        in_specs=[
            pl.BlockSpec((tt, gc), lambda i, g: (jnp.maximum(i - 1, 0), g)),
            pl.BlockSpec((tt, gc), lambda i, g: (i, g)),
            pl.BlockSpec((tt, gc), lambda i, g: (jnp.minimum(i + 1, nt - 1), g)),
            pl.BlockSpec((1, gc, gc), lambda i, g: (g, 0, 0)),
            pl.BlockSpec((1, gc), lambda i, g: (0, g)),
        ],
        out_specs=pl.BlockSpec((tt, gc), lambda i, g: (i, g)),
        out_shape=jax.ShapeDtypeStruct((l, d), F32),
        scratch_shapes=[pltpu.VMEM((tile_rows + halo, GRID_W, gc), F32), pltpu.VMEM((tt, gc), BF)],
        compiler_params=_cparams("parallel", "parallel"),
        name="pool",
    )(h, h, h, w_pool, scale.reshape(1, d))


def _router_kernel(h_ref, wrt_ref, bias_ref, upper_ref, eidx_ref, wts_ref, rank_ref, cnt_ref, carry_ref):
    n_exp = wrt_ref.shape[0]
    tm = h_ref.shape[0]

    @pl.when(pl.program_id(0) == 0)
    def _():
        carry_ref[...] = jnp.zeros_like(carry_ref)

    epg = n_exp // N_ROUTE_GROUPS
    logits = lax.dot_general(wrt_ref[...], h_ref[...], (((1,), (1,)), ((), ())), preferred_element_type=F32)
    scores = jax.nn.sigmoid(logits)
    shape3 = (N_ROUTE_GROUPS, epg, tm)
    sc3 = scores.reshape(shape3)
    s3 = (scores + bias_ref[...]).reshape(shape3)
    e_in = lax.broadcasted_iota(jnp.int32, shape3, 1).astype(F32)
    g_io = lax.broadcasted_iota(jnp.int32, shape3, 0).astype(F32)

    m1 = jnp.max(s3, axis=1, keepdims=True)
    i1 = jnp.min(jnp.where(s3 == m1, e_in, epg), axis=1, keepdims=True)
    m2 = jnp.max(jnp.where(e_in == i1, NEG_INF, s3), axis=1, keepdims=True)
    gs = jnp.broadcast_to(m1 + m2, shape3)

    keep = jnp.zeros(shape3, F32)
    cur = gs
    for _ in range(TOPK_ROUTE_GROUPS):
        m = jnp.max(cur, axis=0, keepdims=True)
        idx = jnp.min(jnp.where(cur == m, g_io, N_ROUTE_GROUPS), axis=0, keepdims=True)
        hit = g_io == idx
        keep = jnp.where(hit, 1.0, keep)
        cur = jnp.where(hit, NEG_INF, cur)

    e_io = g_io * epg + e_in
    chosen = jnp.zeros(shape3, F32)
    cur = jnp.where(keep > 0, s3, NEG_INF)
    picks = []
    for _ in range(TOP_K):
        m = jnp.max(jnp.max(cur, axis=0, keepdims=True), axis=1, keepdims=True)
        idx = jnp.where(cur == m, e_io, n_exp)
        idx = jnp.min(jnp.min(idx, axis=0, keepdims=True), axis=1, keepdims=True)
        hit = e_io == idx
        chosen = jnp.where(hit, 1.0, chosen)
        cur = jnp.where(hit, NEG_INF, cur)
        picks.append(idx)

    w = jnp.where(chosen > 0, sc3, 0.0)
    den = jnp.sum(jnp.sum(w, axis=0, keepdims=True), axis=1, keepdims=True)
    w = w / den * ROUTED_SCALE

    ch = chosen.reshape(n_exp, tm).astype(BF)
    carry = carry_ref[...]
    rank = (jnp.dot(ch, upper_ref[...], preferred_element_type=F32)
            + jnp.concatenate([carry] * (tm // LANES), axis=1)).reshape(shape3)
    carry = carry + jnp.dot(ch, jnp.ones((tm, LANES), BF), preferred_element_type=F32)
    carry_ref[...] = carry
    cnt_ref[...] = carry

    def per_pick(vals, idx):
        sel = jnp.where(e_io == idx, vals, 0.0)
        return jnp.sum(jnp.sum(sel, axis=0, keepdims=True), axis=1, keepdims=True).reshape(1, tm)

    eidx_ref[...] = jnp.concatenate([idx.reshape(1, tm) for idx in picks], axis=0).astype(jnp.int32)
    wts_ref[...] = jnp.concatenate([per_pick(w, idx) for idx in picks], axis=0)
    rank_ref[...] = jnp.concatenate([per_pick(rank, idx) for idx in picks], axis=0).astype(jnp.int32)


def _router(h, w_r, b_r):
    t, d = h.shape
    n_exp = w_r.shape[1]
    tm = _pick(t, 512)
    assert tm % LANES == 0
    upper = jnp.asarray(np.triu(np.ones((tm, tm), np.float32), 1), BF)
    pick = pl.BlockSpec((TOP_K, tm), lambda i: (0, i))
    eidx, wts, rank, cnt = pl.pallas_call(
        _router_kernel,
        grid=(t // tm,),
        in_specs=[
            pl.BlockSpec((tm, d), lambda i: (i, 0)),
            pl.BlockSpec((n_exp, d), lambda i: (0, 0)),
            pl.BlockSpec((n_exp, 1), lambda i: (0, 0)),
            pl.BlockSpec((tm, tm), lambda i: (0, 0)),
        ],
        out_specs=[pick, pick, pick, pl.BlockSpec((n_exp, LANES), lambda i: (0, 0))],
        out_shape=[
            jax.ShapeDtypeStruct((TOP_K, t), jnp.int32),
            jax.ShapeDtypeStruct((TOP_K, t), F32),
            jax.ShapeDtypeStruct((TOP_K, t), jnp.int32),
            jax.ShapeDtypeStruct((n_exp, LANES), F32),
        ],
        scratch_shapes=[pltpu.VMEM((n_exp, LANES), F32)],
        compiler_params=_cparams("arbitrary"),
        name="router",
    )(h, w_r.T.astype(BF), b_r.astype(F32).reshape(n_exp, 1), upper)
    return eidx, wts, rank, cnt[:, 0].astype(jnp.int32)


HI_MASK = 0xFFFF0000


def _pack_halves(lo, hi):
    bits = lambda a: pltpu.bitcast(a.astype(BF).astype(F32), jnp.uint32)
    return lax.shift_right_logical(bits(lo), jnp.uint32(16)) | (bits(hi) & jnp.uint32(HI_MASK))


def _pack_rows(v):
    half = v.shape[1] // 2
    return _pack_halves(v[:, :half], v[:, half:])


def _unpack_rows(w):
    lo = pltpu.bitcast(lax.shift_left(w, jnp.uint32(16)), F32)
    hi = pltpu.bitcast(w & jnp.uint32(HI_MASK), F32)
    return lo, hi


def _dispatch_kernel(pos_ref, h_ref, xs_hbm, hp_ref, sem):
    n_k, tm = pos_ref.shape
    hp_ref[...] = _pack_rows(h_ref[...].astype(F32))

    def issue(t, c):
        for k in range(n_k):
            pltpu.make_async_copy(hp_ref.at[pl.ds(t, 1)], xs_hbm.at[pl.ds(pos_ref[k, t], 1)],
                                  sem.at[0]).start(priority=k % 2)
        return c

    lax.fori_loop(0, tm, issue, 0, unroll=8)

    def drain(t, c):
        for k in range(n_k):
            pltpu.make_async_copy(hp_ref.at[pl.ds(0, 1)], xs_hbm.at[pl.ds(0, 1)], sem.at[0]).wait()
        return c

    lax.fori_loop(0, tm, drain, 0, unroll=8)


def _dispatch(h, pos):
    t, d = h.shape
    n_k = pos.shape[0]
    tm = _pick(t, 256)
    return pl.pallas_call(
        _dispatch_kernel,
        grid=(t // tm,),
        in_specs=[
            pl.BlockSpec((n_k, tm), lambda i: (0, i), memory_space=pltpu.SMEM),
            pl.BlockSpec((tm, d), lambda i: (i, 0)),
        ],
        out_specs=pl.BlockSpec(memory_space=pl.ANY),
        out_shape=jax.ShapeDtypeStruct((n_k * t, d // 2), jnp.uint32),
        scratch_shapes=[pltpu.VMEM((tm, d // 2), jnp.uint32), pltpu.SemaphoreType.DMA((1,))],
        compiler_params=_cparams("arbitrary"),
        name="dispatch",
    )(pos, h)


def _experts_kernel(tile_ref, exp_ref, lo_ref, hi_ref, xs_ref, wgu_ref, wd_ref, ys_ref,
                    acc_ref, act_ref, wgu_bf_ref, wd_bf_ref):
    w = pl.program_id(0)
    prev = jnp.maximum(w - 1, 0)
    tm = xs_ref.shape[0]
    half = xs_ref.shape[1]
    d_exp = wd_ref.shape[0]

    @pl.when(jnp.logical_or(w == 0, exp_ref[w] != exp_ref[prev]))
    def _():
        wgu_bf_ref[...] = wgu_ref[...].astype(BF)
        wd_bf_ref[...] = wd_ref[...].astype(BF)

    live = hi_ref[w] > lo_ref[w]
    first = jnp.logical_or(w == 0, tile_ref[w] != tile_ref[prev])

    @pl.when(live)
    def _():
        x_lo, x_hi = _unpack_rows(xs_ref[...])
        gu = (jnp.dot(x_lo.astype(BF), wgu_bf_ref[:half, :], preferred_element_type=F32)
              + jnp.dot(x_hi.astype(BF), wgu_bf_ref[half:, :], preferred_element_type=F32))
        row = lax.broadcasted_iota(jnp.int32, (tm, d_exp), 0)
        mine = (row >= lo_ref[w]) & (row < hi_ref[w])
        act_ref[...] = jnp.where(mine, _silu(gu[:, :d_exp]) * gu[:, d_exp:], 0.0).astype(BF)

    def down(accumulate):
        act = act_ref[...]
        chunk = min(DOWN_CHUNK, half)
        assert half % chunk == 0
        for j in range(half // chunk):
            c_lo = slice(j * chunk, (j + 1) * chunk)
            c_hi = slice(half + j * chunk, half + (j + 1) * chunk)
            y_lo = jnp.dot(act, wd_bf_ref[:, c_lo], preferred_element_type=F32)
            y_hi = jnp.dot(act, wd_bf_ref[:, c_hi], preferred_element_type=F32)
            if accumulate:
                y_lo = y_lo + acc_ref[:, c_lo]
                y_hi = y_hi + acc_ref[:, c_hi]
            acc_ref[:, c_lo] = y_lo
            acc_ref[:, c_hi] = y_hi
            ys_ref[:, c_lo] = _pack_halves(y_lo, y_hi)

    @pl.when(jnp.logical_and(live, first))
    def _():
        down(False)

    @pl.when(jnp.logical_and(live, jnp.logical_not(first)))
    def _():
        down(True)


def _experts(xs, items, w_gu, w_down, layer, tm):
    p, dw = xs.shape
    _, _, d, f2 = w_gu.shape
    n_items = items[0].shape[0]
    grid_spec = pltpu.PrefetchScalarGridSpec(
        num_scalar_prefetch=4,
        grid=(n_items,),
        in_specs=[
            pl.BlockSpec((tm, dw), lambda w, tile, exp, lo, hi: (tile[w], 0)),
            pl.BlockSpec((None, None, d, f2), lambda w, tile, exp, lo, hi: (layer, exp[w], 0, 0)),
            pl.BlockSpec((None, None, f2 // 2, d), lambda w, tile, exp, lo, hi: (layer, exp[w], 0, 0)),
        ],
        out_specs=pl.BlockSpec((tm, dw), lambda w, tile, exp, lo, hi: (tile[w], 0)),
        scratch_shapes=[pltpu.VMEM((tm, d), F32), pltpu.VMEM((tm, f2 // 2), BF),
                        pltpu.VMEM((d, f2), BF), pltpu.VMEM((f2 // 2, d), BF)],
    )
    return pl.pallas_call(
        _experts_kernel,
        grid_spec=grid_spec,
        out_shape=jax.ShapeDtypeStruct((p, dw), jnp.uint32),
        compiler_params=_cparams("arbitrary"),
        name="experts",
    )(*items, xs, w_gu, w_down)


def _expert_items(cnt, tm, n_tiles):
    n_exp = cnt.shape[0]
    n_items = n_tiles + n_exp - 1
    end = jnp.cumsum(cnt)
    start = end - cnt
    first_tile = start // tm
    n_e = jnp.where(cnt > 0, (end - 1) // tm - first_tile + 1, 0)
    item_end = jnp.cumsum(n_e)
    w = jnp.arange(n_items, dtype=jnp.int32)
    e = jnp.minimum(jnp.searchsorted(item_end, w, side="right"), n_exp - 1).astype(jnp.int32)
    valid = w < item_end[-1]
    tile = jnp.where(valid, first_tile[e] + (w - (item_end[e] - n_e[e])), n_tiles - 1).astype(jnp.int32)
    lo = jnp.where(valid, jnp.maximum(start[e], tile * tm) - tile * tm, 0).astype(jnp.int32)
    hi = jnp.where(valid, jnp.minimum(end[e], (tile + 1) * tm) - tile * tm, 0).astype(jnp.int32)
    e = jnp.where(valid, e, e[jnp.maximum(item_end[-1] - 1, 0)])
    return tile, e, lo, hi


def _shared_kernel(h_ref, wgu_ref, wd_ref, o_ref):
    d_exp = wd_ref.shape[0]
    gu = jnp.dot(h_ref[...], wgu_ref[...], preferred_element_type=F32)
    act = _silu(gu[:, :d_exp]) * gu[:, d_exp:]
    o_ref[...] = jnp.dot(act.astype(BF), wd_ref[...], preferred_element_type=F32)


def _shared(h, w_gu, w_down):
    t, d = h.shape
    f2 = w_gu.shape[1]
    tm = _pick(t, 512)
    return pl.pallas_call(
        _shared_kernel,
        grid=(t // tm,),
        in_specs=[
            pl.BlockSpec((tm, d), lambda i: (i, 0)),
            pl.BlockSpec((d, f2), lambda i: (0, 0)),
            pl.BlockSpec((f2 // 2, d), lambda i: (0, 0)),
        ],
        out_specs=pl.BlockSpec((tm, d), lambda i: (i, 0)),
        out_shape=jax.ShapeDtypeStruct((t, d), F32),
        compiler_params=_cparams("parallel"),
        name="shared",
    )(h, w_gu, w_down)


def _combine_kernel(pos_ref, sh_ref, wts_ref, x_ref, gate_ref, gy_ref, *rest, with_next):
    if with_next:
        gx_ref, shift_ref, scale_ref, ys_hbm, xo_ref, ho_ref, buf_ref, sem = rest
    else:
        ys_hbm, xo_ref, buf_ref, sem = rest
    i = pl.program_id(0)
    n = pl.num_programs(0)
    n_k, tm = pos_ref.shape[1], pos_ref.shape[2]
    half = buf_ref.shape[3]

    def row_copy(j, t, k, slot):
        return pltpu.make_async_copy(ys_hbm.at[pl.ds(pos_ref[j, k, t], 1)], buf_ref.at[slot, k, pl.ds(t, 1)],
                                     sem.at[slot])

    def gather(j, slot):
        def issue(t, c):
            for k in range(n_k):
                row_copy(j, t, k, slot).start(priority=k % 2)
            return c
        lax.fori_loop(0, tm, issue, 0, unroll=8)

    @pl.when(i == 0)
    def _():
        gather(0, 0)

    slot = i % 2
    nxt_slot = 1 - slot

    def drain(s):
        def body(t, c):
            for k in range(n_k):
                pltpu.make_async_copy(ys_hbm.at[pl.ds(0, 1)], buf_ref.at[s, 0, pl.ds(0, 1)], sem.at[s]).wait()
            return c
        lax.fori_loop(0, tm, body, 0, unroll=8)

    drain(slot)

    acc_lo = sh_ref[:, :half]
    acc_hi = sh_ref[:, half:]
    for k in range(n_k):
        y_lo, y_hi = _unpack_rows(buf_ref[slot, k])
        wk = wts_ref[:, k:k + 1]
        acc_lo = acc_lo + wk * y_lo
        acc_hi = acc_hi + wk * y_hi
        for t in range(tm):
            row_copy(1, t, k, nxt_slot).start(priority=t % 2)

    @pl.when(i == n - 1)
    def _():
        drain(nxt_slot)

    f = jnp.concatenate([acc_lo, acc_hi], axis=1)
    xn = x_ref[...] + gate_ref[...] * (_rms(f) * gy_ref[...])
    xo_ref[...] = xn
    if with_next:
        hn = _rms(xn) * gx_ref[...]
        ho_ref[...] = (hn * (1 + scale_ref[...]) + shift_ref[...]).astype(ho_ref.dtype)


def _combine(ys, pos, wts, sh, x, gate, gy, nxt=None):
    t, d = sh.shape
    n_k = pos.shape[0]
    tm = _pick(t, 64)
    nt = t // tm
    pos3 = pos.reshape(n_k, nt, tm).transpose(1, 0, 2)
    pos_pair = jnp.stack([pos3, jnp.concatenate([pos3[1:], pos3[-1:]], axis=0)], axis=1)
    row = pl.BlockSpec((tm, d), lambda i: (i, 0))
    vec = pl.BlockSpec((1, d), lambda i: (0, 0))
    v = lambda a: a.reshape(1, d)
    vecs = [v(gate), v(gy)] + ([v(a) for a in nxt] if nxt is not None else [])
    out_specs = [row, row] if nxt is not None else row
    out_shape = [jax.ShapeDtypeStruct((t, d), F32)] + ([jax.ShapeDtypeStruct((t, d), BF)] if nxt is not None else [])
    return pl.pallas_call(
        functools.partial(_combine_kernel, with_next=nxt is not None),
        grid=(nt,),
        in_specs=[
            pl.BlockSpec((None, 2, n_k, tm), lambda i: (i, 0, 0, 0), memory_space=pltpu.SMEM),
            row,
            pl.BlockSpec((tm, n_k), lambda i: (i, 0)),
            row,
        ] + [vec] * len(vecs) + [pl.BlockSpec(memory_space=pl.ANY)],
        out_specs=out_specs,
        out_shape=out_shape if nxt is not None else out_shape[0],
        scratch_shapes=[pltpu.VMEM((2, n_k, tm, d // 2), jnp.uint32), pltpu.SemaphoreType.DMA((2,))],
        compiler_params=_cparams("arbitrary"),
        name="combine",
    )(pos_pair, sh, wts, x, *vecs, ys)


def _moe_layer(h, x, gate, gy, nxt, layer, router_w, router_bias, exp_w_gu, exp_w_down, shared_w_gu, shared_w_down):
    t = h.shape[0]
    s_gu, s_down = shared_w_gu[layer], shared_w_down[layer]
    eidx, wts, rank, cnt = _router(h, router_w[layer], router_bias[layer])
    start = jnp.cumsum(cnt) - cnt
    n_exp = cnt.shape[0]
    is_e = eidx[None] == jnp.arange(n_exp, dtype=jnp.int32)[:, None, None]
    pos = jnp.sum(jnp.where(is_e, start[:, None, None], 0), axis=0) + rank
    tm = _pick(TOP_K * t, EXPERT_TILE)
    items = _expert_items(cnt, tm, TOP_K * t // tm)
    xs = _dispatch(h, pos)
    ys = _experts(xs, items, exp_w_gu, exp_w_down, layer, tm)
    sh = _shared(h, s_gu.astype(BF), s_down.astype(BF))
    return _combine(ys, pos, wts.T, sh, x, gate, gy, nxt)


def kernel(x, c, ctx, c_ctx, w_mod, b_mod, norm_g, hgrn_w_in, hgrn_lb_logits, hgrn_norm_g, hgrn_w_out,
           pool_w, pool_scale, router_w, router_bias, exp_w_gu, exp_w_down, shared_w_gu, shared_w_down):
    batch, seq, d = x.shape
    assert batch == 1 and c.shape[0] == 1
    depth = w_mod.shape[0]
    assert depth == 2 and w_mod.shape[2] == N_MOD * d
    n_heads = hgrn_w_in.shape[2] // 5 // HEAD_DIM
    x0 = x[0]

    cvecs = jnp.concatenate([c, c_ctx[None], jnp.zeros((6, d), F32)], axis=0)
    mods = _mods(cvecs, w_mod, b_mod)
    mod = lambda layer, who, j: mods[layer, who, j * d:(j + 1) * d]

    w_in = hgrn_w_in[0].astype(BF)
    hx = _normmod(x0, norm_g[0, 0], mod(0, 0, 0), mod(0, 0, 1))
    hc = _normmod(ctx[0], norm_g[0, 0], mod(0, 1, 0), mod(0, 1, 1))
    p = _mm_heads(hx, w_in)
    pc = _mm_heads(hc, w_in)
    n_lb = hgrn_lb_logits.shape[1]
    lbl = hgrn_lb_logits.astype(F32).reshape(2, n_lb, n_heads, HEAD_DIM).transpose(0, 2, 1, 3)
    zero_state = jnp.zeros((n_heads, HEAD_DIM, HEAD_DIM), F32)
    scan = functools.partial(_scan, n_heads=n_heads, layer=0)
    _, s_f = scan(pc, lbl[0], zero_state, reverse=False, z_off=3 * n_heads)
    _, s_b = scan(pc, lbl[1], zero_state, reverse=True, z_off=4 * n_heads)
    o_f, _ = scan(p, lbl[0], s_f, reverse=False, z_off=3 * n_heads)
    o_b, _ = scan(p, lbl[1], s_b, reverse=True, z_off=4 * n_heads)
    a = _gate(o_f, o_b, p, hgrn_norm_g[0])
    y = _mm(a, hgrn_w_out[0].astype(BF), F32)
    x1, h = _resid(x0, y, mod(0, 0, 2), norm_g[0, 1], (norm_g[0, 2], mod(0, 0, 3), mod(0, 0, 4)))
    moe = functools.partial(_moe_layer, router_w=router_w, router_bias=router_bias, exp_w_gu=exp_w_gu,
                            exp_w_down=exp_w_down, shared_w_gu=shared_w_gu, shared_w_down=shared_w_down)
    x2, h = moe(h, x1, mod(0, 0, 5), norm_g[0, 3], (norm_g[1, 0], mod(1, 0, 0), mod(1, 0, 1)), 0)

    y = _pool(h, pool_w[0].astype(BF), pool_scale[0])
    x3, h = _resid(x2, y, mod(1, 0, 2), norm_g[1, 1], (norm_g[1, 2], mod(1, 0, 3), mod(1, 0, 4)))
    x4 = moe(h, x3, mod(1, 0, 5), norm_g[1, 3], None, 1)
    return x4[None]
```

```python
import functools

import jax
import jax.numpy as jnp
import numpy as np
from jax import lax
from jax.experimental import pallas as pl
from jax.experimental.pallas import tpu as pltpu

EPS = 1e-6
BF = jnp.bfloat16
F32 = jnp.float32
NEG_INF = float("-inf")

LANES = 128
HEAD_DIM = 128
SCAN_CHUNK = 128
SCAN_UNROLL = 4
SCAN_TILE = 1024
SCAN_HEADS = 2
GRID_W = 64
POOL_WINDOWS = (2, 4, 8, 16)
POOL_ROW_GROUP = 4
N_ROUTE_GROUPS = 8
TOPK_ROUTE_GROUPS = 4
TOP_K = 8
ROUTED_SCALE = 2.5
N_MOD = 6
EXPERT_TILE = 512
DOWN_CHUNK = 512
VMEM_LIMIT =56 * 1024 * 1024


def _cparams(*sem):
    return pltpu.CompilerParams(dimension_semantics=sem, vmem_limit_bytes=VMEM_LIMIT)


def _silu(v):
    return v * jax.nn.sigmoid(v)


def _rms(v):
    return v * lax.rsqrt(jnp.mean(v * v, axis=-1, keepdims=True) + EPS)


def _split_bf16(v):
    hi = v.astype(BF)
    lo = (v - hi.astype(F32)).astype(BF)
    return hi, lo


def _pick(n, want):
    t = min(n, want)
    while n % t:
        t //= 2
    assert t >= 1
    return t


def _mods_kernel(s_ref, w_ref, b_ref, o_ref):
    s = _silu(s_ref[...])
    o_ref[0] = jnp.dot(s.astype(BF), w_ref[0].astype(BF), preferred_element_type=F32) + b_ref[0]


def _mods(cvecs, w_mod, b_mod):
    depth, d, n = w_mod.shape
    tn = _pick(n, 512)
    return pl.pallas_call(
        _mods_kernel,
        grid=(depth, n // tn),
        in_specs=[
            pl.BlockSpec((8, d), lambda l, j: (0, 0)),
            pl.BlockSpec((1, d, tn), lambda l, j: (l, 0, j)),
            pl.BlockSpec((1, 1, tn), lambda l, j: (l, 0, j)),
        ],
        out_specs=pl.BlockSpec((1, 8, tn), lambda l, j: (l, 0, j)),
        out_shape=jax.ShapeDtypeStruct((depth, 8, n), F32),
        compiler_params=_cparams("parallel", "parallel"),
        name="mods",
    )(cvecs, w_mod, b_mod.reshape(depth, 1, n))


def _normmod_kernel(x_ref, g_ref, sh_ref, sc_ref, o_ref):
    y = _rms(x_ref[...]) * g_ref[...]
    o_ref[...] = (y * (1 + sc_ref[...]) + sh_ref[...]).astype(o_ref.dtype)


def _normmod(x, g, shift, scale):
    m, d = x.shape
    tm = _pick(m, 256)
    vec = pl.BlockSpec((1, d), lambda i: (0, 0))
    return pl.pallas_call(
        _normmod_kernel,
        grid=(m // tm,),
        in_specs=[pl.BlockSpec((tm, d), lambda i: (i, 0)), vec, vec, vec],
        out_specs=pl.BlockSpec((tm, d), lambda i: (i, 0)),
        out_shape=jax.ShapeDtypeStruct((m, d), BF),
        compiler_params=_cparams("parallel"),
        name="normmod",
    )(x, g.reshape(1, d), shift.reshape(1, d), scale.reshape(1, d))


def _resid_norm_kernel(x_ref, y_ref, gate_ref, gy_ref, gx_ref, sh_ref, sc_ref, xo_ref, ho_ref):
    yn = _rms(y_ref[...].astype(F32)) * gy_ref[...]
    xn = x_ref[...] + gate_ref[...] * yn
    xo_ref[...] = xn
    hn = _rms(xn) * gx_ref[...]
    ho_ref[...] = (hn * (1 + sc_ref[...]) + sh_ref[...]).astype(ho_ref.dtype)


def _resid(x, y, gate, gy, nxt):
    m, d = x.shape
    tm = _pick(m, 256)
    row = pl.BlockSpec((tm, d), lambda i: (i, 0))
    vec = pl.BlockSpec((1, d), lambda i: (0, 0))
    v = lambda a: a.reshape(1, d)
    gx, shift, scale = nxt
    return pl.pallas_call(
        _resid_norm_kernel,
        grid=(m // tm,),
        in_specs=[row, row, vec, vec, vec, vec, vec],
        out_specs=[row, row],
        out_shape=[jax.ShapeDtypeStruct((m, d), F32), jax.ShapeDtypeStruct((m, d), BF)],
        compiler_params=_cparams("parallel"),
        name="resid_norm",
    )(x, y, v(gate), v(gy), v(gx), v(shift), v(scale))


def _mm_kernel(a_ref, w_ref, o_ref):
    o_ref[...] = jnp.dot(a_ref[...], w_ref[...], preferred_element_type=F32).astype(o_ref.dtype)


def _mm(a, w, out_dtype):
    m, k = a.shape
    n = w.shape[1]
    tm, tn = _pick(m, 1024), _pick(n, 1024)
    return pl.pallas_call(
        _mm_kernel,
        grid=(m // tm, n // tn),
        in_specs=[pl.BlockSpec((tm, k), lambda i, j: (i, 0)), pl.BlockSpec((k, tn), lambda i, j: (0, j))],
        out_specs=pl.BlockSpec((tm, tn), lambda i, j: (i, j)),
        out_shape=jax.ShapeDtypeStruct((m, n), out_dtype),
        compiler_params=_cparams("parallel", "parallel"),
        name="mm",
    )(a, w)


def _mm_heads_kernel(a_ref, w_ref, o_ref):
    r = jnp.dot(a_ref[...], w_ref[...], preferred_element_type=F32)
    for hh in range(o_ref.shape[0]):
        o_ref[hh] = r[:, hh * LANES:(hh + 1) * LANES].astype(o_ref.dtype)


def _mm_heads(a, w):
    m, k = a.shape
    n = w.shape[1]
    tm, tn = _pick(m, 1024), _pick(n, 1024)
    return pl.pallas_call(
        _mm_heads_kernel,
        grid=(m // tm, n // tn),
        in_specs=[pl.BlockSpec((tm, k), lambda i, j: (i, 0)), pl.BlockSpec((k, tn), lambda i, j: (0, j))],
        out_specs=pl.BlockSpec((tn // LANES, tm, LANES), lambda i, j: (j, i, 0)),
        out_shape=jax.ShapeDtypeStruct((n // LANES, m, LANES), BF),
        compiler_params=_cparams("parallel", "parallel"),
        name="proj",
    )(a, w)


def _scan_tables(c, reverse):
    n_levels = c.bit_length() - 1
    assert c == 1 << n_levels
    t = np.arange(c)[:, None]
    u = np.arange(c)[None, :]
    dmats = [u <= t]
    masks = [u == t]
    roles = []
    for lv in range(n_levels):
        size = c >> lv
        start = (t // size) * size
        boundary = start + size // 2 - 1
        later = t > boundary
        if (size // 2) % 8:
            dmats.append(np.where(later, (u > boundary) & (u <= t), (u > t) & (u <= boundary)))
        masks.append(later & (u <= boundary.T) & (start == start.T))
        roles.append(np.broadcast_to(later, (c, HEAD_DIM)))
    dmat, mask, role = np.stack(dmats), np.stack(masks), np.stack(roles)
    if reverse:
        dmat, mask, role = dmat[:, ::-1, ::-1], mask[:, ::-1, ::-1], role[:, ::-1]
    return (jnp.asarray(dmat.reshape(-1, c), BF), jnp.asarray(mask, F32), jnp.asarray(role, F32))


def _scan_kernel(q_ref, v_ref, z_ref, lbl_ref, s0_ref, dmat_ref, mask_ref, roles_ref, o_ref, sfin_ref,
                 st_ref, *, layer, reverse):
    c = SCAN_CHUNK
    n_levels = roles_ref.shape[0]
    n_h, t_tile = q_ref.shape[0], q_ref.shape[1]
    span = min(SCAN_UNROLL * c, t_tile)
    n_spans = t_tile // span
    n_c = span // c
    i = pl.program_id(1)

    @pl.when(i == 0)
    def _():
        st_ref[...] = s0_ref[...]

    lbs = []
    for h in range(n_h):
        lg = lbl_ref[h]
        e = jnp.exp(lg - jnp.max(lg, axis=0, keepdims=True))
        p = e / jnp.sum(e, axis=0, keepdims=True)
        lbs.append(jnp.sum(p[:layer + 1], axis=0, keepdims=True))

    nt_dims = (((1,), (1,)), ((), ()))
    tn_dims = (((0,), (0,)), ((), ()))
    last = 0 if reverse else c - 1

    def level_rows(lv, qc, kc):
        size = c >> lv
        half = size // 2
        if half % 8:
            return jnp.where(roles_ref[lv] > 0, qc, kc)
        first, second = (qc, kc) if reverse else (kc, qc)
        parts = []
        for a in range(0, c, size):
            parts += [first[a:a + half], second[a + half:a + size]]
        return jnp.concatenate(parts, axis=0)

    def level_exponent(lv, b):
        size = c >> lv
        half = size // 2
        parts = []
        for a in range(0, c, size):
            if reverse:
                bnd = b[a + half:a + half + 1]
                parts += [b[a:a + half] - bnd, bnd - b[a + half:a + size]]
            else:
                bnd = b[a + half - 1:a + half]
                parts += [bnd - b[a:a + half], b[a + half:a + size] - bnd]
        return jnp.concatenate(parts, axis=0)

    def chunk(st, lf, qc, kc, vc):
        ex = jnp.dot(dmat_ref[...], lf.astype(BF), preferred_element_type=F32)
        b = ex[0:c]
        b_last = b[last:last + 1]
        qb, kb, vb = qc.astype(BF), kc.astype(BF), vc.astype(BF)
        att = lax.dot_general(qb, kb, nt_dims, preferred_element_type=F32) * mask_ref[0]
        n_small = 0
        for lv in range(n_levels):
            if (c >> lv) // 2 % 8:
                n_small += 1
                d = ex[n_small * c:(n_small + 1) * c]
            else:
                d = level_exponent(lv, b)
            xb = (level_rows(lv, qc, kc) * jnp.exp2(d)).astype(BF)
            gram = lax.dot_general(xb, xb, nt_dims, preferred_element_type=F32)
            att = att + gram * mask_ref[1 + lv]
        qd = (qc * jnp.exp2(b)).astype(BF)
        o = (jnp.dot(att.astype(BF), vb, preferred_element_type=F32)
             + lax.dot_general(qd, st.astype(BF), nt_dims, preferred_element_type=F32))
        kd = (kc * jnp.exp2(b_last - b)).astype(BF)
        st_new = st * jnp.exp2(b_last) + lax.dot_general(vb, kd, tn_dims, preferred_element_type=F32)
        return st_new, o

    def block(n, sts):
        j = (n_spans - 1 - n) if reverse else n
        rows = pl.ds(pl.multiple_of(j * span, span), span)
        sts = list(sts)
        pre = []
        for h in range(n_h):
            z = z_ref[h, rows, :].astype(F32)
            f = lbs[h] + (1 - lbs[h]) * jax.nn.sigmoid(z)
            pre.append((jnp.log2(f), _silu(q_ref[h, rows, :].astype(F32)), 1 - f, v_ref[h, rows, :].astype(F32)))
        outs = [[None] * n_c for _ in range(n_h)]
        for m in (range(n_c - 1, -1, -1) if reverse else range(n_c)):
            sl = slice(m * c, (m + 1) * c)
            for h in range(n_h):
                logf, q, kk, v = pre[h]
                sts[h], outs[h][m] = chunk(sts[h], logf[sl], q[sl], kk[sl], v[sl])
        for h in range(n_h):
            o_ref[h, rows, :] = jnp.concatenate(outs[h], axis=0).astype(o_ref.dtype)
        return tuple(sts)

    sts = lax.fori_loop(0, n_spans, block, tuple(st_ref[h] for h in range(n_h)))
    for h in range(n_h):
        st_ref[h] = sts[h]

    @pl.when(i == pl.num_programs(1) - 1)
    def _():
        sfin_ref[...] = st_ref[...]


def _scan(p, lb_logits, s0, *, n_heads, layer, reverse, z_off):
    _, l, _ = p.shape
    t_tile = _pick(l, SCAN_TILE)
    assert t_tile % SCAN_CHUNK == 0
    nt = l // t_tile
    tok = (lambda i: nt - 1 - i) if reverse else (lambda i: i)
    hp = _pick(n_heads, SCAN_HEADS)
    blk = lambda off: pl.BlockSpec((hp, t_tile, HEAD_DIM), lambda h, i: (off // hp + h, tok(i), 0))
    assert n_heads % hp == 0 and z_off % hp == 0
    n_lb = lb_logits.shape[1]
    dmat, mask, roles = _scan_tables(SCAN_CHUNK, reverse)
    whole = lambda a: pl.BlockSpec(a.shape, lambda h, i: (0,) * a.ndim)
    return pl.pallas_call(
        functools.partial(_scan_kernel, layer=layer, reverse=reverse),
        grid=(n_heads // hp, nt),
        in_specs=[
            blk(0), blk(2 * n_heads), blk(z_off),
            pl.BlockSpec((hp, n_lb, HEAD_DIM), lambda h, i: (h, 0, 0)),
            pl.BlockSpec((hp, HEAD_DIM, HEAD_DIM), lambda h, i: (h, 0, 0)),
            whole(dmat), whole(mask), whole(roles),
        ],
        out_specs=[
            pl.BlockSpec((hp, t_tile, HEAD_DIM), lambda h, i: (h, tok(i), 0)),
            pl.BlockSpec((hp, HEAD_DIM, HEAD_DIM), lambda h, i: (h, 0, 0)),
        ],
        out_shape=[
            jax.ShapeDtypeStruct((n_heads, l, HEAD_DIM), BF),
            jax.ShapeDtypeStruct((n_heads, HEAD_DIM, HEAD_DIM), F32),
        ],
        scratch_shapes=[pltpu.VMEM((hp, HEAD_DIM, HEAD_DIM), F32)],
        compiler_params=_cparams("parallel", "arbitrary"),
        name="scan_bwd" if reverse else "scan_fwd",
    )(p, p, p, lb_logits, s0, dmat, mask, roles)


def _gate_kernel(of_ref, ob_ref, g_ref, ng_ref, a_ref):
    for h in range(of_ref.shape[0]):
        o = of_ref[h].astype(F32) + ob_ref[h].astype(F32)
        o = _rms(o) * ng_ref[h]
        a_ref[:, h * HEAD_DIM:(h + 1) * HEAD_DIM] = (o * _silu(g_ref[h].astype(F32))).astype(a_ref.dtype)


def _gate(o_f, o_b, p, norm_g):
    n_heads, l, _ = o_f.shape
    tm = _pick(l, 256)
    blk = lambda off: pl.BlockSpec((n_heads, tm, HEAD_DIM), lambda i: (off, i, 0))
    return pl.pallas_call(
        _gate_kernel,
        grid=(l // tm,),
        in_specs=[blk(0), blk(0), blk(1), pl.BlockSpec((n_heads, 1, HEAD_DIM), lambda i: (0, 0, 0))],
        out_specs=pl.BlockSpec((tm, n_heads * HEAD_DIM), lambda i: (i, 0)),
        out_shape=jax.ShapeDtypeStruct((l, n_heads * HEAD_DIM), BF),
        compiler_params=_cparams("parallel"),
        name="gate",
    )(o_f, o_b, p, norm_g.reshape(n_heads, 1, HEAD_DIM))


def _pool_body(win, prev_ref, cur_ref, next_ref, w_ref, scale_ref, o_ref, y_ref, d_ref, n_rows):
    i = pl.program_id(0)
    tile_rows = cur_ref.shape[0] // GRID_W
    halo = y_ref.shape[0] - tile_rows
    top = halo // 2
    half = win // 2
    r0 = i * tile_rows

    rg = POOL_ROW_GROUP
    gt = rg * GRID_W
    shift = GRID_W.bit_length() - 1
    assert GRID_W == 1 << shift and tile_rows % rg == 0 and top % rg == 0
    t_io = lax.broadcasted_iota(jnp.int32, (gt, gt), 0)
    s_io = lax.broadcasted_iota(jnp.int32, (gt, gt), 1)
    same_row = lax.shift_right_logical(t_io, shift) == lax.shift_right_logical(s_io, shift)
    off = (s_io & (GRID_W - 1)) - (t_io & (GRID_W - 1))
    a01 = jnp.where(same_row & (off >= -half) & (off < win - half), 1.0, 0.0).astype(BF)

    def col_filter(src_ref, src_row, dst_row):
        xs = src_ref[pl.ds(pl.multiple_of(src_row * GRID_W, gt), gt), :]
        y = jnp.dot(a01, xs, preferred_element_type=F32)
        grow = jnp.full((1, y.shape[1]), r0 - top + dst_row, jnp.int32)
        ok = (grow >= 0) & (grow < n_rows)
        y_ref[pl.ds(dst_row, rg)] = jnp.where(ok, y, 0.0).reshape(rg, GRID_W, y.shape[1])

    n_above = -(-half // rg)
    n_below = -(-(half - 1) // rg)

    def above(j, c):
        col_filter(prev_ref, tile_rows - (n_above - j) * rg, top - (n_above - j) * rg)
        return c

    def inside(j, c):
        col_filter(cur_ref, j * rg, top + j * rg)
        return c

    def below(j, c):
        col_filter(next_ref, j * rg, top + tile_rows + j * rg)
        return c

    lax.fori_loop(0, n_above, above, 0)
    lax.fori_loop(0, tile_rows // rg, inside, 0)
    lax.fori_loop(0, n_below, below, 0)

    c_io = lax.broadcasted_iota(jnp.int32, (GRID_W, d_ref.shape[1]), 0)
    cnt_c = jnp.minimum(c_io + win - half, GRID_W) - jnp.maximum(c_io - half, 0)
    inv_c = 1.0 / cnt_c.astype(F32)

    def row_filter(r, c):
        grow = jnp.full((1, d_ref.shape[1]), r0 + r, jnp.int32)
        cnt = jnp.minimum(grow + win - half, n_rows) - jnp.maximum(grow - half, 0)
        z = y_ref[top + r - half]
        for j in range(1, win):
            z = z + y_ref[top + r - half + j]
        z = z * (inv_c * (1.0 / cnt.astype(F32)))
        sl = pl.ds(pl.multiple_of(r * GRID_W, GRID_W), GRID_W)
        d_ref[sl, :] = (z - cur_ref[sl, :].astype(F32)).astype(d_ref.dtype)
        return c

    lax.fori_loop(0, tile_rows, row_filter, 0)
    o_ref[...] = jnp.dot(d_ref[...], w_ref[0], preferred_element_type=F32) * scale_ref[...]


def _pool_kernel(prev_ref, cur_ref, next_ref, w_ref, scale_ref, o_ref, y_ref, d_ref, *, n_rows):
    g = pl.program_id(1)
    for gi, win in enumerate(POOL_WINDOWS):
        @pl.when(g == gi)
        def _(win=win):
            _pool_body(win, prev_ref, cur_ref, next_ref, w_ref, scale_ref, o_ref, y_ref, d_ref, n_rows)


def _pool(h, w_pool, scale):
    l, d = h.shape
    n_groups, gc, _ = w_pool.shape
    n_rows = l // GRID_W
    tile_rows = _pick(n_rows, 16)
    halo = max(POOL_WINDOWS)
    assert tile_rows >= halo // 2 and len(POOL_WINDOWS) == n_groups
    tt = tile_rows * GRID_W
    nt = l // tt
    return pl.pallas_call(
        functools.partial(_pool_kernel, n_rows=n_rows),
        grid=(nt, n_groups),
        in_specs=[
            pl.BlockSpec((tt, gc), lambda i, g: (jnp.maximum(i - 1, 0), g)),
            pl.BlockSpec((tt, gc), lambda i, g: (i, g)),
            pl.BlockSpec((tt, gc), lambda i, g: (jnp.minimum(i + 1, nt - 1), g)),
            pl.BlockSpec((1, gc, gc), lambda i, g: (g, 0, 0)),
            pl.BlockSpec((1, gc), lambda i, g: (0, g)),
        ],
        out_specs=pl.BlockSpec((tt, gc), lambda i, g: (i, g)),
        out_shape=jax.ShapeDtypeStruct((l, d), F32),
        scratch_shapes=[pltpu.VMEM((tile_rows + halo, GRID_W, gc), F32), pltpu.VMEM((tt, gc), BF)],
        compiler_params=_cparams("parallel", "parallel"),
        name="pool",
    )(h, h, h, w_pool, scale.reshape(1, d))


def _router_kernel(h_ref, wrt_ref, bias_ref, upper_ref, eidx_ref, wts_ref, rank_ref, cnt_ref, carry_ref):
    n_exp = wrt_ref.shape[0]
    tm = h_ref.shape[0]

    @pl.when(pl.program_id(0) == 0)
    def _():
        carry_ref[...] = jnp.zeros_like(carry_ref)

    epg = n_exp // N_ROUTE_GROUPS
    logits = lax.dot_general(wrt_ref[...], h_ref[...], (((1,), (1,)), ((), ())), preferred_element_type=F32)
    scores = jax.nn.sigmoid(logits)
    shape3 = (N_ROUTE_GROUPS, epg, tm)
    sc3 = scores.reshape(shape3)
    s3 = (scores + bias_ref[...]).reshape(shape3)
    e_in = lax.broadcasted_iota(jnp.int32, shape3, 1).astype(F32)
    g_io = lax.broadcasted_iota(jnp.int32, shape3, 0).astype(F32)

    m1 = jnp.max(s3, axis=1, keepdims=True)
    i1 = jnp.min(jnp.where(s3 == m1, e_in, epg), axis=1, keepdims=True)
    m2 = jnp.max(jnp.where(e_in == i1, NEG_INF, s3), axis=1, keepdims=True)
    gs = jnp.broadcast_to(m1 + m2, shape3)

    keep = jnp.zeros(shape3, F32)
    cur = gs
    for _ in range(TOPK_ROUTE_GROUPS):
        m = jnp.max(cur, axis=0, keepdims=True)
        idx = jnp.min(jnp.where(cur == m, g_io, N_ROUTE_GROUPS), axis=0, keepdims=True)
        hit = g_io == idx
        keep = jnp.where(hit, 1.0, keep)
        cur = jnp.where(hit, NEG_INF, cur)

    e_io = g_io * epg + e_in
    chosen = jnp.zeros(shape3, F32)
    cur = jnp.where(keep > 0, s3, NEG_INF)
    picks = []
    for _ in range(TOP_K):
        m = jnp.max(jnp.max(cur, axis=0, keepdims=True), axis=1, keepdims=True)
        idx = jnp.where(cur == m, e_io, n_exp)
        idx = jnp.min(jnp.min(idx, axis=0, keepdims=True), axis=1, keepdims=True)
        hit = e_io == idx
        chosen = jnp.where(hit, 1.0, chosen)
        cur = jnp.where(hit, NEG_INF, cur)
        picks.append(idx)

    w = jnp.where(chosen > 0, sc3, 0.0)
    den = jnp.sum(jnp.sum(w, axis=0, keepdims=True), axis=1, keepdims=True)
    w = w / den * ROUTED_SCALE

    ch = chosen.reshape(n_exp, tm).astype(BF)
    carry = carry_ref[...]
    rank = (jnp.dot(ch, upper_ref[...], preferred_element_type=F32)
            + jnp.concatenate([carry] * (tm // LANES), axis=1)).reshape(shape3)
    carry = carry + jnp.dot(ch, jnp.ones((tm, LANES), BF), preferred_element_type=F32)
    carry_ref[...] = carry
    cnt_ref[...] = carry

    def per_pick(vals, idx):
        sel = jnp.where(e_io == idx, vals, 0.0)
        return jnp.sum(jnp.sum(sel, axis=0, keepdims=True), axis=1, keepdims=True).reshape(1, tm)

    eidx_ref[...] = jnp.concatenate([idx.reshape(1, tm) for idx in picks], axis=0).astype(jnp.int32)
    wts_ref[...] = jnp.concatenate([per_pick(w, idx) for idx in picks], axis=0)
    rank_ref[...] = jnp.concatenate([per_pick(rank, idx) for idx in picks], axis=0).astype(jnp.int32)


def _router(h, w_r, b_r):
    t, d = h.shape
    n_exp = w_r.shape[1]
    tm = _pick(t, 512)
    assert tm % LANES == 0
    upper = jnp.asarray(np.triu(np.ones((tm, tm), np.float32), 1), BF)
    pick = pl.BlockSpec((TOP_K, tm), lambda i: (0, i))
    eidx, wts, rank, cnt = pl.pallas_call(
        _router_kernel,
        grid=(t // tm,),
        in_specs=[
            pl.BlockSpec((tm, d), lambda i: (i, 0)),
            pl.BlockSpec((n_exp, d), lambda i: (0, 0)),
            pl.BlockSpec((n_exp, 1), lambda i: (0, 0)),
            pl.BlockSpec((tm, tm), lambda i: (0, 0)),
        ],
        out_specs=[pick, pick, pick, pl.BlockSpec((n_exp, LANES), lambda i: (0, 0))],
        out_shape=[
            jax.ShapeDtypeStruct((TOP_K, t), jnp.int32),
            jax.ShapeDtypeStruct((TOP_K, t), F32),
            jax.ShapeDtypeStruct((TOP_K, t), jnp.int32),
            jax.ShapeDtypeStruct((n_exp, LANES), F32),
        ],
        scratch_shapes=[pltpu.VMEM((n_exp, LANES), F32)],
        compiler_params=_cparams("arbitrary"),
        name="router",
    )(h, w_r.T.astype(BF), b_r.astype(F32).reshape(n_exp, 1), upper)
    return eidx, wts, rank, cnt[:, 0].astype(jnp.int32)


HI_MASK = 0xFFFF0000


def _pack_halves(lo, hi):
    bits = lambda a: pltpu.bitcast(a.astype(BF).astype(F32), jnp.uint32)
    return lax.shift_right_logical(bits(lo), jnp.uint32(16)) | (bits(hi) & jnp.uint32(HI_MASK))


def _pack_rows(v):
    half = v.shape[1] // 2
    return _pack_halves(v[:, :half], v[:, half:])


def _unpack_rows(w):
    lo = pltpu.bitcast(lax.shift_left(w, jnp.uint32(16)), F32)
    hi = pltpu.bitcast(w & jnp.uint32(HI_MASK), F32)
    return lo, hi


def _dispatch_kernel(pos_ref, h_ref, xs_hbm, hp_ref, sem):
    i = pl.program_id(0)
    n_k, tm = pos_ref.shape
    slot = i % 2
    hp_ref[slot] = _pack_rows(h_ref[...].astype(F32))

    def issue(t, c):
        for k in range(n_k):
            pltpu.make_async_copy(hp_ref.at[slot, pl.ds(t, 1)], xs_hbm.at[pl.ds(pos_ref[k, t], 1)],
                                  sem.at[slot]).start(priority=k % 2)
        return c

    lax.fori_loop(0, tm, issue, 0, unroll=8)

    def drain(s):
        def body(t, c):
            for k in range(n_k):
                pltpu.make_async_copy(hp_ref.at[s, pl.ds(0, 1)], xs_hbm.at[pl.ds(0, 1)], sem.at[s]).wait()
            return c
        lax.fori_loop(0, tm, body, 0, unroll=8)

    @pl.when(i > 0)
    def _():
        drain(1 - slot)

    @pl.when(i == pl.num_programs(0) - 1)
    def _():
        drain(slot)


def _dispatch(h, pos):
    t, d = h.shape
    n_k = pos.shape[0]
    tm = _pick(t, 256)
    return pl.pallas_call(
        _dispatch_kernel,
        grid=(t // tm,),
        in_specs=[
            pl.BlockSpec((n_k, tm), lambda i: (0, i), memory_space=pltpu.SMEM),
            pl.BlockSpec((tm, d), lambda i: (i, 0)),
        ],
        out_specs=pl.BlockSpec(memory_space=pl.ANY),
        out_shape=jax.ShapeDtypeStruct((n_k * t, d // 2), jnp.uint32),
        scratch_shapes=[pltpu.VMEM((2, tm, d // 2), jnp.uint32), pltpu.SemaphoreType.DMA((2,))],
        compiler_params=_cparams("arbitrary"),
        name="dispatch",
    )(pos, h)


def _experts_kernel(tile_ref, exp_ref, lo_ref, hi_ref, xs_ref, wgu_ref, wd_ref, ys_ref,
                    acc_ref, act_ref, wgu_bf_ref, wd_bf_ref):
    w = pl.program_id(0)
    prev = jnp.maximum(w - 1, 0)
    tm = xs_ref.shape[0]
    half = xs_ref.shape[1]
    d_exp = wd_ref.shape[0]

    @pl.when(jnp.logical_or(w == 0, exp_ref[w] != exp_ref[prev]))
    def _():
        wgu_bf_ref[...] = wgu_ref[...].astype(BF)
        wd_bf_ref[...] = wd_ref[...].astype(BF)

    live = hi_ref[w] > lo_ref[w]
    first = jnp.logical_or(w == 0, tile_ref[w] != tile_ref[prev])

    @pl.when(live)
    def _():
        x_lo, x_hi = _unpack_rows(xs_ref[...])
        gu = (jnp.dot(x_lo.astype(BF), wgu_bf_ref[:half, :], preferred_element_type=F32)
              + jnp.dot(x_hi.astype(BF), wgu_bf_ref[half:, :], preferred_element_type=F32))
        row = lax.broadcasted_iota(jnp.int32, (tm, d_exp), 0)
        mine = (row >= lo_ref[w]) & (row < hi_ref[w])
        act_ref[...] = jnp.where(mine, _silu(gu[:, :d_exp]) * gu[:, d_exp:], 0.0).astype(BF)

    def down(accumulate):
        act = act_ref[...]
        chunk = min(DOWN_CHUNK, half)
        assert half % chunk == 0
        for j in range(half // chunk):
            c_lo = slice(j * chunk, (j + 1) * chunk)
            c_hi = slice(half + j * chunk, half + (j + 1) * chunk)
            y_lo = jnp.dot(act, wd_bf_ref[:, c_lo], preferred_element_type=F32)
            y_hi = jnp.dot(act, wd_bf_ref[:, c_hi], preferred_element_type=F32)
            if accumulate:
                y_lo = y_lo + acc_ref[:, c_lo]
                y_hi = y_hi + acc_ref[:, c_hi]
            acc_ref[:, c_lo] = y_lo
            acc_ref[:, c_hi] = y_hi
            ys_ref[:, c_lo] = _pack_halves(y_lo, y_hi)

    @pl.when(jnp.logical_and(live, first))
    def _():
        down(False)

    @pl.when(jnp.logical_and(live, jnp.logical_not(first)))
    def _():
        down(True)


def _experts(xs, items, w_gu, w_down, layer, tm):
    p, dw = xs.shape
    _, _, d, f2 = w_gu.shape
    n_items = items[0].shape[0]
    grid_spec = pltpu.PrefetchScalarGridSpec(
        num_scalar_prefetch=4,
        grid=(n_items,),
        in_specs=[
            pl.BlockSpec((tm, dw), lambda w, tile, exp, lo, hi: (tile[w], 0)),
            pl.BlockSpec((None, None, d, f2), lambda w, tile, exp, lo, hi: (layer, exp[w], 0, 0)),
            pl.BlockSpec((None, None, f2 // 2, d), lambda w, tile, exp, lo, hi: (layer, exp[w], 0, 0)),
        ],
        out_specs=pl.BlockSpec((tm, dw), lambda w, tile, exp, lo, hi: (tile[w], 0)),
        scratch_shapes=[pltpu.VMEM((tm, d), F32), pltpu.VMEM((tm, f2 // 2), BF),
                        pltpu.VMEM((d, f2), BF), pltpu.VMEM((f2 // 2, d), BF)],
    )
    return pl.pallas_call(
        _experts_kernel,
        grid_spec=grid_spec,
        out_shape=jax.ShapeDtypeStruct((p, dw), jnp.uint32),
        compiler_params=_cparams("arbitrary"),
        name="experts",
    )(*items, xs, w_gu, w_down)


def _expert_items(cnt, tm, n_tiles):
    n_exp = cnt.shape[0]
    n_items = n_tiles + n_exp - 1
    end = jnp.cumsum(cnt)
    start = end - cnt
    first_tile = start // tm
    n_e = jnp.where(cnt > 0, (end - 1) // tm - first_tile + 1, 0)
    item_end = jnp.cumsum(n_e)
    w = jnp.arange(n_items, dtype=jnp.int32)
    e = jnp.minimum(jnp.searchsorted(item_end, w, side="right"), n_exp - 1).astype(jnp.int32)
    valid = w < item_end[-1]
    tile = jnp.where(valid, first_tile[e] + (w - (item_end[e] - n_e[e])), n_tiles - 1).astype(jnp.int32)
    lo = jnp.where(valid, jnp.maximum(start[e], tile * tm) - tile * tm, 0).astype(jnp.int32)
    hi = jnp.where(valid, jnp.minimum(end[e], (tile + 1) * tm) - tile * tm, 0).astype(jnp.int32)
    e = jnp.where(valid, e, e[jnp.maximum(item_end[-1] - 1, 0)])
    return tile, e, lo, hi


def _shared_kernel(h_ref, wgu_ref, wd_ref, o_ref):
    d_exp = wd_ref.shape[0]
    gu = jnp.dot(h_ref[...], wgu_ref[...], preferred_element_type=F32)
    act = _silu(gu[:, :d_exp]) * gu[:, d_exp:]
    o_ref[...] = jnp.dot(act.astype(BF), wd_ref[...], preferred_element_type=F32)


def _shared(h, w_gu, w_down):
    t, d = h.shape
    f2 = w_gu.shape[1]
    tm = _pick(t, 512)
    return pl.pallas_call(
        _shared_kernel,
        grid=(t // tm,),
        in_specs=[
            pl.BlockSpec((tm, d), lambda i: (i, 0)),
            pl.BlockSpec((d, f2), lambda i: (0, 0)),
            pl.BlockSpec((f2 // 2, d), lambda i: (0, 0)),
        ],
        out_specs=pl.BlockSpec((tm, d), lambda i: (i, 0)),
        out_shape=jax.ShapeDtypeStruct((t, d), F32),
        compiler_params=_cparams("parallel"),
        name="shared",
    )(h, w_gu, w_down)


def _combine_kernel(pos_ref, sh_ref, wts_ref, x_ref, gate_ref, gy_ref, *rest, with_next):
    if with_next:
        gx_ref, shift_ref, scale_ref, ys_hbm, xo_ref, ho_ref, buf_ref, sem = rest
    else:
        ys_hbm, xo_ref, buf_ref, sem = rest
    i = pl.program_id(0)
    n = pl.num_programs(0)
    n_k, tm = pos_ref.shape[1], pos_ref.shape[2]
    half = buf_ref.shape[3]

    def row_copy(j, t, k, slot):
        return pltpu.make_async_copy(ys_hbm.at[pl.ds(pos_ref[j, k, t], 1)], buf_ref.at[slot, k, pl.ds(t, 1)],
                                     sem.at[slot])

    def gather(j, slot):
        def issue(t, c):
            for k in range(n_k):
                row_copy(j, t, k, slot).start(priority=k % 2)
            return c
        lax.fori_loop(0, tm, issue, 0, unroll=8)

    @pl.when(i == 0)
    def _():
        gather(0, 0)

    slot = i % 2
    nxt_slot = 1 - slot

    def drain(s):
        def body(t, c):
            for k in range(n_k):
                pltpu.make_async_copy(ys_hbm.at[pl.ds(0, 1)], buf_ref.at[s, 0, pl.ds(0, 1)], sem.at[s]).wait()
            return c
        lax.fori_loop(0, tm, body, 0, unroll=8)

    drain(slot)

    acc_lo = sh_ref[:, :half]
    acc_hi = sh_ref[:, half:]
    for k in range(n_k):
        y_lo, y_hi = _unpack_rows(buf_ref[slot, k])
        wk = wts_ref[:, k:k + 1]
        acc_lo = acc_lo + wk * y_lo
        acc_hi = acc_hi + wk * y_hi
        for t in range(tm):
            row_copy(1, t, k, nxt_slot).start(priority=t % 2)

    @pl.when(i == n - 1)
    def _():
        drain(nxt_slot)

    f = jnp.concatenate([acc_lo, acc_hi], axis=1)
    xn = x_ref[...] + gate_ref[...] * (_rms(f) * gy_ref[...])
    xo_ref[...] = xn
    if with_next:
        hn = _rms(xn) * gx_ref[...]
        ho_ref[...] = (hn * (1 + scale_ref[...]) + shift_ref[...]).astype(ho_ref.dtype)


def _combine(ys, pos, wts, sh, x, gate, gy, nxt=None):
    t, d = sh.shape
    n_k = pos.shape[0]
    tm = _pick(t, 64)
    nt = t // tm
    pos3 = pos.reshape(n_k, nt, tm).transpose(1, 0, 2)
    pos_pair = jnp.stack([pos3, jnp.concatenate([pos3[1:], pos3[-1:]], axis=0)], axis=1)
    row = pl.BlockSpec((tm, d), lambda i: (i, 0))
    vec = pl.BlockSpec((1, d), lambda i: (0, 0))
    v = lambda a: a.reshape(1, d)
    vecs = [v(gate), v(gy)] + ([v(a) for a in nxt] if nxt is not None else [])
    out_specs = [row, row] if nxt is not None else row
    out_shape = [jax.ShapeDtypeStruct((t, d), F32)] + ([jax.ShapeDtypeStruct((t, d), BF)] if nxt is not None else [])
    return pl.pallas_call(
        functools.partial(_combine_kernel, with_next=nxt is not None),
        grid=(nt,),
        in_specs=[
            pl.BlockSpec((None, 2, n_k, tm), lambda i: (i, 0, 0, 0), memory_space=pltpu.SMEM),
            row,
            pl.BlockSpec((tm, n_k), lambda i: (i, 0)),
            row,
        ] + [vec] * len(vecs) + [pl.BlockSpec(memory_space=pl.ANY)],
        out_specs=out_specs,
        out_shape=out_shape if nxt is not None else out_shape[0],
        scratch_shapes=[pltpu.VMEM((2, n_k, tm, d // 2), jnp.uint32), pltpu.SemaphoreType.DMA((2,))],
        compiler_params=_cparams("arbitrary"),
        name="combine",
    )(pos_pair, sh, wts, x, *vecs, ys)


def _moe_layer(h, x, gate, gy, nxt, layer, router_w, router_bias, exp_w_gu, exp_w_down, shared_w_gu, shared_w_down):
    t = h.shape[0]
    s_gu, s_down = shared_w_gu[layer], shared_w_down[layer]
    eidx, wts, rank, cnt = _router(h, router_w[layer], router_bias[layer])
    start = jnp.cumsum(cnt) - cnt
    n_exp = cnt.shape[0]
    is_e = eidx[None] == jnp.arange(n_exp, dtype=jnp.int32)[:, None, None]
    pos = jnp.sum(jnp.where(is_e, start[:, None, None], 0), axis=0) + rank
    tm = _pick(TOP_K * t, EXPERT_TILE)
    items = _expert_items(cnt, tm, TOP_K * t // tm)
    xs = _dispatch(h, pos)
    ys = _experts(xs, items, exp_w_gu, exp_w_down, layer, tm)
    sh = _shared(h, s_gu.astype(BF), s_down.astype(BF))
    return _combine(ys, pos, wts.T, sh, x, gate, gy, nxt)


def kernel(x, c, ctx, c_ctx, w_mod, b_mod, norm_g, hgrn_w_in, hgrn_lb_logits, hgrn_norm_g, hgrn_w_out,
           pool_w, pool_scale, router_w, router_bias, exp_w_gu, exp_w_down, shared_w_gu, shared_w_down):
    batch, seq, d = x.shape
    assert batch == 1 and c.shape[0] == 1
    depth = w_mod.shape[0]
    assert depth == 2 and w_mod.shape[2] == N_MOD * d
    n_heads = hgrn_w_in.shape[2] // 5 // HEAD_DIM
    x0 = x[0]

    cvecs = jnp.concatenate([c, c_ctx[None], jnp.zeros((6, d), F32)], axis=0)
    mods = _mods(cvecs, w_mod, b_mod)
    mod = lambda layer, who, j: mods[layer, who, j * d:(j + 1) * d]

    w_in = hgrn_w_in[0].astype(BF)
    hx = _normmod(x0, norm_g[0, 0], mod(0, 0, 0), mod(0, 0, 1))
    hc = _normmod(ctx[0], norm_g[0, 0], mod(0, 1, 0), mod(0, 1, 1))
    p = _mm_heads(hx, w_in)
    pc = _mm_heads(hc, w_in)
    n_lb = hgrn_lb_logits.shape[1]
    lbl = hgrn_lb_logits.astype(F32).reshape(2, n_lb, n_heads, HEAD_DIM).transpose(0, 2, 1, 3)
    zero_state = jnp.zeros((n_heads, HEAD_DIM, HEAD_DIM), F32)
    scan = functools.partial(_scan, n_heads=n_heads, layer=0)
    _, s_f = scan(pc, lbl[0], zero_state, reverse=False, z_off=3 * n_heads)
    _, s_b = scan(pc, lbl[1], zero_state, reverse=True, z_off=4 * n_heads)
    o_f, _ = scan(p, lbl[0], s_f, reverse=False, z_off=3 * n_heads)
    o_b, _ = scan(p, lbl[1], s_b, reverse=True, z_off=4 * n_heads)
    a = _gate(o_f, o_b, p, hgrn_norm_g[0])
    y = _mm(a, hgrn_w_out[0].astype(BF), F32)
    x1, h = _resid(x0, y, mod(0, 0, 2), norm_g[0, 1], (norm_g[0, 2], mod(0, 0, 3), mod(0, 0, 4)))
    moe = functools.partial(_moe_layer, router_w=router_w, router_bias=router_bias, exp_w_gu=exp_w_gu,
                            exp_w_down=exp_w_down, shared_w_gu=shared_w_gu, shared_w_down=shared_w_down)
    x2, h = moe(h, x1, mod(0, 0, 5), norm_g[0, 3], (norm_g[1, 0], mod(1, 0, 0), mod(1, 0, 1)), 0)

    y = _pool(h, pool_w[0].astype(BF), pool_scale[0])
    x3, h = _resid(x2, y, mod(1, 0, 2), norm_g[1, 1], (norm_g[1, 2], mod(1, 0, 3), mod(1, 0, 4)))
    x4 = moe(h, x3, mod(1, 0, 5), norm_g[1, 3], None, 1)
    return x4[None]
```

```python
import functools

import jax
import jax.numpy as jnp
import numpy as np
from jax import lax
from jax.experimental import pallas as pl
from jax.experimental.pallas import tpu as pltpu

EPS = 1e-6
BF = jnp.bfloat16
F32 = jnp.float32
NEG_INF = float("-inf")

LANES = 128
HEAD_DIM = 128
SCAN_CHUNK = 128
SCAN_UNROLL = 4
SCAN_TILE = 1024
SCAN_HEADS = 2
GRID_W = 64
POOL_WINDOWS = (2, 4, 8, 16)
POOL_ROW_GROUP = 4
N_ROUTE_GROUPS = 8
TOPK_ROUTE_GROUPS = 4
TOP_K = 8
ROUTED_SCALE = 2.5
N_MOD = 6
EXPERT_TILE = 512
DOWN_CHUNK = 512
VMEM_LIMIT =56 * 1024 * 1024


def _cparams(*sem):
    return pltpu.CompilerParams(dimension_semantics=sem, vmem_limit_bytes=VMEM_LIMIT)


def _silu(v):
    return v * jax.nn.sigmoid(v)


def _rms(v):
    return v * lax.rsqrt(jnp.mean(v * v, axis=-1, keepdims=True) + EPS)


def _pick(n, want):
    t = min(n, want)
    while n % t:
        t //= 2
    assert t >= 1
    return t


def _mods_kernel(s_ref, w_ref, b_ref, o_ref):
    s = _silu(s_ref[...])
    o_ref[0] = jnp.dot(s.astype(BF), w_ref[0].astype(BF), preferred_element_type=F32) + b_ref[0]


def _mods(cvecs, w_mod, b_mod):
    depth, d, n = w_mod.shape
    tn = _pick(n, 512)
    return pl.pallas_call(
        _mods_kernel,
        grid=(depth, n // tn),
        in_specs=[
            pl.BlockSpec((8, d), lambda l, j: (0, 0)),
            pl.BlockSpec((1, d, tn), lambda l, j: (l, 0, j)),
            pl.BlockSpec((1, 1, tn), lambda l, j: (l, 0, j)),
        ],
        out_specs=pl.BlockSpec((1, 8, tn), lambda l, j: (l, 0, j)),
        out_shape=jax.ShapeDtypeStruct((depth, 8, n), F32),
        compiler_params=_cparams("parallel", "parallel"),
        name="mods",
    )(cvecs, w_mod, b_mod.reshape(depth, 1, n))


def _normmod_kernel(x_ref, g_ref, sh_ref, sc_ref, o_ref):
    y = _rms(x_ref[...]) * g_ref[...]
    o_ref[...] = (y * (1 + sc_ref[...]) + sh_ref[...]).astype(o_ref.dtype)


def _normmod(x, g, shift, scale):
    m, d = x.shape
    tm = _pick(m, 256)
    vec = pl.BlockSpec((1, d), lambda i: (0, 0))
    return pl.pallas_call(
        _normmod_kernel,
        grid=(m // tm,),
        in_specs=[pl.BlockSpec((tm, d), lambda i: (i, 0)), vec, vec, vec],
        out_specs=pl.BlockSpec((tm, d), lambda i: (i, 0)),
        out_shape=jax.ShapeDtypeStruct((m, d), BF),
        compiler_params=_cparams("parallel"),
        name="normmod",
    )(x, g.reshape(1, d), shift.reshape(1, d), scale.reshape(1, d))


def _resid_norm_kernel(x_ref, y_ref, gate_ref, gy_ref, gx_ref, sh_ref, sc_ref, xo_ref, ho_ref):
    yn = _rms(y_ref[...].astype(F32)) * gy_ref[...]
    xn = x_ref[...] + gate_ref[...] * yn
    xo_ref[...] = xn
    hn = _rms(xn) * gx_ref[...]
    ho_ref[...] = (hn * (1 + sc_ref[...]) + sh_ref[...]).astype(ho_ref.dtype)


def _resid(x, y, gate, gy, nxt):
    m, d = x.shape
    tm = _pick(m, 256)
    row = pl.BlockSpec((tm, d), lambda i: (i, 0))
    vec = pl.BlockSpec((1, d), lambda i: (0, 0))
    v = lambda a: a.reshape(1, d)
    gx, shift, scale = nxt
    return pl.pallas_call(
        _resid_norm_kernel,
        grid=(m // tm,),
        in_specs=[row, row, vec, vec, vec, vec, vec],
        out_specs=[row, row],
        out_shape=[jax.ShapeDtypeStruct((m, d), F32), jax.ShapeDtypeStruct((m, d), BF)],
        compiler_params=_cparams("parallel"),
        name="resid_norm",
    )(x, y, v(gate), v(gy), v(gx), v(shift), v(scale))


def _mm_kernel(a_ref, w_ref, o_ref):
    o_ref[...] = jnp.dot(a_ref[...], w_ref[...], preferred_element_type=F32).astype(o_ref.dtype)


def _mm(a, w, out_dtype):
    m, k = a.shape
    n = w.shape[1]
    tm, tn = _pick(m, 1024), _pick(n, 1024)
    return pl.pallas_call(
        _mm_kernel,
        grid=(m // tm, n // tn),
        in_specs=[pl.BlockSpec((tm, k), lambda i, j: (i, 0)), pl.BlockSpec((k, tn), lambda i, j: (0, j))],
        out_specs=pl.BlockSpec((tm, tn), lambda i, j: (i, j)),
        out_shape=jax.ShapeDtypeStruct((m, n), out_dtype),
        compiler_params=_cparams("parallel", "parallel"),
        name="mm",
    )(a, w)


def _mm_heads_kernel(a_ref, w_ref, o_ref):
    r = jnp.dot(a_ref[...], w_ref[...], preferred_element_type=F32)
    for hh in range(o_ref.shape[0]):
        o_ref[hh] = r[:, hh * LANES:(hh + 1) * LANES].astype(o_ref.dtype)


def _mm_heads(a, w):
    m, k = a.shape
    n = w.shape[1]
    tm, tn = _pick(m, 1024), _pick(n, 1024)
    return pl.pallas_call(
        _mm_heads_kernel,
        grid=(m // tm, n // tn),
        in_specs=[pl.BlockSpec((tm, k), lambda i, j: (i, 0)), pl.BlockSpec((k, tn), lambda i, j: (0, j))],
        out_specs=pl.BlockSpec((tn // LANES, tm, LANES), lambda i, j: (j, i, 0)),
        out_shape=jax.ShapeDtypeStruct((n // LANES, m, LANES), BF),
        compiler_params=_cparams("parallel", "parallel"),
        name="proj",
    )(a, w)


def _scan_tables(c, reverse):
    n_levels = c.bit_length() - 1
    assert c == 1 << n_levels
    t = np.arange(c)[:, None]
    u = np.arange(c)[None, :]
    dmats = [u <= t]
    masks = [u == t]
    roles = []
    for lv in range(n_levels):
        size = c >> lv
        start = (t // size) * size
        boundary = start + size // 2 - 1
        later = t > boundary
        if (size // 2) % 8:
            dmats.append(np.where(later, (u > boundary) & (u <= t), (u > t) & (u <= boundary)))
        masks.append(later & (u <= boundary.T) & (start == start.T))
        roles.append(np.broadcast_to(later, (c, HEAD_DIM)))
    dmat, mask, role = np.stack(dmats), np.stack(masks), np.stack(roles)
    if reverse:
        dmat, mask, role = dmat[:, ::-1, ::-1], mask[:, ::-1, ::-1], role[:, ::-1]
    return (jnp.asarray(dmat.reshape(-1, c), BF), jnp.asarray(mask, F32), jnp.asarray(role, F32))


def _scan_kernel(q_ref, v_ref, z_ref, lbl_ref, s0_ref, dmat_ref, mask_ref, roles_ref, *rest, layer, reverse, gated):
    if gated:
        oo_ref, g_ref, ng_ref, o_ref, sfin_ref, st_ref = rest
    else:
        o_ref, sfin_ref, st_ref = rest
    c = SCAN_CHUNK
    n_levels = roles_ref.shape[0]
    n_h, t_tile = q_ref.shape[0], q_ref.shape[1]
    span = min(SCAN_UNROLL * c, t_tile)
    n_spans = t_tile // span
    n_c = span // c
    i = pl.program_id(1)

    @pl.when(i == 0)
    def _():
        st_ref[...] = s0_ref[...]

    lbs = []
    for h in range(n_h):
        lg = lbl_ref[h]
        e = jnp.exp(lg - jnp.max(lg, axis=0, keepdims=True))
        p = e / jnp.sum(e, axis=0, keepdims=True)
        lbs.append(jnp.sum(p[:layer + 1], axis=0, keepdims=True))

    nt_dims = (((1,), (1,)), ((), ()))
    tn_dims = (((0,), (0,)), ((), ()))
    last = 0 if reverse else c - 1

    def level_rows(lv, qc, kc):
        size = c >> lv
        half = size // 2
        if half % 8:
            return jnp.where(roles_ref[lv] > 0, qc, kc)
        first, second = (qc, kc) if reverse else (kc, qc)
        parts = []
        for a in range(0, c, size):
            parts += [first[a:a + half], second[a + half:a + size]]
        return jnp.concatenate(parts, axis=0)

    def level_exponent(lv, b):
        size = c >> lv
        half = size // 2
        parts = []
        for a in range(0, c, size):
            if reverse:
                bnd = b[a + half:a + half + 1]
                parts += [b[a:a + half] - bnd, bnd - b[a + half:a + size]]
            else:
                bnd = b[a + half - 1:a + half]
                parts += [bnd - b[a:a + half], b[a + half:a + size] - bnd]
        return jnp.concatenate(parts, axis=0)

    def chunk(st, lf, qc, kc, vc):
        ex = jnp.dot(dmat_ref[...], lf.astype(BF), preferred_element_type=F32)
        b = ex[0:c]
        b_last = b[last:last + 1]
        qb, kb, vb = qc.astype(BF), kc.astype(BF), vc.astype(BF)
        att = lax.dot_general(qb, kb, nt_dims, preferred_element_type=F32) * mask_ref[0]
        n_small = 0
        for lv in range(n_levels):
            if (c >> lv) // 2 % 8:
                n_small += 1
                d = ex[n_small * c:(n_small + 1) * c]
            else:
                d = level_exponent(lv, b)
            xb = (level_rows(lv, qc, kc) * jnp.exp2(d)).astype(BF)
            gram = lax.dot_general(xb, xb, nt_dims, preferred_element_type=F32)
            att = att + gram * mask_ref[1 + lv]
        qd = (qc * jnp.exp2(b)).astype(BF)
        o = (jnp.dot(att.astype(BF), vb, preferred_element_type=F32)
             + lax.dot_general(qd, st.astype(BF), nt_dims, preferred_element_type=F32))
        kd = (kc * jnp.exp2(b_last - b)).astype(BF)
        st_new = st * jnp.exp2(b_last) + lax.dot_general(vb, kd, tn_dims, preferred_element_type=F32)
        return st_new, o

    def block(n, sts):
        j = (n_spans - 1 - n) if reverse else n
        rows = pl.ds(pl.multiple_of(j * span, span), span)
        sts = list(sts)
        pre = []
        for h in range(n_h):
            z = z_ref[h, rows, :].astype(F32)
            f = lbs[h] + (1 - lbs[h]) * jax.nn.sigmoid(z)
            pre.append((jnp.log2(f), _silu(q_ref[h, rows, :].astype(F32)), 1 - f, v_ref[h, rows, :].astype(F32)))
        outs = [[None] * n_c for _ in range(n_h)]
        for m in (range(n_c - 1, -1, -1) if reverse else range(n_c)):
            sl = slice(m * c, (m + 1) * c)
            for h in range(n_h):
                logf, q, kk, v = pre[h]
                sts[h], outs[h][m] = chunk(sts[h], logf[sl], q[sl], kk[sl], v[sl])
        for h in range(n_h):
            o = jnp.concatenate(outs[h], axis=0)
            if gated:
                o = _rms(o + oo_ref[h, rows, :].astype(F32)) * ng_ref[h]
                o_ref[rows, h * HEAD_DIM:(h + 1) * HEAD_DIM] = (
                    o * _silu(g_ref[h, rows, :].astype(F32))).astype(o_ref.dtype)
            else:
                o_ref[h, rows, :] = o.astype(o_ref.dtype)
        return tuple(sts)

    sts = lax.fori_loop(0, n_spans, block, tuple(st_ref[h] for h in range(n_h)))
    for h in range(n_h):
        st_ref[h] = sts[h]

    @pl.when(i == pl.num_programs(1) - 1)
    def _():
        sfin_ref[...] = st_ref[...]


def _scan(p, lb_logits, s0, *, n_heads, layer, reverse, z_off, gate=None):
    _, l, _ = p.shape
    t_tile = _pick(l, SCAN_TILE)
    assert t_tile % SCAN_CHUNK == 0
    nt = l // t_tile
    tok = (lambda i: nt - 1 - i) if reverse else (lambda i: i)
    hp = _pick(n_heads, SCAN_HEADS)
    blk = lambda off: pl.BlockSpec((hp, t_tile, HEAD_DIM), lambda h, i: (off // hp + h, tok(i), 0))
    assert n_heads % hp == 0 and z_off % hp == 0
    n_lb = lb_logits.shape[1]
    dmat, mask, roles = _scan_tables(SCAN_CHUNK, reverse)
    whole = lambda a: pl.BlockSpec(a.shape, lambda h, i: (0,) * a.ndim)
    in_specs = [
        blk(0), blk(2 * n_heads), blk(z_off),
        pl.BlockSpec((hp, n_lb, HEAD_DIM), lambda h, i: (h, 0, 0)),
        pl.BlockSpec((hp, HEAD_DIM, HEAD_DIM), lambda h, i: (h, 0, 0)),
        whole(dmat), whole(mask), whole(roles),
    ]
    args = [p, p, p, lb_logits, s0, dmat, mask, roles]
    if gate is None:
        o_spec = pl.BlockSpec((hp, t_tile, HEAD_DIM), lambda h, i: (h, tok(i), 0))
        o_shape = jax.ShapeDtypeStruct((n_heads, l, HEAD_DIM), BF)
    else:
        o_other, norm_g = gate
        in_specs += [blk(0), blk(n_heads), pl.BlockSpec((hp, 1, HEAD_DIM), lambda h, i: (h, 0, 0))]
        args += [o_other, p, norm_g.reshape(n_heads, 1, HEAD_DIM)]
        o_spec = pl.BlockSpec((t_tile, hp * HEAD_DIM), lambda h, i: (tok(i), h))
        o_shape = jax.ShapeDtypeStruct((l, n_heads * HEAD_DIM), BF)
    return pl.pallas_call(
        functools.partial(_scan_kernel, layer=layer, reverse=reverse, gated=gate is not None),
        grid=(n_heads // hp, nt),
        in_specs=in_specs,
        out_specs=[o_spec, pl.BlockSpec((hp, HEAD_DIM, HEAD_DIM), lambda h, i: (h, 0, 0))],
        out_shape=[o_shape, jax.ShapeDtypeStruct((n_heads, HEAD_DIM, HEAD_DIM), F32)],
        scratch_shapes=[pltpu.VMEM((hp, HEAD_DIM, HEAD_DIM), F32)],
        compiler_params=_cparams("parallel", "arbitrary"),
        name="scan_bwd" if reverse else "scan_fwd",
    )(*args)


def _pool_body(win, prev_ref, cur_ref, next_ref, w_ref, scale_ref, o_ref, y_ref, d_ref, n_rows):
    i = pl.program_id(0)
    tile_rows = cur_ref.shape[0] // GRID_W
    halo = y_ref.shape[0] - tile_rows
    top = halo // 2
    half = win // 2
    r0 = i * tile_rows

    rg = POOL_ROW_GROUP
    gt = rg * GRID_W
    shift = GRID_W.bit_length() - 1
    assert GRID_W == 1 << shift and tile_rows % rg == 0 and top % rg == 0
    t_io = lax.broadcasted_iota(jnp.int32, (gt, gt), 0)
    s_io = lax.broadcasted_iota(jnp.int32, (gt, gt), 1)
    same_row = lax.shift_right_logical(t_io, shift) == lax.shift_right_logical(s_io, shift)
    off = (s_io & (GRID_W - 1)) - (t_io & (GRID_W - 1))
    a01 = jnp.where(same_row & (off >= -half) & (off < win - half), 1.0, 0.0).astype(BF)

    def col_filter(src_ref, src_row, dst_row):
        xs = src_ref[pl.ds(pl.multiple_of(src_row * GRID_W, gt), gt), :]
        y = jnp.dot(a01, xs, preferred_element_type=F32)
        grow = jnp.full((1, y.shape[1]), r0 - top + dst_row, jnp.int32)
        ok = (grow >= 0) & (grow < n_rows)
        y_ref[pl.ds(dst_row, rg)] = jnp.where(ok, y, 0.0).reshape(rg, GRID_W, y.shape[1])

    n_above = -(-half // rg)
    n_below = -(-(half - 1) // rg)

    def above(j, c):
        col_filter(prev_ref, tile_rows - (n_above - j) * rg, top - (n_above - j) * rg)
        return c

    def inside(j, c):
        col_filter(cur_ref, j * rg, top + j * rg)
        return c

    def below(j, c):
        col_filter(next_ref, j * rg, top + tile_rows + j * rg)
        return c

    lax.fori_loop(0, n_above, above, 0)
    lax.fori_loop(0, tile_rows // rg, inside, 0)
    lax.fori_loop(0, n_below, below, 0)

    c_io = lax.broadcasted_iota(jnp.int32, (GRID_W, d_ref.shape[1]), 0)
    cnt_c = jnp.minimum(c_io + win - half, GRID_W) - jnp.maximum(c_io - half, 0)
    inv_c = 1.0 / cnt_c.astype(F32)

    def row_filter(r, c):
        grow = jnp.full((1, d_ref.shape[1]), r0 + r, jnp.int32)
        cnt = jnp.minimum(grow + win - half, n_rows) - jnp.maximum(grow - half, 0)
        z = y_ref[top + r - half]
        for j in range(1, win):
            z = z + y_ref[top + r - half + j]
        z = z * (inv_c * (1.0 / cnt.astype(F32)))
        sl = pl.ds(pl.multiple_of(r * GRID_W, GRID_W), GRID_W)
        d_ref[sl, :] = (z - cur_ref[sl, :].astype(F32)).astype(d_ref.dtype)
        return c

    lax.fori_loop(0, tile_rows, row_filter, 0)
    o_ref[...] = jnp.dot(d_ref[...], w_ref[0], preferred_element_type=F32) * scale_ref[...]


def _pool_kernel(prev_ref, cur_ref, next_ref, w_ref, scale_ref, o_ref, y_ref, d_ref, *, n_rows):
    g = pl.program_id(1)
    for gi, win in enumerate(POOL_WINDOWS):
        @pl.when(g == gi)
        def _(win=win):
            _pool_body(win, prev_ref, cur_ref, next_ref, w_ref, scale_ref, o_ref, y_ref, d_ref, n_rows)


def _pool(h, w_pool, scale):
    l, d = h.shape
    n_groups, gc, _ = w_pool.shape
    n_rows = l // GRID_W
    tile_rows = _pick(n_rows, 16)
    halo = max(POOL_WINDOWS)
    assert tile_rows >= halo // 2 and len(POOL_WINDOWS) == n_groups
    tt = tile_rows * GRID_W
    nt = l // tt
    return pl.pallas_call(
        functools.partial(_pool_kernel, n_rows=n_rows),
        grid=(nt, n_groups),
        in_specs=[
            pl.BlockSpec((tt, gc), lambda i, g: (jnp.maximum(i - 1, 0), g)),
            pl.BlockSpec((tt, gc), lambda i, g: (i, g)),
            pl.BlockSpec((tt, gc), lambda i, g: (jnp.minimum(i + 1, nt - 1), g)),
            pl.BlockSpec((1, gc, gc), lambda i, g: (g, 0, 0)),
            pl.BlockSpec((1, gc), lambda i, g: (0, g)),
        ],
        out_specs=pl.BlockSpec((tt, gc), lambda i, g: (i, g)),
        out_shape=jax.ShapeDtypeStruct((l, d), F32),
        scratch_shapes=[pltpu.VMEM((tile_rows + halo, GRID_W, gc), F32), pltpu.VMEM((tt, gc), BF)],
        compiler_params=_cparams("parallel", "parallel"),
        name="pool",
    )(h, h, h, w_pool, scale.reshape(1, d))


def _router_kernel(h_ref, wrt_ref, bias_ref, upper_ref, eidx_ref, wts_ref, rank_ref, cnt_ref, carry_ref):
    n_exp = wrt_ref.shape[0]
    tm = h_ref.shape[0]

    @pl.when(pl.program_id(0) == 0)
    def _():
        carry_ref[...] = jnp.zeros_like(carry_ref)

    epg = n_exp // N_ROUTE_GROUPS
    logits = lax.dot_general(wrt_ref[...], h_ref[...], (((1,), (1,)), ((), ())), preferred_element_type=F32)
    scores = jax.nn.sigmoid(logits)
    shape3 = (N_ROUTE_GROUPS, epg, tm)
    sc3 = scores.reshape(shape3)
    s3 = (scores + bias_ref[...]).reshape(shape3)
    e_in = lax.broadcasted_iota(jnp.int32, shape3, 1).astype(F32)
    g_io = lax.broadcasted_iota(jnp.int32, shape3, 0).astype(F32)

    m1 = jnp.max(s3, axis=1, keepdims=True)
    i1 = jnp.min(jnp.where(s3 == m1, e_in, epg), axis=1, keepdims=True)
    m2 = jnp.max(jnp.where(e_in == i1, NEG_INF, s3), axis=1, keepdims=True)
    gs = jnp.broadcast_to(m1 + m2, shape3)

    keep = jnp.zeros(shape3, F32)
    cur = gs
    for _ in range(TOPK_ROUTE_GROUPS):
        m = jnp.max(cur, axis=0, keepdims=True)
        idx = jnp.min(jnp.where(cur == m, g_io, N_ROUTE_GROUPS), axis=0, keepdims=True)
        hit = g_io == idx
        keep = jnp.where(hit, 1.0, keep)
        cur = jnp.where(hit, NEG_INF, cur)

    e_io = g_io * epg + e_in
    chosen = jnp.zeros(shape3, F32)
    cur = jnp.where(keep > 0, s3, NEG_INF)
    picks = []
    for _ in range(TOP_K):
        m = jnp.max(jnp.max(cur, axis=0, keepdims=True), axis=1, keepdims=True)
        idx = jnp.where(cur == m, e_io, n_exp)
        idx = jnp.min(jnp.min(idx, axis=0, keepdims=True), axis=1, keepdims=True)
        hit = e_io == idx
        chosen = jnp.where(hit, 1.0, chosen)
        cur = jnp.where(hit, NEG_INF, cur)
        picks.append(idx)

    w = jnp.where(chosen > 0, sc3, 0.0)
    den = jnp.sum(jnp.sum(w, axis=0, keepdims=True), axis=1, keepdims=True)
    w = w / den * ROUTED_SCALE

    ch = chosen.reshape(n_exp, tm).astype(BF)
    carry = carry_ref[...]
    rank = (jnp.dot(ch, upper_ref[...], preferred_element_type=F32)
            + jnp.concatenate([carry] * (tm // LANES), axis=1)).reshape(shape3)
    carry = carry + jnp.dot(ch, jnp.ones((tm, LANES), BF), preferred_element_type=F32)
    carry_ref[...] = carry
    cnt_ref[...] = carry

    def per_pick(vals, idx):
        sel = jnp.where(e_io == idx, vals, 0.0)
        return jnp.sum(jnp.sum(sel, axis=0, keepdims=True), axis=1, keepdims=True).reshape(1, tm)

    eidx_ref[...] = jnp.concatenate([idx.reshape(1, tm) for idx in picks], axis=0).astype(jnp.int32)
    wts_ref[...] = jnp.concatenate([per_pick(w, idx) for idx in picks], axis=0)
    rank_ref[...] = jnp.concatenate([per_pick(rank, idx) for idx in picks], axis=0).astype(jnp.int32)


def _router(h, w_r, b_r):
    t, d = h.shape
    n_exp = w_r.shape[1]
    tm = _pick(t, 512)
    assert tm % LANES == 0
    upper = jnp.asarray(np.triu(np.ones((tm, tm), np.float32), 1), BF)
    pick = pl.BlockSpec((TOP_K, tm), lambda i: (0, i))
    eidx, wts, rank, cnt = pl.pallas_call(
        _router_kernel,
        grid=(t // tm,),
        in_specs=[
            pl.BlockSpec((tm, d), lambda i: (i, 0)),
            pl.BlockSpec((n_exp, d), lambda i: (0, 0)),
            pl.BlockSpec((n_exp, 1), lambda i: (0, 0)),
            pl.BlockSpec((tm, tm), lambda i: (0, 0)),
        ],
        out_specs=[pick, pick, pick, pl.BlockSpec((n_exp, LANES), lambda i: (0, 0))],
        out_shape=[
            jax.ShapeDtypeStruct((TOP_K, t), jnp.int32),
            jax.ShapeDtypeStruct((TOP_K, t), F32),
            jax.ShapeDtypeStruct((TOP_K, t), jnp.int32),
            jax.ShapeDtypeStruct((n_exp, LANES), F32),
        ],
        scratch_shapes=[pltpu.VMEM((n_exp, LANES), F32)],
        compiler_params=_cparams("arbitrary"),
        name="router",
    )(h, w_r.T.astype(BF), b_r.astype(F32).reshape(n_exp, 1), upper)
    return eidx, wts, rank, cnt[:, 0].astype(jnp.int32)


HI_MASK = 0xFFFF0000


def _pack_halves(lo, hi):
    bits = lambda a: pltpu.bitcast(a.astype(BF).astype(F32), jnp.uint32)
    return lax.shift_right_logical(bits(lo), jnp.uint32(16)) | (bits(hi) & jnp.uint32(HI_MASK))


def _pack_rows(v):
    half = v.shape[1] // 2
    return _pack_halves(v[:, :half], v[:, half:])


def _unpack_rows(w):
    lo = pltpu.bitcast(lax.shift_left(w, jnp.uint32(16)), F32)
    hi = pltpu.bitcast(w & jnp.uint32(HI_MASK), F32)
    return lo, hi


def _dispatch_kernel(pos_ref, h_ref, xs_hbm, hp_ref, sem):
    i = pl.program_id(0)
    n_k, tm = pos_ref.shape
    slot = i % 2
    hp_ref[slot] = _pack_rows(h_ref[...].astype(F32))

    def issue(t, c):
        for k in range(n_k):
            pltpu.make_async_copy(hp_ref.at[slot, pl.ds(t, 1)], xs_hbm.at[pl.ds(pos_ref[k, t], 1)],
                                  sem.at[slot]).start(priority=k % 2)
        return c

    lax.fori_loop(0, tm, issue, 0, unroll=8)

    def drain(s):
        def body(t, c):
            for k in range(n_k):
                pltpu.make_async_copy(hp_ref.at[s, pl.ds(0, 1)], xs_hbm.at[pl.ds(0, 1)], sem.at[s]).wait()
            return c
        lax.fori_loop(0, tm, body, 0, unroll=8)

    @pl.when(i > 0)
    def _():
        drain(1 - slot)

    @pl.when(i == pl.num_programs(0) - 1)
    def _():
        drain(slot)


def _dispatch(h, pos):
    t, d = h.shape
    n_k = pos.shape[0]
    tm = _pick(t, 256)
    return pl.pallas_call(
        _dispatch_kernel,
        grid=(t // tm,),
        in_specs=[
            pl.BlockSpec((n_k, tm), lambda i: (0, i), memory_space=pltpu.SMEM),
            pl.BlockSpec((tm, d), lambda i: (i, 0)),
        ],
        out_specs=pl.BlockSpec(memory_space=pl.ANY),
        out_shape=jax.ShapeDtypeStruct((n_k * t, d // 2), jnp.uint32),
        scratch_shapes=[pltpu.VMEM((2, tm, d // 2), jnp.uint32), pltpu.SemaphoreType.DMA((2,))],
        compiler_params=_cparams("arbitrary"),
        name="dispatch",
    )(pos, h)


def _experts_kernel(tile_ref, exp_ref, lo_ref, hi_ref, xs_ref, wgu_ref, wd_ref, ys_ref,
                    acc_ref, act_ref, wgu_bf_ref, wd_bf_ref):
    w = pl.program_id(0)
    prev = jnp.maximum(w - 1, 0)
    tm = xs_ref.shape[0]
    half = xs_ref.shape[1]
    d_exp = wd_ref.shape[0]

    @pl.when(jnp.logical_or(w == 0, exp_ref[w] != exp_ref[prev]))
    def _():
        wgu_bf_ref[...] = wgu_ref[...].astype(BF)
        wd_bf_ref[...] = wd_ref[...].astype(BF)

    live = hi_ref[w] > lo_ref[w]
    first = jnp.logical_or(w == 0, tile_ref[w] != tile_ref[prev])

    @pl.when(live)
    def _():
        x_lo, x_hi = _unpack_rows(xs_ref[...])
        gu = (jnp.dot(x_lo.astype(BF), wgu_bf_ref[:half, :], preferred_element_type=F32)
              + jnp.dot(x_hi.astype(BF), wgu_bf_ref[half:, :], preferred_element_type=F32))
        row = lax.broadcasted_iota(jnp.int32, (tm, d_exp), 0)
        mine = (row >= lo_ref[w]) & (row < hi_ref[w])
        act_ref[...] = jnp.where(mine, _silu(gu[:, :d_exp]) * gu[:, d_exp:], 0.0).astype(BF)

    def down(accumulate):
        act = act_ref[...]
        chunk = min(DOWN_CHUNK, half)
        assert half % chunk == 0
        for j in range(half // chunk):
            c_lo = slice(j * chunk, (j + 1) * chunk)
            c_hi = slice(half + j * chunk, half + (j + 1) * chunk)
            y_lo = jnp.dot(act, wd_bf_ref[:, c_lo], preferred_element_type=F32)
            y_hi = jnp.dot(act, wd_bf_ref[:, c_hi], preferred_element_type=F32)
            if accumulate:
                y_lo = y_lo + acc_ref[:, c_lo]
                y_hi = y_hi + acc_ref[:, c_hi]
            acc_ref[:, c_lo] = y_lo
            acc_ref[:, c_hi] = y_hi
            ys_ref[:, c_lo] = _pack_halves(y_lo, y_hi)

    @pl.when(jnp.logical_and(live, first))
    def _():
        down(False)

    @pl.when(jnp.logical_and(live, jnp.logical_not(first)))
    def _():
        down(True)


def _experts(xs, items, w_gu, w_down, layer, tm):
    p, dw = xs.shape
    _, _, d, f2 = w_gu.shape
    n_items = items[0].shape[0]
    grid_spec = pltpu.PrefetchScalarGridSpec(
        num_scalar_prefetch=4,
        grid=(n_items,),
        in_specs=[
            pl.BlockSpec((tm, dw), lambda w, tile, exp, lo, hi: (tile[w], 0)),
            pl.BlockSpec((None, None, d, f2), lambda w, tile, exp, lo, hi: (layer, exp[w], 0, 0)),
            pl.BlockSpec((None, None, f2 // 2, d), lambda w, tile, exp, lo, hi: (layer, exp[w], 0, 0)),
        ],
        out_specs=pl.BlockSpec((tm, dw), lambda w, tile, exp, lo, hi: (tile[w], 0)),
        scratch_shapes=[pltpu.VMEM((tm, d), F32), pltpu.VMEM((tm, f2 // 2), BF),
                        pltpu.VMEM((d, f2), BF), pltpu.VMEM((f2 // 2, d), BF)],
    )
    return pl.pallas_call(
        _experts_kernel,
        grid_spec=grid_spec,
        out_shape=jax.ShapeDtypeStruct((p, dw), jnp.uint32),
        compiler_params=_cparams("arbitrary"),
        name="experts",
    )(*items, xs, w_gu, w_down)


def _expert_items(cnt, tm, n_tiles):
    n_exp = cnt.shape[0]
    n_items = n_tiles + n_exp - 1
    end = jnp.cumsum(cnt)
    start = end - cnt
    first_tile = start // tm
    n_e = jnp.where(cnt > 0, (end - 1) // tm - first_tile + 1, 0)
    item_end = jnp.cumsum(n_e)
    w = jnp.arange(n_items, dtype=jnp.int32)
    e = jnp.minimum(jnp.searchsorted(item_end, w, side="right"), n_exp - 1).astype(jnp.int32)
    valid = w < item_end[-1]
    tile = jnp.where(valid, first_tile[e] + (w - (item_end[e] - n_e[e])), n_tiles - 1).astype(jnp.int32)
    lo = jnp.where(valid, jnp.maximum(start[e], tile * tm) - tile * tm, 0).astype(jnp.int32)
    hi = jnp.where(valid, jnp.minimum(end[e], (tile + 1) * tm) - tile * tm, 0).astype(jnp.int32)
    e = jnp.where(valid, e, e[jnp.maximum(item_end[-1] - 1, 0)])
    return tile, e, lo, hi


def _shared_kernel(h_ref, wgu_ref, wd_ref, o_ref):
    d_exp = wd_ref.shape[0]
    gu = jnp.dot(h_ref[...], wgu_ref[...], preferred_element_type=F32)
    act = _silu(gu[:, :d_exp]) * gu[:, d_exp:]
    o_ref[...] = jnp.dot(act.astype(BF), wd_ref[...], preferred_element_type=F32)


def _shared(h, w_gu, w_down):
    t, d = h.shape
    f2 = w_gu.shape[1]
    tm = _pick(t, 512)
    return pl.pallas_call(
        _shared_kernel,
        grid=(t // tm,),
        in_specs=[
            pl.BlockSpec((tm, d), lambda i: (i, 0)),
            pl.BlockSpec((d, f2), lambda i: (0, 0)),
            pl.BlockSpec((f2 // 2, d), lambda i: (0, 0)),
        ],
        out_specs=pl.BlockSpec((tm, d), lambda i: (i, 0)),
        out_shape=jax.ShapeDtypeStruct((t, d), F32),
        compiler_params=_cparams("parallel"),
        name="shared",
    )(h, w_gu, w_down)


def _combine_kernel(pos_ref, sh_ref, wts_ref, x_ref, gate_ref, gy_ref, *rest, with_next):
    if with_next:
        gx_ref, shift_ref, scale_ref, ys_hbm, xo_ref, ho_ref, buf_ref, sem = rest
    else:
        ys_hbm, xo_ref, buf_ref, sem = rest
    i = pl.program_id(0)
    n = pl.num_programs(0)
    n_k, tm = pos_ref.shape[1], pos_ref.shape[2]
    half = buf_ref.shape[3]

    def row_copy(j, t, k, slot):
        return pltpu.make_async_copy(ys_hbm.at[pl.ds(pos_ref[j, k, t], 1)], buf_ref.at[slot, k, pl.ds(t, 1)],
                                     sem.at[slot])

    def gather(j, slot):
        def issue(t, c):
            for k in range(n_k):
                row_copy(j, t, k, slot).start(priority=k % 2)
            return c
        lax.fori_loop(0, tm, issue, 0, unroll=8)

    @pl.when(i == 0)
    def _():
        gather(0, 0)

    slot = i % 2
    nxt_slot = 1 - slot

    def drain(s):
        def body(t, c):
            for k in range(n_k):
                pltpu.make_async_copy(ys_hbm.at[pl.ds(0, 1)], buf_ref.at[s, 0, pl.ds(0, 1)], sem.at[s]).wait()
            return c
        lax.fori_loop(0, tm, body, 0, unroll=8)

    drain(slot)

    acc_lo = sh_ref[:, :half]
    acc_hi = sh_ref[:, half:]
    for k in range(n_k):
        y_lo, y_hi = _unpack_rows(buf_ref[slot, k])
        wk = wts_ref[:, k:k + 1]
        acc_lo = acc_lo + wk * y_lo
        acc_hi = acc_hi + wk * y_hi
        for t in range(tm):
            row_copy(1, t, k, nxt_slot).start(priority=t % 2)

    @pl.when(i == n - 1)
    def _():
        drain(nxt_slot)

    f = jnp.concatenate([acc_lo, acc_hi], axis=1)
    xn = x_ref[...] + gate_ref[...] * (_rms(f) * gy_ref[...])
    xo_ref[...] = xn
    if with_next:
        hn = _rms(xn) * gx_ref[...]
        ho_ref[...] = (hn * (1 + scale_ref[...]) + shift_ref[...]).astype(ho_ref.dtype)


def _combine(ys, pos, wts, sh, x, gate, gy, nxt=None):
    t, d = sh.shape
    n_k = pos.shape[0]
    tm = _pick(t, 64)
    nt = t // tm
    pos3 = pos.reshape(n_k, nt, tm).transpose(1, 0, 2)
    pos_pair = jnp.stack([pos3, jnp.concatenate([pos3[1:], pos3[-1:]], axis=0)], axis=1)
    row = pl.BlockSpec((tm, d), lambda i: (i, 0))
    vec = pl.BlockSpec((1, d), lambda i: (0, 0))
    v = lambda a: a.reshape(1, d)
    vecs = [v(gate), v(gy)] + ([v(a) for a in nxt] if nxt is not None else [])
    out_specs = [row, row] if nxt is not None else row
    out_shape = [jax.ShapeDtypeStruct((t, d), F32)] + ([jax.ShapeDtypeStruct((t, d), BF)] if nxt is not None else [])
    return pl.pallas_call(
        functools.partial(_combine_kernel, with_next=nxt is not None),
        grid=(nt,),
        in_specs=[
            pl.BlockSpec((None, 2, n_k, tm), lambda i: (i, 0, 0, 0), memory_space=pltpu.SMEM),
            row,
            pl.BlockSpec((tm, n_k), lambda i: (i, 0)),
            row,
        ] + [vec] * len(vecs) + [pl.BlockSpec(memory_space=pl.ANY)],
        out_specs=out_specs,
        out_shape=out_shape if nxt is not None else out_shape[0],
        scratch_shapes=[pltpu.VMEM((2, n_k, tm, d // 2), jnp.uint32), pltpu.SemaphoreType.DMA((2,))],
        compiler_params=_cparams("arbitrary"),
        name="combine",
    )(pos_pair, sh, wts, x, *vecs, ys)


def _moe_layer(h, x, gate, gy, nxt, layer, router_w, router_bias, exp_w_gu, exp_w_down, shared_w_gu, shared_w_down):
    t = h.shape[0]
    s_gu, s_down = shared_w_gu[layer], shared_w_down[layer]
    eidx, wts, rank, cnt = _router(h, router_w[layer], router_bias[layer])
    start = jnp.cumsum(cnt) - cnt
    n_exp = cnt.shape[0]
    is_e = eidx[None] == jnp.arange(n_exp, dtype=jnp.int32)[:, None, None]
    pos = jnp.sum(jnp.where(is_e, start[:, None, None], 0), axis=0) + rank
    tm = _pick(TOP_K * t, EXPERT_TILE)
    items = _expert_items(cnt, tm, TOP_K * t // tm)
    xs = _dispatch(h, pos)
    ys = _experts(xs, items, exp_w_gu, exp_w_down, layer, tm)
    sh = _shared(h, s_gu.astype(BF), s_down.astype(BF))
    return _combine(ys, pos, wts.T, sh, x, gate, gy, nxt)


def kernel(x, c, ctx, c_ctx, w_mod, b_mod, norm_g, hgrn_w_in, hgrn_lb_logits, hgrn_norm_g, hgrn_w_out,
           pool_w, pool_scale, router_w, router_bias, exp_w_gu, exp_w_down, shared_w_gu, shared_w_down):
    batch, seq, d = x.shape
    assert batch == 1 and c.shape[0] == 1
    depth = w_mod.shape[0]
    assert depth == 2 and w_mod.shape[2] == N_MOD * d
    n_heads = hgrn_w_in.shape[2] // 5 // HEAD_DIM
    x0 = x[0]

    cvecs = jnp.concatenate([c, c_ctx[None], jnp.zeros((6, d), F32)], axis=0)
    mods = _mods(cvecs, w_mod, b_mod)
    mod = lambda layer, who, j: mods[layer, who, j * d:(j + 1) * d]

    w_in = hgrn_w_in[0].astype(BF)
    hx = _normmod(x0, norm_g[0, 0], mod(0, 0, 0), mod(0, 0, 1))
    hc = _normmod(ctx[0], norm_g[0, 0], mod(0, 1, 0), mod(0, 1, 1))
    p = _mm_heads(hx, w_in)
    pc = _mm_heads(hc, w_in)
    n_lb = hgrn_lb_logits.shape[1]
    lbl = hgrn_lb_logits.astype(F32).reshape(2, n_lb, n_heads, HEAD_DIM).transpose(0, 2, 1, 3)
    zero_state = jnp.zeros((n_heads, HEAD_DIM, HEAD_DIM), F32)
    scan = functools.partial(_scan, n_heads=n_heads, layer=0)
    _, s_f = scan(pc, lbl[0], zero_state, reverse=False, z_off=3 * n_heads)
    _, s_b = scan(pc, lbl[1], zero_state, reverse=True, z_off=4 * n_heads)
    o_f, _ = scan(p, lbl[0], s_f, reverse=False, z_off=3 * n_heads)
    a, _ = scan(p, lbl[1], s_b, reverse=True, z_off=4 * n_heads, gate=(o_f, hgrn_norm_g[0]))
    y = _mm(a, hgrn_w_out[0].astype(BF), F32)
    x1, h = _resid(x0, y, mod(0, 0, 2), norm_g[0, 1], (norm_g[0, 2], mod(0, 0, 3), mod(0, 0, 4)))
    moe = functools.partial(_moe_layer, router_w=router_w, router_bias=router_bias, exp_w_gu=exp_w_gu,
                            exp_w_down=exp_w_down, shared_w_gu=shared_w_gu, shared_w_down=shared_w_down)
    x2, h = moe(h, x1, mod(0, 0, 5), norm_g[0, 3], (norm_g[1, 0], mod(1, 0, 0), mod(1, 0, 1)), 0)

    y = _pool(h, pool_w[0].astype(BF), pool_scale[0])
    x3, h = _resid(x2, y, mod(1, 0, 2), norm_g[1, 1], (norm_g[1, 2], mod(1, 0, 3), mod(1, 0, 4)))
    x4 = moe(h, x3, mod(1, 0, 5), norm_g[1, 3], None, 1)
    return x4[None]
```

```python
import functools

import jax
import jax.numpy as jnp
import numpy as np
from jax import lax
from jax.experimental import pallas as pl
from jax.experimental.pallas import tpu as pltpu

EPS = 1e-6
BF = jnp.bfloat16
F32 = jnp.float32
NEG_INF = float("-inf")

LANES = 128
HEAD_DIM = 128
SCAN_CHUNK = 128
SCAN_UNROLL = 4
SCAN_TILE = 1024
SCAN_HEADS = 2
GRID_W = 64
POOL_WINDOWS = (2, 4, 8, 16)
POOL_ROW_GROUP = 4
N_ROUTE_GROUPS = 8
TOPK_ROUTE_GROUPS = 4
TOP_K = 8
ROUTED_SCALE = 2.5
N_MOD = 6
EXPERT_TILE = 512
DOWN_CHUNK = 512
VMEM_LIMIT =56 * 1024 * 1024


def _cparams(*sem):
    return pltpu.CompilerParams(dimension_semantics=sem, vmem_limit_bytes=VMEM_LIMIT)


def _silu(v):
    return v * jax.nn.sigmoid(v)


def _rms(v):
    return v * lax.rsqrt(jnp.mean(v * v, axis=-1, keepdims=True) + EPS)


def _pick(n, want):
    t = min(n, want)
    while n % t:
        t //= 2
    assert t >= 1
    return t


def _mods_kernel(s_ref, w_ref, b_ref, o_ref):
    s = _silu(s_ref[...])
    o_ref[0] = jnp.dot(s.astype(BF), w_ref[0].astype(BF), preferred_element_type=F32) + b_ref[0]


def _mods(cvecs, w_mod, b_mod):
    depth, d, n = w_mod.shape
    tn = _pick(n, 512)
    return pl.pallas_call(
        _mods_kernel,
        grid=(depth, n // tn),
        in_specs=[
            pl.BlockSpec((8, d), lambda l, j: (0, 0)),
            pl.BlockSpec((1, d, tn), lambda l, j: (l, 0, j)),
            pl.BlockSpec((1, 1, tn), lambda l, j: (l, 0, j)),
        ],
        out_specs=pl.BlockSpec((1, 8, tn), lambda l, j: (l, 0, j)),
        out_shape=jax.ShapeDtypeStruct((depth, 8, n), F32),
        compiler_params=_cparams("parallel", "parallel"),
        name="mods",
    )(cvecs, w_mod, b_mod.reshape(depth, 1, n))


def _normmod_kernel(x_ref, g_ref, sh_ref, sc_ref, o_ref):
    y = _rms(x_ref[...]) * g_ref[...]
    o_ref[...] = (y * (1 + sc_ref[...]) + sh_ref[...]).astype(o_ref.dtype)


def _normmod(x, g, shift, scale):
    m, d = x.shape
    tm = _pick(m, 256)
    vec = pl.BlockSpec((1, d), lambda i: (0, 0))
    return pl.pallas_call(
        _normmod_kernel,
        grid=(m // tm,),
        in_specs=[pl.BlockSpec((tm, d), lambda i: (i, 0)), vec, vec, vec],
        out_specs=pl.BlockSpec((tm, d), lambda i: (i, 0)),
        out_shape=jax.ShapeDtypeStruct((m, d), BF),
        compiler_params=_cparams("parallel"),
        name="normmod",
    )(x, g.reshape(1, d), shift.reshape(1, d), scale.reshape(1, d))


def _resid_norm_kernel(x_ref, y_ref, gate_ref, gy_ref, gx_ref, sh_ref, sc_ref, xo_ref, ho_ref):
    yn = _rms(y_ref[...].astype(F32)) * gy_ref[...]
    xn = x_ref[...] + gate_ref[...] * yn
    xo_ref[...] = xn
    hn = _rms(xn) * gx_ref[...]
    ho_ref[...] = (hn * (1 + sc_ref[...]) + sh_ref[...]).astype(ho_ref.dtype)


def _resid(x, y, gate, gy, nxt):
    m, d = x.shape
    tm = _pick(m, 256)
    row = pl.BlockSpec((tm, d), lambda i: (i, 0))
    vec = pl.BlockSpec((1, d), lambda i: (0, 0))
    v = lambda a: a.reshape(1, d)
    gx, shift, scale = nxt
    return pl.pallas_call(
        _resid_norm_kernel,
        grid=(m // tm,),
        in_specs=[row, row, vec, vec, vec, vec, vec],
        out_specs=[row, row],
        out_shape=[jax.ShapeDtypeStruct((m, d), F32), jax.ShapeDtypeStruct((m, d), BF)],
        compiler_params=_cparams("parallel"),
        name="resid_norm",
    )(x, y, v(gate), v(gy), v(gx), v(shift), v(scale))


def _mm_kernel(a_ref, w_ref, o_ref):
    o_ref[...] = jnp.dot(a_ref[...], w_ref[...], preferred_element_type=F32).astype(o_ref.dtype)


def _mm(a, w, out_dtype):
    m, k = a.shape
    n = w.shape[1]
    tm, tn = _pick(m, 1024), _pick(n, 1024)
    return pl.pallas_call(
        _mm_kernel,
        grid=(m // tm, n // tn),
        in_specs=[pl.BlockSpec((tm, k), lambda i, j: (i, 0)), pl.BlockSpec((k, tn), lambda i, j: (0, j))],
        out_specs=pl.BlockSpec((tm, tn), lambda i, j: (i, j)),
        out_shape=jax.ShapeDtypeStruct((m, n), out_dtype),
        compiler_params=_cparams("parallel", "parallel"),
        name="mm",
    )(a, w)


def _mm_heads_kernel(a_ref, w_ref, o_ref):
    r = jnp.dot(a_ref[...], w_ref[...], preferred_element_type=F32)
    for hh in range(o_ref.shape[0]):
        o_ref[hh] = r[:, hh * LANES:(hh + 1) * LANES].astype(o_ref.dtype)


def _mm_heads(a, w):
    m, k = a.shape
    n = w.shape[1]
    tm, tn = _pick(m, 1024), _pick(n, 1024)
    return pl.pallas_call(
        _mm_heads_kernel,
        grid=(m // tm, n // tn),
        in_specs=[pl.BlockSpec((tm, k), lambda i, j: (i, 0)), pl.BlockSpec((k, tn), lambda i, j: (0, j))],
        out_specs=pl.BlockSpec((tn // LANES, tm, LANES), lambda i, j: (j, i, 0)),
        out_shape=jax.ShapeDtypeStruct((n // LANES, m, LANES), BF),
        compiler_params=_cparams("parallel", "parallel"),
        name="proj",
    )(a, w)


def _scan_tables(c, reverse):
    n_levels = c.bit_length() - 1
    assert c == 1 << n_levels
    t = np.arange(c)[:, None]
    u = np.arange(c)[None, :]
    dmats = [u <= t]
    masks = [u == t]
    roles = []
    for lv in range(n_levels):
        size = c >> lv
        start = (t // size) * size
        boundary = start + size // 2 - 1
        later = t > boundary
        if (size // 2) % 8:
            dmats.append(np.where(later, (u > boundary) & (u <= t), (u > t) & (u <= boundary)))
        masks.append(later & (u <= boundary.T) & (start == start.T))
        roles.append(np.broadcast_to(later, (c, HEAD_DIM)))
    dmat, mask, role = np.stack(dmats), np.stack(masks), np.stack(roles)
    if reverse:
        dmat, mask, role = dmat[:, ::-1, ::-1], mask[:, ::-1, ::-1], role[:, ::-1]
    return (jnp.asarray(dmat.reshape(-1, c), BF), jnp.asarray(mask, F32), jnp.asarray(role, F32))


def _scan_kernel(q_ref, v_ref, z_ref, lbl_ref, s0_ref, dmat_ref, mask_ref, roles_ref, *rest, layer, reverse, gated):
    if gated:
        oo_ref, g_ref, ng_ref, o_ref, sfin_ref, st_ref = rest
    else:
        o_ref, sfin_ref, st_ref = rest
    c = SCAN_CHUNK
    n_levels = roles_ref.shape[0]
    n_h, t_tile = q_ref.shape[0], q_ref.shape[1]
    span = min(SCAN_UNROLL * c, t_tile)
    n_spans = t_tile // span
    n_c = span // c
    i = pl.program_id(1)

    @pl.when(i == 0)
    def _():
        st_ref[...] = s0_ref[...]

    lbs = []
    for h in range(n_h):
        lg = lbl_ref[h]
        e = jnp.exp(lg - jnp.max(lg, axis=0, keepdims=True))
        p = e / jnp.sum(e, axis=0, keepdims=True)
        lbs.append(jnp.sum(p[:layer + 1], axis=0, keepdims=True))

    nt_dims = (((1,), (1,)), ((), ()))
    tn_dims = (((0,), (0,)), ((), ()))
    last = 0 if reverse else c - 1

    def level_rows(lv, qc, kc):
        size = c >> lv
        half = size // 2
        if half % 8:
            return jnp.where(roles_ref[lv] > 0, qc, kc)
        first, second = (qc, kc) if reverse else (kc, qc)
        parts = []
        for a in range(0, c, size):
            parts += [first[a:a + half], second[a + half:a + size]]
        return jnp.concatenate(parts, axis=0)

    def level_exponent(lv, b):
        size = c >> lv
        half = size // 2
        parts = []
        for a in range(0, c, size):
            if reverse:
                bnd = b[a + half:a + half + 1]
                parts += [b[a:a + half] - bnd, bnd - b[a + half:a + size]]
            else:
                bnd = b[a + half - 1:a + half]
                parts += [bnd - b[a:a + half], b[a + half:a + size] - bnd]
        return jnp.concatenate(parts, axis=0)

    def chunk(st, lf, qc, kc, vc):
        ex = jnp.dot(dmat_ref[...], lf.astype(BF), preferred_element_type=F32)
        b = ex[0:c]
        b_last = b[last:last + 1]
        qb, kb, vb = qc.astype(BF), kc.astype(BF), vc.astype(BF)
        att = lax.dot_general(qb, kb, nt_dims, preferred_element_type=F32) * mask_ref[0]
        n_small = 0
        for lv in range(n_levels):
            if (c >> lv) // 2 % 8:
                n_small += 1
                d = ex[n_small * c:(n_small + 1) * c]
            else:
                d = level_exponent(lv, b)
            xb = (level_rows(lv, qc, kc) * jnp.exp2(d)).astype(BF)
            gram = lax.dot_general(xb, xb, nt_dims, preferred_element_type=F32)
            att = att + gram * mask_ref[1 + lv]
        qd = (qc * jnp.exp2(b)).astype(BF)
        o = (jnp.dot(att.astype(BF), vb, preferred_element_type=F32)
             + lax.dot_general(qd, st.astype(BF), nt_dims, preferred_element_type=F32))
        kd = (kc * jnp.exp2(b_last - b)).astype(BF)
        st_new = st * jnp.exp2(b_last) + lax.dot_general(vb, kd, tn_dims, preferred_element_type=F32)
        return st_new, o

    def block(n, sts):
        j = (n_spans - 1 - n) if reverse else n
        rows = pl.ds(pl.multiple_of(j * span, span), span)
        sts = list(sts)
        pre = []
        for h in range(n_h):
            z = z_ref[h, rows, :].astype(F32)
            f = lbs[h] + (1 - lbs[h]) * jax.nn.sigmoid(z)
            pre.append((jnp.log2(f), _silu(q_ref[h, rows, :].astype(F32)), 1 - f, v_ref[h, rows, :].astype(F32)))
        outs = [[None] * n_c for _ in range(n_h)]
        for m in (range(n_c - 1, -1, -1) if reverse else range(n_c)):
            sl = slice(m * c, (m + 1) * c)
            for h in range(n_h):
                logf, q, kk, v = pre[h]
                sts[h], outs[h][m] = chunk(sts[h], logf[sl], q[sl], kk[sl], v[sl])
        for h in range(n_h):
            o = jnp.concatenate(outs[h], axis=0)
            if gated:
                o = _rms(o + oo_ref[h, rows, :].astype(F32)) * ng_ref[h]
                o_ref[rows, h * HEAD_DIM:(h + 1) * HEAD_DIM] = (
                    o * _silu(g_ref[h, rows, :].astype(F32))).astype(o_ref.dtype)
            else:
                o_ref[h, rows, :] = o.astype(o_ref.dtype)
        return tuple(sts)

    sts = lax.fori_loop(0, n_spans, block, tuple(st_ref[h] for h in range(n_h)))
    for h in range(n_h):
        st_ref[h] = sts[h]

    @pl.when(i == pl.num_programs(1) - 1)
    def _():
        sfin_ref[...] = st_ref[...]


def _scan(p, lb_logits, s0, *, n_heads, layer, reverse, z_off, gate=None):
    _, l, _ = p.shape
    t_tile = _pick(l, SCAN_TILE)
    assert t_tile % SCAN_CHUNK == 0
    nt = l // t_tile
    tok = (lambda i: nt - 1 - i) if reverse else (lambda i: i)
    hp = _pick(n_heads, SCAN_HEADS)
    blk = lambda off: pl.BlockSpec((hp, t_tile, HEAD_DIM), lambda h, i: (off // hp + h, tok(i), 0))
    assert n_heads % hp == 0 and z_off % hp == 0
    n_lb = lb_logits.shape[1]
    dmat, mask, roles = _scan_tables(SCAN_CHUNK, reverse)
    whole = lambda a: pl.BlockSpec(a.shape, lambda h, i: (0,) * a.ndim)
    in_specs = [
        blk(0), blk(2 * n_heads), blk(z_off),
        pl.BlockSpec((hp, n_lb, HEAD_DIM), lambda h, i: (h, 0, 0)),
        pl.BlockSpec((hp, HEAD_DIM, HEAD_DIM), lambda h, i: (h, 0, 0)),
        whole(dmat), whole(mask), whole(roles),
    ]
    args = [p, p, p, lb_logits, s0, dmat, mask, roles]
    if gate is None:
        o_spec = pl.BlockSpec((hp, t_tile, HEAD_DIM), lambda h, i: (h, tok(i), 0))
        o_shape = jax.ShapeDtypeStruct((n_heads, l, HEAD_DIM), BF)
    else:
        o_other, norm_g = gate
        in_specs += [blk(0), blk(n_heads), pl.BlockSpec((hp, 1, HEAD_DIM), lambda h, i: (h, 0, 0))]
        args += [o_other, p, norm_g.reshape(n_heads, 1, HEAD_DIM)]
        o_spec = pl.BlockSpec((t_tile, hp * HEAD_DIM), lambda h, i: (tok(i), h))
        o_shape = jax.ShapeDtypeStruct((l, n_heads * HEAD_DIM), BF)
    return pl.pallas_call(
        functools.partial(_scan_kernel, layer=layer, reverse=reverse, gated=gate is not None),
        grid=(n_heads // hp, nt),
        in_specs=in_specs,
        out_specs=[o_spec, pl.BlockSpec((hp, HEAD_DIM, HEAD_DIM), lambda h, i: (h, 0, 0))],
        out_shape=[o_shape, jax.ShapeDtypeStruct((n_heads, HEAD_DIM, HEAD_DIM), F32)],
        scratch_shapes=[pltpu.VMEM((hp, HEAD_DIM, HEAD_DIM), F32)],
        compiler_params=_cparams("parallel", "arbitrary"),
        name="scan_bwd" if reverse else "scan_fwd",
    )(*args)


def _pool_body(win, prev_ref, cur_ref, next_ref, w_ref, scale_ref, o_ref, y_ref, d_ref, n_rows):
    i = pl.program_id(0)
    tile_rows = cur_ref.shape[0] // GRID_W
    halo = y_ref.shape[0] - tile_rows
    top = halo // 2
    half = win // 2
    r0 = i * tile_rows

    rg = POOL_ROW_GROUP
    gt = rg * GRID_W
    shift = GRID_W.bit_length() - 1
    assert GRID_W == 1 << shift and tile_rows % rg == 0 and top % rg == 0
    t_io = lax.broadcasted_iota(jnp.int32, (gt, gt), 0)
    s_io = lax.broadcasted_iota(jnp.int32, (gt, gt), 1)
    same_row = lax.shift_right_logical(t_io, shift) == lax.shift_right_logical(s_io, shift)
    off = (s_io & (GRID_W - 1)) - (t_io & (GRID_W - 1))
    a01 = jnp.where(same_row & (off >= -half) & (off < win - half), 1.0, 0.0).astype(BF)

    def col_filter(src_ref, src_row, dst_row):
        xs = src_ref[pl.ds(pl.multiple_of(src_row * GRID_W, gt), gt), :]
        y = jnp.dot(a01, xs, preferred_element_type=F32)
        grow = jnp.full((1, y.shape[1]), r0 - top + dst_row, jnp.int32)
        ok = (grow >= 0) & (grow < n_rows)
        y_ref[pl.ds(dst_row, rg)] = jnp.where(ok, y, 0.0).reshape(rg, GRID_W, y.shape[1])

    n_above = -(-half // rg)
    n_below = -(-(half - 1) // rg)

    def above(j, c):
        col_filter(prev_ref, tile_rows - (n_above - j) * rg, top - (n_above - j) * rg)
        return c

    def inside(j, c):
        col_filter(cur_ref, j * rg, top + j * rg)
        return c

    def below(j, c):
        col_filter(next_ref, j * rg, top + tile_rows + j * rg)
        return c

    lax.fori_loop(0, n_above, above, 0)
    lax.fori_loop(0, tile_rows // rg, inside, 0)
    lax.fori_loop(0, n_below, below, 0)

    c_io = lax.broadcasted_iota(jnp.int32, (GRID_W, d_ref.shape[1]), 0)
    cnt_c = jnp.minimum(c_io + win - half, GRID_W) - jnp.maximum(c_io - half, 0)
    inv_c = 1.0 / cnt_c.astype(F32)

    def row_filter(r, c):
        grow = jnp.full((1, d_ref.shape[1]), r0 + r, jnp.int32)
        cnt = jnp.minimum(grow + win - half, n_rows) - jnp.maximum(grow - half, 0)
        z = y_ref[top + r - half]
        for j in range(1, win):
            z = z + y_ref[top + r - half + j]
        z = z * (inv_c * (1.0 / cnt.astype(F32)))
        sl = pl.ds(pl.multiple_of(r * GRID_W, GRID_W), GRID_W)
        d_ref[sl, :] = (z - cur_ref[sl, :].astype(F32)).astype(d_ref.dtype)
        return c

    lax.fori_loop(0, tile_rows, row_filter, 0)
    o_ref[...] = jnp.dot(d_ref[...], w_ref[0], preferred_element_type=F32) * scale_ref[...]


def _pool_kernel(prev_ref, cur_ref, next_ref, w_ref, scale_ref, o_ref, y_ref, d_ref, *, n_rows):
    g = pl.program_id(1)
    for gi, win in enumerate(POOL_WINDOWS):
        @pl.when(g == gi)
        def _(win=win):
            _pool_body(win, prev_ref, cur_ref, next_ref, w_ref, scale_ref, o_ref, y_ref, d_ref, n_rows)


def _pool(h, w_pool, scale):
    l, d = h.shape
    n_groups, gc, _ = w_pool.shape
    n_rows = l // GRID_W
    tile_rows = _pick(n_rows, 16)
    halo = max(POOL_WINDOWS)
    assert tile_rows >= halo // 2 and len(POOL_WINDOWS) == n_groups
    tt = tile_rows * GRID_W
    nt = l // tt
    return pl.pallas_call(
        functools.partial(_pool_kernel, n_rows=n_rows),
        grid=(nt, n_groups),
        in_specs=[
            pl.BlockSpec((tt, gc), lambda i, g: (jnp.maximum(i - 1, 0), g)),
            pl.BlockSpec((tt, gc), lambda i, g: (i, g)),
            pl.BlockSpec((tt, gc), lambda i, g: (jnp.minimum(i + 1, nt - 1), g)),
            pl.BlockSpec((1, gc, gc), lambda i, g: (g, 0, 0)),
            pl.BlockSpec((1, gc), lambda i, g: (0, g)),
        ],
        out_specs=pl.BlockSpec((tt, gc), lambda i, g: (i, g)),
        out_shape=jax.ShapeDtypeStruct((l, d), F32),
        scratch_shapes=[pltpu.VMEM((tile_rows + halo, GRID_W, gc), F32), pltpu.VMEM((tt, gc), BF)],
        compiler_params=_cparams("parallel", "parallel"),
        name="pool",
    )(h, h, h, w_pool, scale.reshape(1, d))


def _router_kernel(h_ref, wrt_ref, bias_ref, upper_ref, eidx_ref, wts_ref, rank_ref, cnt_ref, carry_ref):
    n_exp = wrt_ref.shape[0]
    tm = h_ref.shape[0]

    @pl.when(pl.program_id(0) == 0)
    def _():
        carry_ref[...] = jnp.zeros_like(carry_ref)

    epg = n_exp // N_ROUTE_GROUPS
    logits = lax.dot_general(wrt_ref[...], h_ref[...], (((1,), (1,)), ((), ())), preferred_element_type=F32)
    scores = jax.nn.sigmoid(logits)
    shape3 = (N_ROUTE_GROUPS, epg, tm)
    sc3 = scores.reshape(shape3)
    s3 = (scores + bias_ref[...]).reshape(shape3)
    e_in = lax.broadcasted_iota(jnp.int32, shape3, 1).astype(F32)
    g_io = lax.broadcasted_iota(jnp.int32, shape3, 0).astype(F32)

    m1 = jnp.max(s3, axis=1, keepdims=True)
    i1 = jnp.min(jnp.where(s3 == m1, e_in, epg), axis=1, keepdims=True)
    m2 = jnp.max(jnp.where(e_in == i1, NEG_INF, s3), axis=1, keepdims=True)
    gs = jnp.broadcast_to(m1 + m2, shape3)

    keep = jnp.zeros(shape3, F32)
    cur = gs
    for _ in range(TOPK_ROUTE_GROUPS):
        m = jnp.max(cur, axis=0, keepdims=True)
        idx = jnp.min(jnp.where(cur == m, g_io, N_ROUTE_GROUPS), axis=0, keepdims=True)
        hit = g_io == idx
        keep = jnp.where(hit, 1.0, keep)
        cur = jnp.where(hit, NEG_INF, cur)

    e_io = g_io * epg + e_in
    chosen = jnp.zeros(shape3, F32)
    cur = jnp.where(keep > 0, s3, NEG_INF)
    picks = []
    for _ in range(TOP_K):
        m = jnp.max(jnp.max(cur, axis=0, keepdims=True), axis=1, keepdims=True)
        idx = jnp.where(cur == m, e_io, n_exp)
        idx = jnp.min(jnp.min(idx, axis=0, keepdims=True), axis=1, keepdims=True)
        hit = e_io == idx
        chosen = jnp.where(hit, 1.0, chosen)
        cur = jnp.where(hit, NEG_INF, cur)
        picks.append(idx)

    w = jnp.where(chosen > 0, sc3, 0.0)
    den = jnp.sum(jnp.sum(w, axis=0, keepdims=True), axis=1, keepdims=True)
    w = w / den * ROUTED_SCALE

    ch = chosen.reshape(n_exp, tm).astype(BF)
    carry = carry_ref[...]
    rank = (jnp.dot(ch, upper_ref[...], preferred_element_type=F32)
            + jnp.concatenate([carry] * (tm // LANES), axis=1)).reshape(shape3)
    carry = carry + jnp.dot(ch, jnp.ones((tm, LANES), BF), preferred_element_type=F32)
    carry_ref[...] = carry
    cnt_ref[...] = carry

    def per_pick(vals, idx):
        sel = jnp.where(e_io == idx, vals, 0.0)
        return jnp.sum(jnp.sum(sel, axis=0, keepdims=True), axis=1, keepdims=True).reshape(1, tm)

    eidx_ref[...] = jnp.concatenate([idx.reshape(1, tm) for idx in picks], axis=0).astype(jnp.int32)
    wts_ref[...] = jnp.concatenate([per_pick(w, idx) for idx in picks], axis=0)
    rank_ref[...] = jnp.concatenate([per_pick(rank, idx) for idx in picks], axis=0).astype(jnp.int32)


def _router(h, w_r, b_r):
    t, d = h.shape
    n_exp = w_r.shape[1]
    tm = _pick(t, 512)
    assert tm % LANES == 0
    upper = jnp.asarray(np.triu(np.ones((tm, tm), np.float32), 1), BF)
    pick = pl.BlockSpec((TOP_K, tm), lambda i: (0, i))
    eidx, wts, rank, cnt = pl.pallas_call(
        _router_kernel,
        grid=(t // tm,),
        in_specs=[
            pl.BlockSpec((tm, d), lambda i: (i, 0)),
            pl.BlockSpec((n_exp, d), lambda i: (0, 0)),
            pl.BlockSpec((n_exp, 1), lambda i: (0, 0)),
            pl.BlockSpec((tm, tm), lambda i: (0, 0)),
        ],
        out_specs=[pick, pick, pick, pl.BlockSpec((n_exp, LANES), lambda i: (0, 0))],
        out_shape=[
            jax.ShapeDtypeStruct((TOP_K, t), jnp.int32),
            jax.ShapeDtypeStruct((TOP_K, t), F32),
            jax.ShapeDtypeStruct((TOP_K, t), jnp.int32),
            jax.ShapeDtypeStruct((n_exp, LANES), F32),
        ],
        scratch_shapes=[pltpu.VMEM((n_exp, LANES), F32)],
        compiler_params=_cparams("arbitrary"),
        name="router",
    )(h, w_r.T.astype(BF), b_r.astype(F32).reshape(n_exp, 1), upper)
    return eidx, wts, rank, cnt[:, 0].astype(jnp.int32)


HI_MASK = 0xFFFF0000


def _pack_halves(lo, hi):
    bits = lambda a: pltpu.bitcast(a.astype(BF).astype(F32), jnp.uint32)
    return lax.shift_right_logical(bits(lo), jnp.uint32(16)) | (bits(hi) & jnp.uint32(HI_MASK))


def _pack_rows(v):
    half = v.shape[1] // 2
    return _pack_halves(v[:, :half], v[:, half:])


def _unpack_rows(w):
    lo = pltpu.bitcast(lax.shift_left(w, jnp.uint32(16)), F32)
    hi = pltpu.bitcast(w & jnp.uint32(HI_MASK), F32)
    return lo, hi


def _dispatch_kernel(pos_ref, h_ref, xs_hbm, hp_ref, sem):
    i = pl.program_id(0)
    n_k, tm = pos_ref.shape
    slot = i % 2
    hp_ref[slot] = _pack_rows(h_ref[...].astype(F32))

    def issue(t, c):
        for k in range(n_k):
            pltpu.make_async_copy(hp_ref.at[slot, pl.ds(t, 1)], xs_hbm.at[pl.ds(pos_ref[k, t], 1)],
                                  sem.at[slot]).start(priority=k % 2)
        return c

    lax.fori_loop(0, tm, issue, 0, unroll=8)

    def drain(s):
        def body(t, c):
            for k in range(n_k):
                pltpu.make_async_copy(hp_ref.at[s, pl.ds(0, 1)], xs_hbm.at[pl.ds(0, 1)], sem.at[s]).wait()
            return c
        lax.fori_loop(0, tm, body, 0, unroll=8)

    @pl.when(i > 0)
    def _():
        drain(1 - slot)

    @pl.when(i == pl.num_programs(0) - 1)
    def _():
        drain(slot)


def _dispatch(h, pos):
    t, d = h.shape
    n_k = pos.shape[0]
    tm = _pick(t, 256)
    return pl.pallas_call(
        _dispatch_kernel,
        grid=(t // tm,),
        in_specs=[
            pl.BlockSpec((n_k, tm), lambda i: (0, i), memory_space=pltpu.SMEM),
            pl.BlockSpec((tm, d), lambda i: (i, 0)),
        ],
        out_specs=pl.BlockSpec(memory_space=pl.ANY),
        out_shape=jax.ShapeDtypeStruct((n_k * t, d // 2), jnp.uint32),
        scratch_shapes=[pltpu.VMEM((2, tm, d // 2), jnp.uint32), pltpu.SemaphoreType.DMA((2,))],
        compiler_params=_cparams("arbitrary"),
        name="dispatch",
    )(pos, h)


def _experts_kernel(tile_ref, exp_ref, lo_ref, hi_ref, xs_ref, wgu_ref, wd_ref, ys_ref,
                    acc_ref, wgu_bf_ref, wd_bf_ref):
    w = pl.program_id(0)
    prev = jnp.maximum(w - 1, 0)
    tm = xs_ref.shape[0]
    half = xs_ref.shape[1]
    d_exp = wd_ref.shape[0]

    @pl.when(jnp.logical_or(w == 0, exp_ref[w] != exp_ref[prev]))
    def _():
        wgu_bf_ref[...] = wgu_ref[...].astype(BF)
        wd_bf_ref[...] = wd_ref[...].astype(BF)

    lo, hi = lo_ref[w], hi_ref[w]
    live = hi > lo
    first = jnp.logical_or(w == 0, tile_ref[w] != tile_ref[prev])

    def run(rows, accumulate):
        n = rows.stop - rows.start
        x_lo, x_hi = _unpack_rows(xs_ref[rows, :])
        gu = (jnp.dot(x_lo.astype(BF), wgu_bf_ref[:half, :], preferred_element_type=F32)
              + jnp.dot(x_hi.astype(BF), wgu_bf_ref[half:, :], preferred_element_type=F32))
        row = lax.broadcasted_iota(jnp.int32, (n, d_exp), 0) + rows.start
        mine = (row >= lo) & (row < hi)
        act = jnp.where(mine, _silu(gu[:, :d_exp]) * gu[:, d_exp:], 0.0).astype(BF)
        chunk = min(DOWN_CHUNK, half)
        assert half % chunk == 0
        for j in range(half // chunk):
            c_lo = slice(j * chunk, (j + 1) * chunk)
            c_hi = slice(half + j * chunk, half + (j + 1) * chunk)
            y_lo = jnp.dot(act, wd_bf_ref[:, c_lo], preferred_element_type=F32)
            y_hi = jnp.dot(act, wd_bf_ref[:, c_hi], preferred_element_type=F32)
            if accumulate:
                y_lo = y_lo + acc_ref[rows, c_lo]
                y_hi = y_hi + acc_ref[rows, c_hi]
            acc_ref[rows, c_lo] = y_lo
            acc_ref[rows, c_hi] = y_hi
            ys_ref[rows, c_lo] = _pack_halves(y_lo, y_hi)

    rb = tm // 2
    top, bottom = slice(0, rb), slice(rb, tm)
    in_top, in_bottom = lo < rb, hi > rb
    cases = [
        (jnp.logical_and(in_top, in_bottom), slice(0, tm), None),
        (jnp.logical_and(in_top, jnp.logical_not(in_bottom)), top, bottom),
        (jnp.logical_and(jnp.logical_not(in_top), in_bottom), bottom, top),
    ]
    for cond, rows, skipped in cases:
        @pl.when(jnp.logical_and(live, jnp.logical_and(cond, first)))
        def _(rows=rows, skipped=skipped):
            if skipped is not None:
                acc_ref[skipped, :] = jnp.zeros((rb, acc_ref.shape[1]), F32)
            run(rows, False)

        @pl.when(jnp.logical_and(live, jnp.logical_and(cond, jnp.logical_not(first))))
        def _(rows=rows):
            run(rows, True)


def _experts(xs, items, w_gu, w_down, layer, tm):
    p, dw = xs.shape
    _, _, d, f2 = w_gu.shape
    n_items = items[0].shape[0]
    grid_spec = pltpu.PrefetchScalarGridSpec(
        num_scalar_prefetch=4,
        grid=(n_items,),
        in_specs=[
            pl.BlockSpec((tm, dw), lambda w, tile, exp, lo, hi: (tile[w], 0)),
            pl.BlockSpec((None, None, d, f2), lambda w, tile, exp, lo, hi: (layer, exp[w], 0, 0)),
            pl.BlockSpec((None, None, f2 // 2, d), lambda w, tile, exp, lo, hi: (layer, exp[w], 0, 0)),
        ],
        out_specs=pl.BlockSpec((tm, dw), lambda w, tile, exp, lo, hi: (tile[w], 0)),
        scratch_shapes=[pltpu.VMEM((tm, d), F32), pltpu.VMEM((d, f2), BF), pltpu.VMEM((f2 // 2, d), BF)],
    )
    return pl.pallas_call(
        _experts_kernel,
        grid_spec=grid_spec,
        out_shape=jax.ShapeDtypeStruct((p, dw), jnp.uint32),
        compiler_params=_cparams("arbitrary"),
        name="experts",
    )(*items, xs, w_gu, w_down)


def _expert_items(cnt, tm, n_tiles):
    n_exp = cnt.shape[0]
    n_items = n_tiles + n_exp - 1
    end = jnp.cumsum(cnt)
    start = end - cnt
    first_tile = start // tm
    n_e = jnp.where(cnt > 0, (end - 1) // tm - first_tile + 1, 0)
    item_end = jnp.cumsum(n_e)
    w = jnp.arange(n_items, dtype=jnp.int32)
    e = jnp.minimum(jnp.searchsorted(item_end, w, side="right"), n_exp - 1).astype(jnp.int32)
    valid = w < item_end[-1]
    tile = jnp.where(valid, first_tile[e] + (w - (item_end[e] - n_e[e])), n_tiles - 1).astype(jnp.int32)
    lo = jnp.where(valid, jnp.maximum(start[e], tile * tm) - tile * tm, 0).astype(jnp.int32)
    hi = jnp.where(valid, jnp.minimum(end[e], (tile + 1) * tm) - tile * tm, 0).astype(jnp.int32)
    e = jnp.where(valid, e, e[jnp.maximum(item_end[-1] - 1, 0)])
    return tile, e, lo, hi


def _shared_kernel(h_ref, wgu_ref, wd_ref, o_ref):
    d_exp = wd_ref.shape[0]
    gu = jnp.dot(h_ref[...], wgu_ref[...], preferred_element_type=F32)
    act = _silu(gu[:, :d_exp]) * gu[:, d_exp:]
    o_ref[...] = jnp.dot(act.astype(BF), wd_ref[...], preferred_element_type=F32)


def _shared(h, w_gu, w_down):
    t, d = h.shape
    f2 = w_gu.shape[1]
    tm = _pick(t, 512)
    return pl.pallas_call(
        _shared_kernel,
        grid=(t // tm,),
        in_specs=[
            pl.BlockSpec((tm, d), lambda i: (i, 0)),
            pl.BlockSpec((d, f2), lambda i: (0, 0)),
            pl.BlockSpec((f2 // 2, d), lambda i: (0, 0)),
        ],
        out_specs=pl.BlockSpec((tm, d), lambda i: (i, 0)),
        out_shape=jax.ShapeDtypeStruct((t, d), F32),
        compiler_params=_cparams("parallel"),
        name="shared",
    )(h, w_gu, w_down)


def _combine_kernel(pos_ref, sh_ref, wts_ref, x_ref, gate_ref, gy_ref, *rest, with_next):
    if with_next:
        gx_ref, shift_ref, scale_ref, ys_hbm, xo_ref, ho_ref, buf_ref, sem = rest
    else:
        ys_hbm, xo_ref, buf_ref, sem = rest
    i = pl.program_id(0)
    n = pl.num_programs(0)
    n_k, tm = pos_ref.shape[1], pos_ref.shape[2]
    half = buf_ref.shape[3]

    def row_copy(j, t, k, slot):
        return pltpu.make_async_copy(ys_hbm.at[pl.ds(pos_ref[j, k, t], 1)], buf_ref.at[slot, k, pl.ds(t, 1)],
                                     sem.at[slot])

    def gather(j, slot):
        def issue(t, c):
            for k in range(n_k):
                row_copy(j, t, k, slot).start(priority=k % 2)
            return c
        lax.fori_loop(0, tm, issue, 0, unroll=8)

    @pl.when(i == 0)
    def _():
        gather(0, 0)

    slot = i % 2
    nxt_slot = 1 - slot

    def drain(s):
        def body(t, c):
            for k in range(n_k):
                pltpu.make_async_copy(ys_hbm.at[pl.ds(0, 1)], buf_ref.at[s, 0, pl.ds(0, 1)], sem.at[s]).wait()
            return c
        lax.fori_loop(0, tm, body, 0, unroll=8)

    drain(slot)

    acc_lo = sh_ref[:, :half]
    acc_hi = sh_ref[:, half:]
    for k in range(n_k):
        y_lo, y_hi = _unpack_rows(buf_ref[slot, k])
        wk = wts_ref[:, k:k + 1]
        acc_lo = acc_lo + wk * y_lo
        acc_hi = acc_hi + wk * y_hi
        for t in range(tm):
            row_copy(1, t, k, nxt_slot).start(priority=t % 2)

    @pl.when(i == n - 1)
    def _():
        drain(nxt_slot)

    f = jnp.concatenate([acc_lo, acc_hi], axis=1)
    xn = x_ref[...] + gate_ref[...] * (_rms(f) * gy_ref[...])
    xo_ref[...] = xn
    if with_next:
        hn = _rms(xn) * gx_ref[...]
        ho_ref[...] = (hn * (1 + scale_ref[...]) + shift_ref[...]).astype(ho_ref.dtype)


def _combine(ys, pos, wts, sh, x, gate, gy, nxt=None):
    t, d = sh.shape
    n_k = pos.shape[0]
    tm = _pick(t, 64)
    nt = t // tm
    pos3 = pos.reshape(n_k, nt, tm).transpose(1, 0, 2)
    pos_pair = jnp.stack([pos3, jnp.concatenate([pos3[1:], pos3[-1:]], axis=0)], axis=1)
    row = pl.BlockSpec((tm, d), lambda i: (i, 0))
    vec = pl.BlockSpec((1, d), lambda i: (0, 0))
    v = lambda a: a.reshape(1, d)
    vecs = [v(gate), v(gy)] + ([v(a) for a in nxt] if nxt is not None else [])
    out_specs = [row, row] if nxt is not None else row
    out_shape = [jax.ShapeDtypeStruct((t, d), F32)] + ([jax.ShapeDtypeStruct((t, d), BF)] if nxt is not None else [])
    return pl.pallas_call(
        functools.partial(_combine_kernel, with_next=nxt is not None),
        grid=(nt,),
        in_specs=[
            pl.BlockSpec((None, 2, n_k, tm), lambda i: (i, 0, 0, 0), memory_space=pltpu.SMEM),
            row,
            pl.BlockSpec((tm, n_k), lambda i: (i, 0)),
            row,
        ] + [vec] * len(vecs) + [pl.BlockSpec(memory_space=pl.ANY)],
        out_specs=out_specs,
        out_shape=out_shape if nxt is not None else out_shape[0],
        scratch_shapes=[pltpu.VMEM((2, n_k, tm, d // 2), jnp.uint32), pltpu.SemaphoreType.DMA((2,))],
        compiler_params=_cparams("arbitrary"),
        name="combine",
    )(pos_pair, sh, wts, x, *vecs, ys)


def _moe_layer(h, x, gate, gy, nxt, layer, router_w, router_bias, exp_w_gu, exp_w_down, shared_w_gu, shared_w_down):
    t = h.shape[0]
    s_gu, s_down = shared_w_gu[layer], shared_w_down[layer]
    eidx, wts, rank, cnt = _router(h, router_w[layer], router_bias[layer])
    start = jnp.cumsum(cnt) - cnt
    n_exp = cnt.shape[0]
    is_e = eidx[None] == jnp.arange(n_exp, dtype=jnp.int32)[:, None, None]
    pos = jnp.sum(jnp.where(is_e, start[:, None, None], 0), axis=0) + rank
    tm = _pick(TOP_K * t, EXPERT_TILE)
    items = _expert_items(cnt, tm, TOP_K * t // tm)
    xs = _dispatch(h, pos)
    ys = _experts(xs, items, exp_w_gu, exp_w_down, layer, tm)
    sh = _shared(h, s_gu.astype(BF), s_down.astype(BF))
    return _combine(ys, pos, wts.T, sh, x, gate, gy, nxt)


def kernel(x, c, ctx, c_ctx, w_mod, b_mod, norm_g, hgrn_w_in, hgrn_lb_logits, hgrn_norm_g, hgrn_w_out,
           pool_w, pool_scale, router_w, router_bias, exp_w_gu, exp_w_down, shared_w_gu, shared_w_down):
    batch, seq, d = x.shape
    assert batch == 1 and c.shape[0] == 1
    depth = w_mod.shape[0]
    assert depth == 2 and w_mod.shape[2] == N_MOD * d
    n_heads = hgrn_w_in.shape[2] // 5 // HEAD_DIM
    x0 = x[0]

    cvecs = jnp.concatenate([c, c_ctx[None], jnp.zeros((6, d), F32)], axis=0)
    mods = _mods(cvecs, w_mod, b_mod)
    mod = lambda layer, who, j: mods[layer, who, j * d:(j + 1) * d]

    w_in = hgrn_w_in[0].astype(BF)
    hx = _normmod(x0, norm_g[0, 0], mod(0, 0, 0), mod(0, 0, 1))
    hc = _normmod(ctx[0], norm_g[0, 0], mod(0, 1, 0), mod(0, 1, 1))
    p = _mm_heads(hx, w_in)
    pc = _mm_heads(hc, w_in)
    n_lb = hgrn_lb_logits.shape[1]
    lbl = hgrn_lb_logits.astype(F32).reshape(2, n_lb, n_heads, HEAD_DIM).transpose(0, 2, 1, 3)
    zero_state = jnp.zeros((n_heads, HEAD_DIM, HEAD_DIM), F32)
    scan = functools.partial(_scan, n_heads=n_heads, layer=0)
    _, s_f = scan(pc, lbl[0], zero_state, reverse=False, z_off=3 * n_heads)
    _, s_b = scan(pc, lbl[1], zero_state, reverse=True, z_off=4 * n_heads)
    o_f, _ = scan(p, lbl[0], s_f, reverse=False, z_off=3 * n_heads)
    a, _ = scan(p, lbl[1], s_b, reverse=True, z_off=4 * n_heads, gate=(o_f, hgrn_norm_g[0]))
    y = _mm(a, hgrn_w_out[0].astype(BF), F32)
    x1, h = _resid(x0, y, mod(0, 0, 2), norm_g[0, 1], (norm_g[0, 2], mod(0, 0, 3), mod(0, 0, 4)))
    moe = functools.partial(_moe_layer, router_w=router_w, router_bias=router_bias, exp_w_gu=exp_w_gu,
                            exp_w_down=exp_w_down, shared_w_gu=shared_w_gu, shared_w_down=shared_w_down)
    x2, h = moe(h, x1, mod(0, 0, 5), norm_g[0, 3], (norm_g[1, 0], mod(1, 0, 0), mod(1, 0, 1)), 0)

    y = _pool(h, pool_w[0].astype(BF), pool_scale[0])
    x3, h = _resid(x2, y, mod(1, 0, 2), norm_g[1, 1], (norm_g[1, 2], mod(1, 0, 3), mod(1, 0, 4)))
    x4 = moe(h, x3, mod(1, 0, 5), norm_g[1, 3], None, 1)
    return x4[None]
```

```python
import functools

import jax
import jax.numpy as jnp
import numpy as np
from jax import lax
from jax.experimental import pallas as pl
from jax.experimental.pallas import tpu as pltpu

EPS = 1e-6
BF = jnp.bfloat16
F32 = jnp.float32
NEG_INF = float("-inf")

LANES = 128
HEAD_DIM = 128
SCAN_CHUNK = 128
SCAN_UNROLL = 4
SCAN_TILE = 1024
SCAN_HEADS = 2
GRID_W = 64
POOL_WINDOWS = (2, 4, 8, 16)
POOL_ROW_GROUP = 4
N_ROUTE_GROUPS = 8
TOPK_ROUTE_GROUPS = 4
TOP_K = 8
ROUTED_SCALE = 2.5
N_MOD = 6
EXPERT_TILE = 512
DOWN_CHUNK = 512
VMEM_LIMIT =56 * 1024 * 1024


def _cparams(*sem):
    return pltpu.CompilerParams(dimension_semantics=sem, vmem_limit_bytes=VMEM_LIMIT)


def _silu(v):
    return v * jax.nn.sigmoid(v)


def _rms(v):
    return v * lax.rsqrt(jnp.mean(v * v, axis=-1, keepdims=True) + EPS)


def _pick(n, want):
    t = min(n, want)
    while n % t:
        t //= 2
    assert t >= 1
    return t


def _mods_kernel(s_ref, w_ref, b_ref, o_ref):
    s = _silu(s_ref[...])
    o_ref[0] = jnp.dot(s.astype(BF), w_ref[0].astype(BF), preferred_element_type=F32) + b_ref[0]


def _mods(cvecs, w_mod, b_mod):
    depth, d, n = w_mod.shape
    tn = _pick(n, 512)
    return pl.pallas_call(
        _mods_kernel,
        grid=(depth, n // tn),
        in_specs=[
            pl.BlockSpec((8, d), lambda l, j: (0, 0)),
            pl.BlockSpec((1, d, tn), lambda l, j: (l, 0, j)),
            pl.BlockSpec((1, 1, tn), lambda l, j: (l, 0, j)),
        ],
        out_specs=pl.BlockSpec((1, 8, tn), lambda l, j: (l, 0, j)),
        out_shape=jax.ShapeDtypeStruct((depth, 8, n), F32),
        compiler_params=_cparams("parallel", "parallel"),
        name="mods",
    )(cvecs, w_mod, b_mod.reshape(depth, 1, n))


def _normmod_kernel(x_ref, g_ref, sh_ref, sc_ref, o_ref):
    y = _rms(x_ref[...]) * g_ref[...]
    o_ref[...] = (y * (1 + sc_ref[...]) + sh_ref[...]).astype(o_ref.dtype)


def _normmod(x, g, shift, scale):
    m, d = x.shape
    tm = _pick(m, 256)
    vec = pl.BlockSpec((1, d), lambda i: (0, 0))
    return pl.pallas_call(
        _normmod_kernel,
        grid=(m // tm,),
        in_specs=[pl.BlockSpec((tm, d), lambda i: (i, 0)), vec, vec, vec],
        out_specs=pl.BlockSpec((tm, d), lambda i: (i, 0)),
        out_shape=jax.ShapeDtypeStruct((m, d), BF),
        compiler_params=_cparams("parallel"),
        name="normmod",
    )(x, g.reshape(1, d), shift.reshape(1, d), scale.reshape(1, d))


def _resid_norm_kernel(x_ref, y_ref, gate_ref, gy_ref, gx_ref, sh_ref, sc_ref, xo_ref, ho_ref):
    yn = _rms(y_ref[...].astype(F32)) * gy_ref[...]
    xn = x_ref[...] + gate_ref[...] * yn
    xo_ref[...] = xn
    hn = _rms(xn) * gx_ref[...]
    ho_ref[...] = (hn * (1 + sc_ref[...]) + sh_ref[...]).astype(ho_ref.dtype)


def _resid(x, y, gate, gy, nxt):
    m, d = x.shape
    tm = _pick(m, 256)
    row = pl.BlockSpec((tm, d), lambda i: (i, 0))
    vec = pl.BlockSpec((1, d), lambda i: (0, 0))
    v = lambda a: a.reshape(1, d)
    gx, shift, scale = nxt
    return pl.pallas_call(
        _resid_norm_kernel,
        grid=(m // tm,),
        in_specs=[row, row, vec, vec, vec, vec, vec],
        out_specs=[row, row],
        out_shape=[jax.ShapeDtypeStruct((m, d), F32), jax.ShapeDtypeStruct((m, d), BF)],
        compiler_params=_cparams("parallel"),
        name="resid_norm",
    )(x, y, v(gate), v(gy), v(gx), v(shift), v(scale))


def _mm_kernel(a_ref, w_ref, o_ref):
    o_ref[...] = jnp.dot(a_ref[...], w_ref[...], preferred_element_type=F32).astype(o_ref.dtype)


def _mm(a, w, out_dtype):
    m, k = a.shape
    n = w.shape[1]
    tm, tn = _pick(m, 1024), _pick(n, 1024)
    return pl.pallas_call(
        _mm_kernel,
        grid=(m // tm, n // tn),
        in_specs=[pl.BlockSpec((tm, k), lambda i, j: (i, 0)), pl.BlockSpec((k, tn), lambda i, j: (0, j))],
        out_specs=pl.BlockSpec((tm, tn), lambda i, j: (i, j)),
        out_shape=jax.ShapeDtypeStruct((m, n), out_dtype),
        compiler_params=_cparams("parallel", "parallel"),
        name="mm",
    )(a, w)


def _mm_heads_kernel(a_ref, w_ref, o_ref):
    r = jnp.dot(a_ref[...], w_ref[...], preferred_element_type=F32)
    for hh in range(o_ref.shape[0]):
        o_ref[hh] = r[:, hh * LANES:(hh + 1) * LANES].astype(o_ref.dtype)


def _mm_heads(a, w):
    m, k = a.shape
    n = w.shape[1]
    tm, tn = _pick(m, 1024), _pick(n, 1024)
    return pl.pallas_call(
        _mm_heads_kernel,
        grid=(m // tm, n // tn),
        in_specs=[pl.BlockSpec((tm, k), lambda i, j: (i, 0)), pl.BlockSpec((k, tn), lambda i, j: (0, j))],
        out_specs=pl.BlockSpec((tn // LANES, tm, LANES), lambda i, j: (j, i, 0)),
        out_shape=jax.ShapeDtypeStruct((n // LANES, m, LANES), BF),
        compiler_params=_cparams("parallel", "parallel"),
        name="proj",
    )(a, w)


def _scan_tables(c, reverse):
    n_levels = c.bit_length() - 1
    assert c == 1 << n_levels
    t = np.arange(c)[:, None]
    u = np.arange(c)[None, :]
    dmats = [u <= t]
    masks = [u == t]
    roles = []
    for lv in range(n_levels):
        size = c >> lv
        start = (t // size) * size
        boundary = start + size // 2 - 1
        later = t > boundary
        if (size // 2) % 8:
            dmats.append(np.where(later, (u > boundary) & (u <= t), (u > t) & (u <= boundary)))
        masks.append(later & (u <= boundary.T) & (start == start.T))
        roles.append(np.broadcast_to(later, (c, HEAD_DIM)))
    dmat, mask, role = np.stack(dmats), np.stack(masks), np.stack(roles)
    if reverse:
        dmat, mask, role = dmat[:, ::-1, ::-1], mask[:, ::-1, ::-1], role[:, ::-1]
    return (jnp.asarray(dmat.reshape(-1, c), BF), jnp.asarray(mask, F32), jnp.asarray(role, F32))


def _scan_kernel(q_ref, v_ref, z_ref, lbl_ref, s0_ref, dmat_ref, mask_ref, roles_ref, *rest, layer, reverse, gated):
    if gated:
        oo_ref, g_ref, ng_ref, o_ref, sfin_ref, st_ref = rest
    else:
        o_ref, sfin_ref, st_ref = rest
    c = SCAN_CHUNK
    n_levels = roles_ref.shape[0]
    n_h, t_tile = q_ref.shape[0], q_ref.shape[1]
    span = min(SCAN_UNROLL * c, t_tile)
    n_spans = t_tile // span
    n_c = span // c
    i = pl.program_id(1)

    @pl.when(i == 0)
    def _():
        st_ref[...] = s0_ref[...]

    lbs = []
    for h in range(n_h):
        lg = lbl_ref[h]
        e = jnp.exp(lg - jnp.max(lg, axis=0, keepdims=True))
        p = e / jnp.sum(e, axis=0, keepdims=True)
        lbs.append(jnp.sum(p[:layer + 1], axis=0, keepdims=True))

    nt_dims = (((1,), (1,)), ((), ()))
    tn_dims = (((0,), (0,)), ((), ()))
    last = 0 if reverse else c - 1

    def level_rows(lv, qc, kc):
        size = c >> lv
        half = size // 2
        if half % 8:
            return jnp.where(roles_ref[lv] > 0, qc, kc)
        first, second = (qc, kc) if reverse else (kc, qc)
        parts = []
        for a in range(0, c, size):
            parts += [first[a:a + half], second[a + half:a + size]]
        return jnp.concatenate(parts, axis=0)

    def level_exponent(lv, b):
        size = c >> lv
        half = size // 2
        parts = []
        for a in range(0, c, size):
            if reverse:
                bnd = b[a + half:a + half + 1]
                parts += [b[a:a + half] - bnd, bnd - b[a + half:a + size]]
            else:
                bnd = b[a + half - 1:a + half]
                parts += [bnd - b[a:a + half], b[a + half:a + size] - bnd]
        return jnp.concatenate(parts, axis=0)

    def chunk(st, lf, qc, kc, vc):
        ex = jnp.dot(dmat_ref[...], lf.astype(BF), preferred_element_type=F32)
        b = ex[0:c]
        b_last = b[last:last + 1]
        qb, kb, vb = qc.astype(BF), kc.astype(BF), vc.astype(BF)
        att = lax.dot_general(qb, kb, nt_dims, preferred_element_type=F32) * mask_ref[0]
        n_small = 0
        for lv in range(n_levels):
            if (c >> lv) // 2 % 8:
                n_small += 1
                d = ex[n_small * c:(n_small + 1) * c]
            else:
                d = level_exponent(lv, b)
            xb = (level_rows(lv, qc, kc) * jnp.exp2(d)).astype(BF)
            gram = lax.dot_general(xb, xb, nt_dims, preferred_element_type=F32)
            att = att + gram * mask_ref[1 + lv]
        qd = (qc * jnp.exp2(b)).astype(BF)
        o = (jnp.dot(att.astype(BF), vb, preferred_element_type=F32)
             + lax.dot_general(qd, st.astype(BF), nt_dims, preferred_element_type=F32))
        kd = (kc * jnp.exp2(b_last - b)).astype(BF)
        st_new = st * jnp.exp2(b_last) + lax.dot_general(vb, kd, tn_dims, preferred_element_type=F32)
        return st_new, o

    def block(n, sts):
        j = (n_spans - 1 - n) if reverse else n
        rows = pl.ds(pl.multiple_of(j * span, span), span)
        sts = list(sts)
        pre = []
        for h in range(n_h):
            z = z_ref[h, rows, :].astype(F32)
            f = lbs[h] + (1 - lbs[h]) * jax.nn.sigmoid(z)
            pre.append((jnp.log2(f), _silu(q_ref[h, rows, :].astype(F32)), 1 - f, v_ref[h, rows, :].astype(F32)))
        outs = [[None] * n_c for _ in range(n_h)]
        for m in (range(n_c - 1, -1, -1) if reverse else range(n_c)):
            sl = slice(m * c, (m + 1) * c)
            for h in range(n_h):
                logf, q, kk, v = pre[h]
                sts[h], outs[h][m] = chunk(sts[h], logf[sl], q[sl], kk[sl], v[sl])
        for h in range(n_h):
            o = jnp.concatenate(outs[h], axis=0)
            if gated:
                o = _rms(o + oo_ref[h, rows, :].astype(F32)) * ng_ref[h]
                o_ref[rows, h * HEAD_DIM:(h + 1) * HEAD_DIM] = (
                    o * _silu(g_ref[h, rows, :].astype(F32))).astype(o_ref.dtype)
            else:
                o_ref[h, rows, :] = o.astype(o_ref.dtype)
        return tuple(sts)

    sts = lax.fori_loop(0, n_spans, block, tuple(st_ref[h] for h in range(n_h)))
    for h in range(n_h):
        st_ref[h] = sts[h]

    @pl.when(i == pl.num_programs(1) - 1)
    def _():
        sfin_ref[...] = st_ref[...]


def _scan(p, lb_logits, s0, *, n_heads, layer, reverse, z_off, gate=None):
    _, l, _ = p.shape
    t_tile = _pick(l, SCAN_TILE)
    assert t_tile % SCAN_CHUNK == 0
    nt = l // t_tile
    tok = (lambda i: nt - 1 - i) if reverse else (lambda i: i)
    hp = _pick(n_heads, SCAN_HEADS)
    blk = lambda off: pl.BlockSpec((hp, t_tile, HEAD_DIM), lambda h, i: (off // hp + h, tok(i), 0))
    assert n_heads % hp == 0 and z_off % hp == 0
    n_lb = lb_logits.shape[1]
    dmat, mask, roles = _scan_tables(SCAN_CHUNK, reverse)
    whole = lambda a: pl.BlockSpec(a.shape, lambda h, i: (0,) * a.ndim)
    in_specs = [
        blk(0), blk(2 * n_heads), blk(z_off),
        pl.BlockSpec((hp, n_lb, HEAD_DIM), lambda h, i: (h, 0, 0)),
        pl.BlockSpec((hp, HEAD_DIM, HEAD_DIM), lambda h, i: (h, 0, 0)),
        whole(dmat), whole(mask), whole(roles),
    ]
    args = [p, p, p, lb_logits, s0, dmat, mask, roles]
    if gate is None:
        o_spec = pl.BlockSpec((hp, t_tile, HEAD_DIM), lambda h, i: (h, tok(i), 0))
        o_shape = jax.ShapeDtypeStruct((n_heads, l, HEAD_DIM), BF)
    else:
        o_other, norm_g = gate
        in_specs += [blk(0), blk(n_heads), pl.BlockSpec((hp, 1, HEAD_DIM), lambda h, i: (h, 0, 0))]
        args += [o_other, p, norm_g.reshape(n_heads, 1, HEAD_DIM)]
        o_spec = pl.BlockSpec((t_tile, hp * HEAD_DIM), lambda h, i: (tok(i), h))
        o_shape = jax.ShapeDtypeStruct((l, n_heads * HEAD_DIM), BF)
    return pl.pallas_call(
        functools.partial(_scan_kernel, layer=layer, reverse=reverse, gated=gate is not None),
        grid=(n_heads // hp, nt),
        in_specs=in_specs,
        out_specs=[o_spec, pl.BlockSpec((hp, HEAD_DIM, HEAD_DIM), lambda h, i: (h, 0, 0))],
        out_shape=[o_shape, jax.ShapeDtypeStruct((n_heads, HEAD_DIM, HEAD_DIM), F32)],
        scratch_shapes=[pltpu.VMEM((hp, HEAD_DIM, HEAD_DIM), F32)],
        compiler_params=_cparams("parallel", "arbitrary"),
        name="scan_bwd" if reverse else "scan_fwd",
    )(*args)


def _pool_body(win, prev_ref, cur_ref, next_ref, w_ref, scale_ref, o_ref, y_ref, d_ref, n_rows):
    i = pl.program_id(0)
    tile_rows = cur_ref.shape[0] // GRID_W
    halo = y_ref.shape[0] - tile_rows
    top = halo // 2
    half = win // 2
    r0 = i * tile_rows

    rg = POOL_ROW_GROUP
    gt = rg * GRID_W
    shift = GRID_W.bit_length() - 1
    assert GRID_W == 1 << shift and tile_rows % rg == 0 and top % rg == 0
    t_io = lax.broadcasted_iota(jnp.int32, (gt, gt), 0)
    s_io = lax.broadcasted_iota(jnp.int32, (gt, gt), 1)
    same_row = lax.shift_right_logical(t_io, shift) == lax.shift_right_logical(s_io, shift)
    off = (s_io & (GRID_W - 1)) - (t_io & (GRID_W - 1))
    a01 = jnp.where(same_row & (off >= -half) & (off < win - half), 1.0, 0.0).astype(BF)

    def col_filter(src_ref, src_row, dst_row):
        xs = src_ref[pl.ds(pl.multiple_of(src_row * GRID_W, gt), gt), :]
        y = jnp.dot(a01, xs, preferred_element_type=F32)
        grow = jnp.full((1, y.shape[1]), r0 - top + dst_row, jnp.int32)
        ok = (grow >= 0) & (grow < n_rows)
        y_ref[pl.ds(dst_row, rg)] = jnp.where(ok, y, 0.0).reshape(rg, GRID_W, y.shape[1])

    n_above = -(-half // rg)
    n_below = -(-(half - 1) // rg)

    def above(j, c):
        col_filter(prev_ref, tile_rows - (n_above - j) * rg, top - (n_above - j) * rg)
        return c

    def inside(j, c):
        col_filter(cur_ref, j * rg, top + j * rg)
        return c

    def below(j, c):
        col_filter(next_ref, j * rg, top + tile_rows + j * rg)
        return c

    lax.fori_loop(0, n_above, above, 0)
    lax.fori_loop(0, tile_rows // rg, inside, 0)
    lax.fori_loop(0, n_below, below, 0)

    c_io = lax.broadcasted_iota(jnp.int32, (GRID_W, d_ref.shape[1]), 0)
    cnt_c = jnp.minimum(c_io + win - half, GRID_W) - jnp.maximum(c_io - half, 0)
    inv_c = 1.0 / cnt_c.astype(F32)

    def row_filter(r, c):
        grow = jnp.full((1, d_ref.shape[1]), r0 + r, jnp.int32)
        cnt = jnp.minimum(grow + win - half, n_rows) - jnp.maximum(grow - half, 0)
        z = y_ref[top + r - half]
        for j in range(1, win):
            z = z + y_ref[top + r - half + j]
        z = z * (inv_c * (1.0 / cnt.astype(F32)))
        sl = pl.ds(pl.multiple_of(r * GRID_W, GRID_W), GRID_W)
        d_ref[sl, :] = (z - cur_ref[sl, :].astype(F32)).astype(d_ref.dtype)
        return c

    lax.fori_loop(0, tile_rows, row_filter, 0)
    o_ref[...] = jnp.dot(d_ref[...], w_ref[0], preferred_element_type=F32) * scale_ref[...]


def _pool_kernel(prev_ref, cur_ref, next_ref, w_ref, scale_ref, o_ref, y_ref, d_ref, *, n_rows):
    g = pl.program_id(1)
    for gi, win in enumerate(POOL_WINDOWS):
        @pl.when(g == gi)
        def _(win=win):
            _pool_body(win, prev_ref, cur_ref, next_ref, w_ref, scale_ref, o_ref, y_ref, d_ref, n_rows)


def _pool(h, w_pool, scale):
    l, d = h.shape
    n_groups, gc, _ = w_pool.shape
    n_rows = l // GRID_W
    tile_rows = _pick(n_rows, 16)
    halo = max(POOL_WINDOWS)
    assert tile_rows >= halo // 2 and len(POOL_WINDOWS) == n_groups
    tt = tile_rows * GRID_W
    nt = l // tt
    return pl.pallas_call(
        functools.partial(_pool_kernel, n_rows=n_rows),
        grid=(nt, n_groups),
        in_specs=[
            pl.BlockSpec((tt, gc), lambda i, g: (jnp.maximum(i - 1, 0), g)),
            pl.BlockSpec((tt, gc), lambda i, g: (i, g)),
            pl.BlockSpec((tt, gc), lambda i, g: (jnp.minimum(i + 1, nt - 1), g)),
            pl.BlockSpec((1, gc, gc), lambda i, g: (g, 0, 0)),
            pl.BlockSpec((1, gc), lambda i, g: (0, g)),
        ],
        out_specs=pl.BlockSpec((tt, gc), lambda i, g: (i, g)),
        out_shape=jax.ShapeDtypeStruct((l, d), F32),
        scratch_shapes=[pltpu.VMEM((tile_rows + halo, GRID_W, gc), F32), pltpu.VMEM((tt, gc), BF)],
        compiler_params=_cparams("parallel", "parallel"),
        name="pool",
    )(h, h, h, w_pool, scale.reshape(1, d))


def _router_kernel(h_ref, wrt_ref, bias_ref, upper_ref, eidx_ref, wts_ref, rank_ref, cnt_ref, carry_ref):
    n_exp = wrt_ref.shape[0]
    tm = h_ref.shape[0]

    @pl.when(pl.program_id(0) == 0)
    def _():
        carry_ref[...] = jnp.zeros_like(carry_ref)

    epg = n_exp // N_ROUTE_GROUPS
    logits = lax.dot_general(wrt_ref[...], h_ref[...], (((1,), (1,)), ((), ())), preferred_element_type=F32)
    scores = jax.nn.sigmoid(logits)
    shape3 = (N_ROUTE_GROUPS, epg, tm)
    sc3 = scores.reshape(shape3)
    s3 = (scores + bias_ref[...]).reshape(shape3)
    e_in = lax.broadcasted_iota(jnp.int32, shape3, 1).astype(F32)
    g_io = lax.broadcasted_iota(jnp.int32, shape3, 0).astype(F32)

    m1 = jnp.max(s3, axis=1, keepdims=True)
    i1 = jnp.min(jnp.where(s3 == m1, e_in, epg), axis=1, keepdims=True)
    m2 = jnp.max(jnp.where(e_in == i1, NEG_INF, s3), axis=1, keepdims=True)
    gs = jnp.broadcast_to(m1 + m2, shape3)

    keep = jnp.zeros(shape3, F32)
    cur = gs
    for _ in range(TOPK_ROUTE_GROUPS):
        m = jnp.max(cur, axis=0, keepdims=True)
        idx = jnp.min(jnp.where(cur == m, g_io, N_ROUTE_GROUPS), axis=0, keepdims=True)
        hit = g_io == idx
        keep = jnp.where(hit, 1.0, keep)
        cur = jnp.where(hit, NEG_INF, cur)

    e_io = g_io * epg + e_in
    chosen = jnp.zeros(shape3, F32)
    cur = jnp.where(keep > 0, s3, NEG_INF)
    picks = []
    for _ in range(TOP_K):
        m = jnp.max(jnp.max(cur, axis=0, keepdims=True), axis=1, keepdims=True)
        idx = jnp.where(cur == m, e_io, n_exp)
        idx = jnp.min(jnp.min(idx, axis=0, keepdims=True), axis=1, keepdims=True)
        hit = e_io == idx
        chosen = jnp.where(hit, 1.0, chosen)
        cur = jnp.where(hit, NEG_INF, cur)
        picks.append(idx)

    w = jnp.where(chosen > 0, sc3, 0.0)
    den = jnp.sum(jnp.sum(w, axis=0, keepdims=True), axis=1, keepdims=True)
    w = w / den * ROUTED_SCALE

    ch = chosen.reshape(n_exp, tm).astype(BF)
    carry = carry_ref[...]
    rank = (jnp.dot(ch, upper_ref[...], preferred_element_type=F32)
            + jnp.concatenate([carry] * (tm // LANES), axis=1)).reshape(shape3)
    carry = carry + jnp.dot(ch, jnp.ones((tm, LANES), BF), preferred_element_type=F32)
    carry_ref[...] = carry
    cnt_ref[...] = carry

    def per_pick(vals, idx):
        sel = jnp.where(e_io == idx, vals, 0.0)
        return jnp.sum(jnp.sum(sel, axis=0, keepdims=True), axis=1, keepdims=True).reshape(1, tm)

    eidx_ref[...] = jnp.concatenate([idx.reshape(1, tm) for idx in picks], axis=0).astype(jnp.int32)
    wts_ref[...] = jnp.concatenate([per_pick(w, idx) for idx in picks], axis=0)
    rank_ref[...] = jnp.concatenate([per_pick(rank, idx) for idx in picks], axis=0).astype(jnp.int32)


def _router(h, w_r, b_r):
    t, d = h.shape
    n_exp = w_r.shape[1]
    tm = _pick(t, 512)
    assert tm % LANES == 0
    upper = jnp.asarray(np.triu(np.ones((tm, tm), np.float32), 1), BF)
    pick = pl.BlockSpec((TOP_K, tm), lambda i: (0, i))
    eidx, wts, rank, cnt = pl.pallas_call(
        _router_kernel,
        grid=(t // tm,),
        in_specs=[
            pl.BlockSpec((tm, d), lambda i: (i, 0)),
            pl.BlockSpec((n_exp, d), lambda i: (0, 0)),
            pl.BlockSpec((n_exp, 1), lambda i: (0, 0)),
            pl.BlockSpec((tm, tm), lambda i: (0, 0)),
        ],
        out_specs=[pick, pick, pick, pl.BlockSpec((n_exp, LANES), lambda i: (0, 0))],
        out_shape=[
            jax.ShapeDtypeStruct((TOP_K, t), jnp.int32),
            jax.ShapeDtypeStruct((TOP_K, t), F32),
            jax.ShapeDtypeStruct((TOP_K, t), jnp.int32),
            jax.ShapeDtypeStruct((n_exp, LANES), F32),
        ],
        scratch_shapes=[pltpu.VMEM((n_exp, LANES), F32)],
        compiler_params=_cparams("arbitrary"),
        name="router",
    )(h, w_r.T.astype(BF), b_r.astype(F32).reshape(n_exp, 1), upper)
    return eidx, wts, rank, cnt[:, 0].astype(jnp.int32)


HI_MASK = 0xFFFF0000


def _pack_halves(lo, hi):
    bits = lambda a: pltpu.bitcast(a.astype(BF).astype(F32), jnp.uint32)
    return lax.shift_right_logical(bits(lo), jnp.uint32(16)) | (bits(hi) & jnp.uint32(HI_MASK))


def _pack_rows(v):
    half = v.shape[1] // 2
    return _pack_halves(v[:, :half], v[:, half:])


def _unpack_rows(w):
    lo = pltpu.bitcast(lax.shift_left(w, jnp.uint32(16)), F32)
    hi = pltpu.bitcast(w & jnp.uint32(HI_MASK), F32)
    return lo, hi


def _dispatch_kernel(pos_ref, h_ref, xs_hbm, hp_ref, sem):
    i = pl.program_id(0)
    n_k, tm = pos_ref.shape
    slot = i % 2
    hp_ref[slot] = _pack_rows(h_ref[...].astype(F32))

    for t in range(tm):
        for k in range(n_k):
            pltpu.make_async_copy(hp_ref.at[slot, pl.ds(t, 1)], xs_hbm.at[pl.ds(pos_ref[k, t], 1)],
                                  sem.at[slot]).start(priority=k % 2)

    def drain(s):
        def body(t, c):
            for k in range(n_k):
                pltpu.make_async_copy(hp_ref.at[s, pl.ds(0, 1)], xs_hbm.at[pl.ds(0, 1)], sem.at[s]).wait()
            return c
        lax.fori_loop(0, tm, body, 0, unroll=8)

    @pl.when(i > 0)
    def _():
        drain(1 - slot)

    @pl.when(i == pl.num_programs(0) - 1)
    def _():
        drain(slot)


def _dispatch(h, pos):
    t, d = h.shape
    n_k = pos.shape[0]
    tm = _pick(t, 256)
    return pl.pallas_call(
        _dispatch_kernel,
        grid=(t // tm,),
        in_specs=[
            pl.BlockSpec((n_k, tm), lambda i: (0, i), memory_space=pltpu.SMEM),
            pl.BlockSpec((tm, d), lambda i: (i, 0)),
        ],
        out_specs=pl.BlockSpec(memory_space=pl.ANY),
        out_shape=jax.ShapeDtypeStruct((n_k * t, d // 2), jnp.uint32),
        scratch_shapes=[pltpu.VMEM((2, tm, d // 2), jnp.uint32), pltpu.SemaphoreType.DMA((2,))],
        compiler_params=_cparams("arbitrary"),
        name="dispatch",
    )(pos, h)


def _experts_kernel(tile_ref, exp_ref, lo_ref, hi_ref, xs_ref, wgu_ref, wd_ref, ys_ref,
                    acc_ref, wgu_bf_ref, wd_bf_ref):
    w = pl.program_id(0)
    prev = jnp.maximum(w - 1, 0)
    tm = xs_ref.shape[0]
    half = xs_ref.shape[1]
    d_exp = wd_ref.shape[0]

    @pl.when(jnp.logical_or(w == 0, exp_ref[w] != exp_ref[prev]))
    def _():
        wgu_bf_ref[...] = wgu_ref[...].astype(BF)
        wd_bf_ref[...] = wd_ref[...].astype(BF)

    lo, hi = lo_ref[w], hi_ref[w]
    live = hi > lo
    first = jnp.logical_or(w == 0, tile_ref[w] != tile_ref[prev])

    def run(rows, accumulate):
        n = rows.stop - rows.start
        x_lo, x_hi = _unpack_rows(xs_ref[rows, :])
        gu = (jnp.dot(x_lo.astype(BF), wgu_bf_ref[:half, :], preferred_element_type=F32)
              + jnp.dot(x_hi.astype(BF), wgu_bf_ref[half:, :], preferred_element_type=F32))
        row = lax.broadcasted_iota(jnp.int32, (n, d_exp), 0) + rows.start
        mine = (row >= lo) & (row < hi)
        act = jnp.where(mine, _silu(gu[:, :d_exp]) * gu[:, d_exp:], 0.0).astype(BF)
        chunk = min(DOWN_CHUNK, half)
        assert half % chunk == 0
        for j in range(half // chunk):
            c_lo = slice(j * chunk, (j + 1) * chunk)
            c_hi = slice(half + j * chunk, half + (j + 1) * chunk)
            y_lo = jnp.dot(act, wd_bf_ref[:, c_lo], preferred_element_type=F32)
            y_hi = jnp.dot(act, wd_bf_ref[:, c_hi], preferred_element_type=F32)
            if accumulate:
                y_lo = y_lo + acc_ref[rows, c_lo]
                y_hi = y_hi + acc_ref[rows, c_hi]
            acc_ref[rows, c_lo] = y_lo
            acc_ref[rows, c_hi] = y_hi
            ys_ref[rows, c_lo] = _pack_halves(y_lo, y_hi)

    rb = tm // 2
    top, bottom = slice(0, rb), slice(rb, tm)
    in_top, in_bottom = lo < rb, hi > rb
    cases = [
        (jnp.logical_and(in_top, in_bottom), slice(0, tm), None),
        (jnp.logical_and(in_top, jnp.logical_not(in_bottom)), top, bottom),
        (jnp.logical_and(jnp.logical_not(in_top), in_bottom), bottom, top),
    ]
    for cond, rows, skipped in cases:
        @pl.when(jnp.logical_and(live, jnp.logical_and(cond, first)))
        def _(rows=rows, skipped=skipped):
            if skipped is not None:
                acc_ref[skipped, :] = jnp.zeros((rb, acc_ref.shape[1]), F32)
            run(rows, False)

        @pl.when(jnp.logical_and(live, jnp.logical_and(cond, jnp.logical_not(first))))
        def _(rows=rows):
            run(rows, True)


def _experts(xs, items, w_gu, w_down, layer, tm):
    p, dw = xs.shape
    _, _, d, f2 = w_gu.shape
    n_items = items[0].shape[0]
    grid_spec = pltpu.PrefetchScalarGridSpec(
        num_scalar_prefetch=4,
        grid=(n_items,),
        in_specs=[
            pl.BlockSpec((tm, dw), lambda w, tile, exp, lo, hi: (tile[w], 0)),
            pl.BlockSpec((None, None, d, f2), lambda w, tile, exp, lo, hi: (layer, exp[w], 0, 0)),
            pl.BlockSpec((None, None, f2 // 2, d), lambda w, tile, exp, lo, hi: (layer, exp[w], 0, 0)),
        ],
        out_specs=pl.BlockSpec((tm, dw), lambda w, tile, exp, lo, hi: (tile[w], 0)),
        scratch_shapes=[pltpu.VMEM((tm, d), F32), pltpu.VMEM((d, f2), BF), pltpu.VMEM((f2 // 2, d), BF)],
    )
    return pl.pallas_call(
        _experts_kernel,
        grid_spec=grid_spec,
        out_shape=jax.ShapeDtypeStruct((p, dw), jnp.uint32),
        compiler_params=_cparams("arbitrary"),
        name="experts",
    )(*items, xs, w_gu, w_down)


def _expert_items(cnt, tm, n_tiles):
    n_exp = cnt.shape[0]
    n_items = n_tiles + n_exp - 1
    end = jnp.cumsum(cnt)
    start = end - cnt
    first_tile = start // tm
    n_e = jnp.where(cnt > 0, (end - 1) // tm - first_tile + 1, 0)
    item_end = jnp.cumsum(n_e)
    w = jnp.arange(n_items, dtype=jnp.int32)
    e = jnp.minimum(jnp.searchsorted(item_end, w, side="right"), n_exp - 1).astype(jnp.int32)
    valid = w < item_end[-1]
    tile = jnp.where(valid, first_tile[e] + (w - (item_end[e] - n_e[e])), n_tiles - 1).astype(jnp.int32)
    lo = jnp.where(valid, jnp.maximum(start[e], tile * tm) - tile * tm, 0).astype(jnp.int32)
    hi = jnp.where(valid, jnp.minimum(end[e], (tile + 1) * tm) - tile * tm, 0).astype(jnp.int32)
    e = jnp.where(valid, e, e[jnp.maximum(item_end[-1] - 1, 0)])
    return tile, e, lo, hi


def _shared_kernel(h_ref, wgu_ref, wd_ref, o_ref):
    d_exp = wd_ref.shape[0]
    gu = jnp.dot(h_ref[...], wgu_ref[...], preferred_element_type=F32)
    act = _silu(gu[:, :d_exp]) * gu[:, d_exp:]
    o_ref[...] = jnp.dot(act.astype(BF), wd_ref[...], preferred_element_type=F32)


def _shared(h, w_gu, w_down):
    t, d = h.shape
    f2 = w_gu.shape[1]
    tm = _pick(t, 512)
    return pl.pallas_call(
        _shared_kernel,
        grid=(t // tm,),
        in_specs=[
            pl.BlockSpec((tm, d), lambda i: (i, 0)),
            pl.BlockSpec((d, f2), lambda i: (0, 0)),
            pl.BlockSpec((f2 // 2, d), lambda i: (0, 0)),
        ],
        out_specs=pl.BlockSpec((tm, d), lambda i: (i, 0)),
        out_shape=jax.ShapeDtypeStruct((t, d), F32),
        compiler_params=_cparams("parallel"),
        name="shared",
    )(h, w_gu, w_down)


def _combine_kernel(pos_ref, sh_ref, wts_ref, x_ref, gate_ref, gy_ref, *rest, with_next):
    if with_next:
        gx_ref, shift_ref, scale_ref, ys_hbm, xo_ref, ho_ref, buf_ref, sem = rest
    else:
        ys_hbm, xo_ref, buf_ref, sem = rest
    i = pl.program_id(0)
    n = pl.num_programs(0)
    n_k, tm = pos_ref.shape[1], pos_ref.shape[2]
    half = buf_ref.shape[3]

    def row_copy(j, t, k, slot):
        return pltpu.make_async_copy(ys_hbm.at[pl.ds(pos_ref[j, k, t], 1)], buf_ref.at[slot, k, pl.ds(t, 1)],
                                     sem.at[slot])

    def gather(j, slot):
        def issue(t, c):
            for k in range(n_k):
                row_copy(j, t, k, slot).start(priority=k % 2)
            return c
        lax.fori_loop(0, tm, issue, 0, unroll=8)

    @pl.when(i == 0)
    def _():
        gather(0, 0)

    def drain(s):
        def body(t, c):
            for k in range(n_k):
                pltpu.make_async_copy(ys_hbm.at[pl.ds(0, 1)], buf_ref.at[s, 0, pl.ds(0, 1)], sem.at[s]).wait()
            return c
        lax.fori_loop(0, tm, body, 0, unroll=8)

    def step(slot):
        nxt_slot = 1 - slot
        drain(slot)

        acc_lo = sh_ref[:, :half]
        acc_hi = sh_ref[:, half:]
        for k in range(n_k):
            y_lo, y_hi = _unpack_rows(buf_ref[slot, k])
            wk = wts_ref[:, k:k + 1]
            acc_lo = acc_lo + wk * y_lo
            acc_hi = acc_hi + wk * y_hi
            for t in range(tm):
                row_copy(1, t, k, nxt_slot).start(priority=t % 2)

        @pl.when(i == n - 1)
        def _():
            drain(nxt_slot)

        f = jnp.concatenate([acc_lo, acc_hi], axis=1)
        xn = x_ref[...] + gate_ref[...] * (_rms(f) * gy_ref[...])
        xo_ref[...] = xn
        if with_next:
            hn = _rms(xn) * gx_ref[...]
            ho_ref[...] = (hn * (1 + scale_ref[...]) + shift_ref[...]).astype(ho_ref.dtype)

    for parity in range(2):
        @pl.when(i % 2 == parity)
        def _(parity=parity):
            step(parity)


def _combine(ys, pos, wts, sh, x, gate, gy, nxt=None):
    t, d = sh.shape
    n_k = pos.shape[0]
    tm = _pick(t, 64)
    nt = t // tm
    pos3 = pos.reshape(n_k, nt, tm).transpose(1, 0, 2)
    pos_pair = jnp.stack([pos3, jnp.concatenate([pos3[1:], pos3[-1:]], axis=0)], axis=1)
    row = pl.BlockSpec((tm, d), lambda i: (i, 0))
    vec = pl.BlockSpec((1, d), lambda i: (0, 0))
    v = lambda a: a.reshape(1, d)
    vecs = [v(gate), v(gy)] + ([v(a) for a in nxt] if nxt is not None else [])
    out_specs = [row, row] if nxt is not None else row
    out_shape = [jax.ShapeDtypeStruct((t, d), F32)] + ([jax.ShapeDtypeStruct((t, d), BF)] if nxt is not None else [])
    return pl.pallas_call(
        functools.partial(_combine_kernel, with_next=nxt is not None),
        grid=(nt,),
        in_specs=[
            pl.BlockSpec((None, 2, n_k, tm), lambda i: (i, 0, 0, 0), memory_space=pltpu.SMEM),
            row,
            pl.BlockSpec((tm, n_k), lambda i: (i, 0)),
            row,
        ] + [vec] * len(vecs) + [pl.BlockSpec(memory_space=pl.ANY)],
        out_specs=out_specs,
        out_shape=out_shape if nxt is not None else out_shape[0],
        scratch_shapes=[pltpu.VMEM((2, n_k, tm, d // 2), jnp.uint32), pltpu.SemaphoreType.DMA((2,))],
        compiler_params=_cparams("arbitrary"),
        name="combine",
    )(pos_pair, sh, wts, x, *vecs, ys)


def _moe_layer(h, x, gate, gy, nxt, layer, router_w, router_bias, exp_w_gu, exp_w_down, shared_w_gu, shared_w_down):
    t = h.shape[0]
    s_gu, s_down = shared_w_gu[layer], shared_w_down[layer]
    eidx, wts, rank, cnt = _router(h, router_w[layer], router_bias[layer])
    start = jnp.cumsum(cnt) - cnt
    n_exp = cnt.shape[0]
    is_e = eidx[None] == jnp.arange(n_exp, dtype=jnp.int32)[:, None, None]
    pos = jnp.sum(jnp.where(is_e, start[:, None, None], 0), axis=0) + rank
    tm = _pick(TOP_K * t, EXPERT_TILE)
    items = _expert_items(cnt, tm, TOP_K * t // tm)
    xs = _dispatch(h, pos)
    ys = _experts(xs, items, exp_w_gu, exp_w_down, layer, tm)
    sh = _shared(h, s_gu.astype(BF), s_down.astype(BF))
    return _combine(ys, pos, wts.T, sh, x, gate, gy, nxt)


def kernel(x, c, ctx, c_ctx, w_mod, b_mod, norm_g, hgrn_w_in, hgrn_lb_logits, hgrn_norm_g, hgrn_w_out,
           pool_w, pool_scale, router_w, router_bias, exp_w_gu, exp_w_down, shared_w_gu, shared_w_down):
    batch, seq, d = x.shape
    assert batch == 1 and c.shape[0] == 1
    depth = w_mod.shape[0]
    assert depth == 2 and w_mod.shape[2] == N_MOD * d
    n_heads = hgrn_w_in.shape[2] // 5 // HEAD_DIM
    x0 = x[0]

    cvecs = jnp.concatenate([c, c_ctx[None], jnp.zeros((6, d), F32)], axis=0)
    mods = _mods(cvecs, w_mod, b_mod)
    mod = lambda layer, who, j: mods[layer, who, j * d:(j + 1) * d]

    w_in = hgrn_w_in[0].astype(BF)
    hx = _normmod(x0, norm_g[0, 0], mod(0, 0, 0), mod(0, 0, 1))
    hc = _normmod(ctx[0], norm_g[0, 0], mod(0, 1, 0), mod(0, 1, 1))
    p = _mm_heads(hx, w_in)
    pc = _mm_heads(hc, w_in)
    n_lb = hgrn_lb_logits.shape[1]
    lbl = hgrn_lb_logits.astype(F32).reshape(2, n_lb, n_heads, HEAD_DIM).transpose(0, 2, 1, 3)
    zero_state = jnp.zeros((n_heads, HEAD_DIM, HEAD_DIM), F32)
    scan = functools.partial(_scan, n_heads=n_heads, layer=0)
    _, s_f = scan(pc, lbl[0], zero_state, reverse=False, z_off=3 * n_heads)
    _, s_b = scan(pc, lbl[1], zero_state, reverse=True, z_off=4 * n_heads)
    o_f, _ = scan(p, lbl[0], s_f, reverse=False, z_off=3 * n_heads)
    a, _ = scan(p, lbl[1], s_b, reverse=True, z_off=4 * n_heads, gate=(o_f, hgrn_norm_g[0]))
    y = _mm(a, hgrn_w_out[0].astype(BF), F32)
    x1, h = _resid(x0, y, mod(0, 0, 2), norm_g[0, 1], (norm_g[0, 2], mod(0, 0, 3), mod(0, 0, 4)))
    moe = functools.partial(_moe_layer, router_w=router_w, router_bias=router_bias, exp_w_gu=exp_w_gu,
                            exp_w_down=exp_w_down, shared_w_gu=shared_w_gu, shared_w_down=shared_w_down)
    x2, h = moe(h, x1, mod(0, 0, 5), norm_g[0, 3], (norm_g[1, 0], mod(1, 0, 0), mod(1, 0, 1)), 0)

    y = _pool(h, pool_w[0].astype(BF), pool_scale[0])
    x3, h = _resid(x2, y, mod(1, 0, 2), norm_g[1, 1], (norm_g[1, 2], mod(1, 0, 3), mod(1, 0, 4)))
    x4 = moe(h, x3, mod(1, 0, 5), norm_g[1, 3], None, 1)
    return x4[None]
```

```python
import functools

import jax
import jax.numpy as jnp
import numpy as np
from jax import lax
from jax.experimental import pallas as pl
from jax.experimental.pallas import tpu as pltpu

EPS = 1e-6
BF = jnp.bfloat16
F32 = jnp.float32
NEG_INF = float("-inf")

LANES = 128
HEAD_DIM = 128
SCAN_CHUNK = 128
SCAN_UNROLL = 4
SCAN_TILE = 2048
SCAN_HEADS = 2
GRID_W = 64
POOL_WINDOWS = (2, 4, 8, 16)
POOL_ROW_GROUP = 4
N_ROUTE_GROUPS = 8
TOPK_ROUTE_GROUPS = 4
TOP_K = 8
ROUTED_SCALE = 2.5
N_MOD = 6
EXPERT_TILE = 512
DOWN_CHUNK = 512
VMEM_LIMIT =56 * 1024 * 1024


def _cparams(*sem):
    return pltpu.CompilerParams(dimension_semantics=sem, vmem_limit_bytes=VMEM_LIMIT)


def _silu(v):
    return v * jax.nn.sigmoid(v)


def _rms(v):
    return v * lax.rsqrt(jnp.mean(v * v, axis=-1, keepdims=True) + EPS)


def _pick(n, want):
    t = min(n, want)
    while n % t:
        t //= 2
    assert t >= 1
    return t


def _mods_kernel(s_ref, w_ref, b_ref, o_ref):
    s = _silu(s_ref[...])
    o_ref[0] = jnp.dot(s.astype(BF), w_ref[0].astype(BF), preferred_element_type=F32) + b_ref[0]


def _mods(cvecs, w_mod, b_mod):
    depth, d, n = w_mod.shape
    tn = _pick(n, 512)
    return pl.pallas_call(
        _mods_kernel,
        grid=(depth, n // tn),
        in_specs=[
            pl.BlockSpec((8, d), lambda l, j: (0, 0)),
            pl.BlockSpec((1, d, tn), lambda l, j: (l, 0, j)),
            pl.BlockSpec((1, 1, tn), lambda l, j: (l, 0, j)),
        ],
        out_specs=pl.BlockSpec((1, 8, tn), lambda l, j: (l, 0, j)),
        out_shape=jax.ShapeDtypeStruct((depth, 8, n), F32),
        compiler_params=_cparams("parallel", "parallel"),
        name="mods",
    )(cvecs, w_mod, b_mod.reshape(depth, 1, n))


def _normmod_kernel(x_ref, g_ref, sh_ref, sc_ref, o_ref):
    y = _rms(x_ref[...]) * g_ref[...]
    o_ref[...] = (y * (1 + sc_ref[...]) + sh_ref[...]).astype(o_ref.dtype)


def _normmod(x, g, shift, scale):
    m, d = x.shape
    tm = _pick(m, 256)
    vec = pl.BlockSpec((1, d), lambda i: (0, 0))
    return pl.pallas_call(
        _normmod_kernel,
        grid=(m // tm,),
        in_specs=[pl.BlockSpec((tm, d), lambda i: (i, 0)), vec, vec, vec],
        out_specs=pl.BlockSpec((tm, d), lambda i: (i, 0)),
        out_shape=jax.ShapeDtypeStruct((m, d), BF),
        compiler_params=_cparams("parallel"),
        name="normmod",
    )(x, g.reshape(1, d), shift.reshape(1, d), scale.reshape(1, d))


def _resid_norm_kernel(x_ref, y_ref, gate_ref, gy_ref, gx_ref, sh_ref, sc_ref, xo_ref, ho_ref):
    yn = _rms(y_ref[...].astype(F32)) * gy_ref[...]
    xn = x_ref[...] + gate_ref[...] * yn
    xo_ref[...] = xn
    hn = _rms(xn) * gx_ref[...]
    ho_ref[...] = (hn * (1 + sc_ref[...]) + sh_ref[...]).astype(ho_ref.dtype)


def _resid(x, y, gate, gy, nxt):
    m, d = x.shape
    tm = _pick(m, 256)
    row = pl.BlockSpec((tm, d), lambda i: (i, 0))
    vec = pl.BlockSpec((1, d), lambda i: (0, 0))
    v = lambda a: a.reshape(1, d)
    gx, shift, scale = nxt
    return pl.pallas_call(
        _resid_norm_kernel,
        grid=(m // tm,),
        in_specs=[row, row, vec, vec, vec, vec, vec],
        out_specs=[row, row],
        out_shape=[jax.ShapeDtypeStruct((m, d), F32), jax.ShapeDtypeStruct((m, d), BF)],
        compiler_params=_cparams("parallel"),
        name="resid_norm",
    )(x, y, v(gate), v(gy), v(gx), v(shift), v(scale))


def _mm_kernel(a_ref, w_ref, o_ref):
    o_ref[...] = jnp.dot(a_ref[...], w_ref[...], preferred_element_type=F32).astype(o_ref.dtype)


def _mm(a, w, out_dtype):
    m, k = a.shape
    n = w.shape[1]
    tm, tn = _pick(m, 1024), _pick(n, 1024)
    return pl.pallas_call(
        _mm_kernel,
        grid=(m // tm, n // tn),
        in_specs=[pl.BlockSpec((tm, k), lambda i, j: (i, 0)), pl.BlockSpec((k, tn), lambda i, j: (0, j))],
        out_specs=pl.BlockSpec((tm, tn), lambda i, j: (i, j)),
        out_shape=jax.ShapeDtypeStruct((m, n), out_dtype),
        compiler_params=_cparams("parallel", "parallel"),
        name="mm",
    )(a, w)


def _mm_heads_kernel(a_ref, w_ref, o_ref):
    r = jnp.dot(a_ref[...], w_ref[...], preferred_element_type=F32)
    for hh in range(o_ref.shape[0]):
        o_ref[hh] = r[:, hh * LANES:(hh + 1) * LANES].astype(o_ref.dtype)


def _mm_heads(a, w):
    m, k = a.shape
    n = w.shape[1]
    tm, tn = _pick(m, 1024), _pick(n, 1024)
    return pl.pallas_call(
        _mm_heads_kernel,
        grid=(m // tm, n // tn),
        in_specs=[pl.BlockSpec((tm, k), lambda i, j: (i, 0)), pl.BlockSpec((k, tn), lambda i, j: (0, j))],
        out_specs=pl.BlockSpec((tn // LANES, tm, LANES), lambda i, j: (j, i, 0)),
        out_shape=jax.ShapeDtypeStruct((n // LANES, m, LANES), BF),
        compiler_params=_cparams("parallel", "parallel"),
        name="proj",
    )(a, w)


def _scan_tables(c, reverse):
    n_levels = c.bit_length() - 1
    assert c == 1 << n_levels
    t = np.arange(c)[:, None]
    u = np.arange(c)[None, :]
    dmats = [u <= t]
    masks = [u == t]
    roles = []
    for lv in range(n_levels):
        size = c >> lv
        start = (t // size) * size
        boundary = start + size // 2 - 1
        later = t > boundary
        if (size // 2) % 8:
            dmats.append(np.where(later, (u > boundary) & (u <= t), (u > t) & (u <= boundary)))
        masks.append(later & (u <= boundary.T) & (start == start.T))
        roles.append(np.broadcast_to(later, (c, HEAD_DIM)))
    dmat, mask, role = np.stack(dmats), np.stack(masks), np.stack(roles)
    if reverse:
        dmat, mask, role = dmat[:, ::-1, ::-1], mask[:, ::-1, ::-1], role[:, ::-1]
    return (jnp.asarray(dmat.reshape(-1, c), BF), jnp.asarray(mask, F32), jnp.asarray(role, F32))


def _scan_kernel(q_ref, v_ref, z_ref, lbl_ref, s0_ref, dmat_ref, mask_ref, roles_ref, *rest, layer, reverse, gated):
    if gated:
        oo_ref, g_ref, ng_ref, o_ref, sfin_ref, st_ref = rest
    else:
        o_ref, sfin_ref, st_ref = rest
    c = SCAN_CHUNK
    n_levels = roles_ref.shape[0]
    n_h, t_tile = q_ref.shape[0], q_ref.shape[1]
    span = min(SCAN_UNROLL * c, t_tile)
    n_spans = t_tile // span
    n_c = span // c
    i = pl.program_id(1)

    @pl.when(i == 0)
    def _():
        st_ref[...] = s0_ref[...]

    lbs = []
    for h in range(n_h):
        lg = lbl_ref[h]
        e = jnp.exp(lg - jnp.max(lg, axis=0, keepdims=True))
        p = e / jnp.sum(e, axis=0, keepdims=True)
        lbs.append(jnp.sum(p[:layer + 1], axis=0, keepdims=True))

    nt_dims = (((1,), (1,)), ((), ()))
    tn_dims = (((0,), (0,)), ((), ()))
    last = 0 if reverse else c - 1

    def level_rows(lv, qc, kc):
        size = c >> lv
        half = size // 2
        if half % 8:
            return jnp.where(roles_ref[lv] > 0, qc, kc)
        first, second = (qc, kc) if reverse else (kc, qc)
        parts = []
        for a in range(0, c, size):
            parts += [first[a:a + half], second[a + half:a + size]]
        return jnp.concatenate(parts, axis=0)

    def level_exponent(lv, b):
        size = c >> lv
        half = size // 2
        parts = []
        for a in range(0, c, size):
            if reverse:
                bnd = b[a + half:a + half + 1]
                parts += [b[a:a + half] - bnd, bnd - b[a + half:a + size]]
            else:
                bnd = b[a + half - 1:a + half]
                parts += [bnd - b[a:a + half], b[a + half:a + size] - bnd]
        return jnp.concatenate(parts, axis=0)

    def chunk(st, lf, qc, kc, vc):
        ex = jnp.dot(dmat_ref[...], lf.astype(BF), preferred_element_type=F32)
        b = ex[0:c]
        b_last = b[last:last + 1]
        qb, kb, vb = qc.astype(BF), kc.astype(BF), vc.astype(BF)
        att = lax.dot_general(qb, kb, nt_dims, preferred_element_type=F32) * mask_ref[0]
        n_small = 0
        for lv in range(n_levels):
            if (c >> lv) // 2 % 8:
                n_small += 1
                d = ex[n_small * c:(n_small + 1) * c]
            else:
                d = level_exponent(lv, b)
            xb = (level_rows(lv, qc, kc) * jnp.exp2(d)).astype(BF)
            gram = lax.dot_general(xb, xb, nt_dims, preferred_element_type=F32)
            att = att + gram * mask_ref[1 + lv]
        qd = (qc * jnp.exp2(b)).astype(BF)
        o = (jnp.dot(att.astype(BF), vb, preferred_element_type=F32)
             + lax.dot_general(qd, st.astype(BF), nt_dims, preferred_element_type=F32))
        kd = (kc * jnp.exp2(b_last - b)).astype(BF)
        st_new = st * jnp.exp2(b_last) + lax.dot_general(vb, kd, tn_dims, preferred_element_type=F32)
        return st_new, o

    def block(n, sts):
        j = (n_spans - 1 - n) if reverse else n
        rows = pl.ds(pl.multiple_of(j * span, span), span)
        sts = list(sts)
        pre = []
        for h in range(n_h):
            z = z_ref[h, rows, :].astype(F32)
            f = lbs[h] + (1 - lbs[h]) * jax.nn.sigmoid(z)
            pre.append((jnp.log2(f), _silu(q_ref[h, rows, :].astype(F32)), 1 - f, v_ref[h, rows, :].astype(F32)))
        outs = [[None] * n_c for _ in range(n_h)]
        for m in (range(n_c - 1, -1, -1) if reverse else range(n_c)):
            sl = slice(m * c, (m + 1) * c)
            for h in range(n_h):
                logf, q, kk, v = pre[h]
                sts[h], outs[h][m] = chunk(sts[h], logf[sl], q[sl], kk[sl], v[sl])
        for h in range(n_h):
            o = jnp.concatenate(outs[h], axis=0)
            if gated:
                o = _rms(o + oo_ref[h, rows, :].astype(F32)) * ng_ref[h]
                o_ref[rows, h * HEAD_DIM:(h + 1) * HEAD_DIM] = (
                    o * _silu(g_ref[h, rows, :].astype(F32))).astype(o_ref.dtype)
            else:
                o_ref[h, rows, :] = o.astype(o_ref.dtype)
        return tuple(sts)

    sts = lax.fori_loop(0, n_spans, block, tuple(st_ref[h] for h in range(n_h)))
    for h in range(n_h):
        st_ref[h] = sts[h]

    @pl.when(i == pl.num_programs(1) - 1)
    def _():
        sfin_ref[...] = st_ref[...]


def _scan(p, lb_logits, s0, *, n_heads, layer, reverse, z_off, gate=None):
    _, l, _ = p.shape
    t_tile = _pick(l, SCAN_TILE)
    assert t_tile % SCAN_CHUNK == 0
    nt = l // t_tile
    tok = (lambda i: nt - 1 - i) if reverse else (lambda i: i)
    hp = _pick(n_heads, SCAN_HEADS)
    blk = lambda off: pl.BlockSpec((hp, t_tile, HEAD_DIM), lambda h, i: (off // hp + h, tok(i), 0))
    assert n_heads % hp == 0 and z_off % hp == 0
    n_lb = lb_logits.shape[1]
    dmat, mask, roles = _scan_tables(SCAN_CHUNK, reverse)
    whole = lambda a: pl.BlockSpec(a.shape, lambda h, i: (0,) * a.ndim)
    in_specs = [
        blk(0), blk(2 * n_heads), blk(z_off),
        pl.BlockSpec((hp, n_lb, HEAD_DIM), lambda h, i: (h, 0, 0)),
        pl.BlockSpec((hp, HEAD_DIM, HEAD_DIM), lambda h, i: (h, 0, 0)),
        whole(dmat), whole(mask), whole(roles),
    ]
    args = [p, p, p, lb_logits, s0, dmat, mask, roles]
    if gate is None:
        o_spec = pl.BlockSpec((hp, t_tile, HEAD_DIM), lambda h, i: (h, tok(i), 0))
        o_shape = jax.ShapeDtypeStruct((n_heads, l, HEAD_DIM), BF)
    else:
        o_other, norm_g = gate
        in_specs += [blk(0), blk(n_heads), pl.BlockSpec((hp, 1, HEAD_DIM), lambda h, i: (h, 0, 0))]
        args += [o_other, p, norm_g.reshape(n_heads, 1, HEAD_DIM)]
        o_spec = pl.BlockSpec((t_tile, hp * HEAD_DIM), lambda h, i: (tok(i), h))
        o_shape = jax.ShapeDtypeStruct((l, n_heads * HEAD_DIM), BF)
    return pl.pallas_call(
        functools.partial(_scan_kernel, layer=layer, reverse=reverse, gated=gate is not None),
        grid=(n_heads // hp, nt),
        in_specs=in_specs,
        out_specs=[o_spec, pl.BlockSpec((hp, HEAD_DIM, HEAD_DIM), lambda h, i: (h, 0, 0))],
        out_shape=[o_shape, jax.ShapeDtypeStruct((n_heads, HEAD_DIM, HEAD_DIM), F32)],
        scratch_shapes=[pltpu.VMEM((hp, HEAD_DIM, HEAD_DIM), F32)],
        compiler_params=_cparams("parallel", "arbitrary"),
        name="scan_bwd" if reverse else "scan_fwd",
    )(*args)


def _pool_body(win, prev_ref, cur_ref, next_ref, w_ref, scale_ref, o_ref, y_ref, d_ref, n_rows):
    i = pl.program_id(0)
    tile_rows = cur_ref.shape[0] // GRID_W
    halo = y_ref.shape[0] - tile_rows
    top = halo // 2
    half = win // 2
    r0 = i * tile_rows

    rg = POOL_ROW_GROUP
    gt = rg * GRID_W
    shift = GRID_W.bit_length() - 1
    assert GRID_W == 1 << shift and tile_rows % rg == 0 and top % rg == 0
    t_io = lax.broadcasted_iota(jnp.int32, (gt, gt), 0)
    s_io = lax.broadcasted_iota(jnp.int32, (gt, gt), 1)
    same_row = lax.shift_right_logical(t_io, shift) == lax.shift_right_logical(s_io, shift)
    off = (s_io & (GRID_W - 1)) - (t_io & (GRID_W - 1))
    a01 = jnp.where(same_row & (off >= -half) & (off < win - half), 1.0, 0.0).astype(BF)

    def col_filter(src_ref, src_row, dst_row):
        xs = src_ref[pl.ds(pl.multiple_of(src_row * GRID_W, gt), gt), :]
        y = jnp.dot(a01, xs, preferred_element_type=F32)
        grow = jnp.full((1, y.shape[1]), r0 - top + dst_row, jnp.int32)
        ok = (grow >= 0) & (grow < n_rows)
        y_ref[pl.ds(dst_row, rg)] = jnp.where(ok, y, 0.0).reshape(rg, GRID_W, y.shape[1])

    n_above = -(-half // rg)
    n_below = -(-(half - 1) // rg)

    def above(j, c):
        col_filter(prev_ref, tile_rows - (n_above - j) * rg, top - (n_above - j) * rg)
        return c

    def inside(j, c):
        col_filter(cur_ref, j * rg, top + j * rg)
        return c

    def below(j, c):
        col_filter(next_ref, j * rg, top + tile_rows + j * rg)
        return c

    lax.fori_loop(0, n_above, above, 0)
    lax.fori_loop(0, tile_rows // rg, inside, 0)
    lax.fori_loop(0, n_below, below, 0)

    c_io = lax.broadcasted_iota(jnp.int32, (GRID_W, d_ref.shape[1]), 0)
    cnt_c = jnp.minimum(c_io + win - half, GRID_W) - jnp.maximum(c_io - half, 0)
    inv_c = 1.0 / cnt_c.astype(F32)

    def row_filter(r, c):
        grow = jnp.full((1, d_ref.shape[1]), r0 + r, jnp.int32)
        cnt = jnp.minimum(grow + win - half, n_rows) - jnp.maximum(grow - half, 0)
        z = y_ref[top + r - half]
        for j in range(1, win):
            z = z + y_ref[top + r - half + j]
        z = z * (inv_c * (1.0 / cnt.astype(F32)))
        sl = pl.ds(pl.multiple_of(r * GRID_W, GRID_W), GRID_W)
        d_ref[sl, :] = (z - cur_ref[sl, :].astype(F32)).astype(d_ref.dtype)
        return c

    lax.fori_loop(0, tile_rows, row_filter, 0)
    o_ref[...] = jnp.dot(d_ref[...], w_ref[0], preferred_element_type=F32) * scale_ref[...]


def _pool_kernel(prev_ref, cur_ref, next_ref, w_ref, scale_ref, o_ref, y_ref, d_ref, *, n_rows):
    g = pl.program_id(1)
    for gi, win in enumerate(POOL_WINDOWS):
        @pl.when(g == gi)
        def _(win=win):
            _pool_body(win, prev_ref, cur_ref, next_ref, w_ref, scale_ref, o_ref, y_ref, d_ref, n_rows)


def _pool(h, w_pool, scale):
    l, d = h.shape
    n_groups, gc, _ = w_pool.shape
    n_rows = l // GRID_W
    tile_rows = _pick(n_rows, 16)
    halo = max(POOL_WINDOWS)
    assert tile_rows >= halo // 2 and len(POOL_WINDOWS) == n_groups
    tt = tile_rows * GRID_W
    nt = l // tt
    return pl.pallas_call(
        functools.partial(_pool_kernel, n_rows=n_rows),
        grid=(nt, n_groups),
        in_specs=[
            pl.BlockSpec((tt, gc), lambda i, g: (jnp.maximum(i - 1, 0), g)),
            pl.BlockSpec((tt, gc), lambda i, g: (i, g)),
            pl.BlockSpec((tt, gc), lambda i, g: (jnp.minimum(i + 1, nt - 1), g)),
            pl.BlockSpec((1, gc, gc), lambda i, g: (g, 0, 0)),
            pl.BlockSpec((1, gc), lambda i, g: (0, g)),
        ],
        out_specs=pl.BlockSpec((tt, gc), lambda i, g: (i, g)),
        out_shape=jax.ShapeDtypeStruct((l, d), F32),
        scratch_shapes=[pltpu.VMEM((tile_rows + halo, GRID_W, gc), F32), pltpu.VMEM((tt, gc), BF)],
        compiler_params=_cparams("parallel", "parallel"),
        name="pool",
    )(h, h, h, w_pool, scale.reshape(1, d))


def _router_kernel(h_ref, wrt_ref, bias_ref, upper_ref, eidx_ref, wts_ref, rank_ref, cnt_ref, carry_ref):
    n_exp = wrt_ref.shape[0]
    tm = h_ref.shape[0]

    @pl.when(pl.program_id(0) == 0)
    def _():
        carry_ref[...] = jnp.zeros_like(carry_ref)

    epg = n_exp // N_ROUTE_GROUPS
    logits = lax.dot_general(wrt_ref[...], h_ref[...], (((1,), (1,)), ((), ())), preferred_element_type=F32)
    scores = jax.nn.sigmoid(logits)
    shape3 = (N_ROUTE_GROUPS, epg, tm)
    sc3 = scores.reshape(shape3)
    s3 = (scores + bias_ref[...]).reshape(shape3)
    e_in = lax.broadcasted_iota(jnp.int32, shape3, 1).astype(F32)
    g_io = lax.broadcasted_iota(jnp.int32, shape3, 0).astype(F32)

    m1 = jnp.max(s3, axis=1, keepdims=True)
    i1 = jnp.min(jnp.where(s3 == m1, e_in, epg), axis=1, keepdims=True)
    m2 = jnp.max(jnp.where(e_in == i1, NEG_INF, s3), axis=1, keepdims=True)
    gs = jnp.broadcast_to(m1 + m2, shape3)

    keep = jnp.zeros(shape3, F32)
    cur = gs
    for _ in range(TOPK_ROUTE_GROUPS):
        m = jnp.max(cur, axis=0, keepdims=True)
        idx = jnp.min(jnp.where(cur == m, g_io, N_ROUTE_GROUPS), axis=0, keepdims=True)
        hit = g_io == idx
        keep = jnp.where(hit, 1.0, keep)
        cur = jnp.where(hit, NEG_INF, cur)

    e_io = g_io * epg + e_in
    chosen = jnp.zeros(shape3, F32)
    cur = jnp.where(keep > 0, s3, NEG_INF)
    picks = []
    for _ in range(TOP_K):
        m = jnp.max(jnp.max(cur, axis=0, keepdims=True), axis=1, keepdims=True)
        idx = jnp.where(cur == m, e_io, n_exp)
        idx = jnp.min(jnp.min(idx, axis=0, keepdims=True), axis=1, keepdims=True)
        hit = e_io == idx
        chosen = jnp.where(hit, 1.0, chosen)
        cur = jnp.where(hit, NEG_INF, cur)
        picks.append(idx)

    w = jnp.where(chosen > 0, sc3, 0.0)
    den = jnp.sum(jnp.sum(w, axis=0, keepdims=True), axis=1, keepdims=True)
    w = w / den * ROUTED_SCALE

    ch = chosen.reshape(n_exp, tm).astype(BF)
    carry = carry_ref[...]
    rank = (jnp.dot(ch, upper_ref[...], preferred_element_type=F32)
            + jnp.concatenate([carry] * (tm // LANES), axis=1)).reshape(shape3)
    carry = carry + jnp.dot(ch, jnp.ones((tm, LANES), BF), preferred_element_type=F32)
    carry_ref[...] = carry
    cnt_ref[...] = carry

    def per_pick(vals, idx):
        sel = jnp.where(e_io == idx, vals, 0.0)
        return jnp.sum(jnp.sum(sel, axis=0, keepdims=True), axis=1, keepdims=True).reshape(1, tm)

    eidx_ref[...] = jnp.concatenate([idx.reshape(1, tm) for idx in picks], axis=0).astype(jnp.int32)
    wts_ref[...] = jnp.concatenate([per_pick(w, idx) for idx in picks], axis=0)
    rank_ref[...] = jnp.concatenate([per_pick(rank, idx) for idx in picks], axis=0).astype(jnp.int32)


def _router(h, w_r, b_r):
    t, d = h.shape
    n_exp = w_r.shape[1]
    tm = _pick(t, 512)
    assert tm % LANES == 0
    upper = jnp.asarray(np.triu(np.ones((tm, tm), np.float32), 1), BF)
    pick = pl.BlockSpec((TOP_K, tm), lambda i: (0, i))
    eidx, wts, rank, cnt = pl.pallas_call(
        _router_kernel,
        grid=(t // tm,),
        in_specs=[
            pl.BlockSpec((tm, d), lambda i: (i, 0)),
            pl.BlockSpec((n_exp, d), lambda i: (0, 0)),
            pl.BlockSpec((n_exp, 1), lambda i: (0, 0)),
            pl.BlockSpec((tm, tm), lambda i: (0, 0)),
        ],
        out_specs=[pick, pick, pick, pl.BlockSpec((n_exp, LANES), lambda i: (0, 0))],
        out_shape=[
            jax.ShapeDtypeStruct((TOP_K, t), jnp.int32),
            jax.ShapeDtypeStruct((TOP_K, t), F32),
            jax.ShapeDtypeStruct((TOP_K, t), jnp.int32),
            jax.ShapeDtypeStruct((n_exp, LANES), F32),
        ],
        scratch_shapes=[pltpu.VMEM((n_exp, LANES), F32)],
        compiler_params=_cparams("arbitrary"),
        name="router",
    )(h, w_r.T.astype(BF), b_r.astype(F32).reshape(n_exp, 1), upper)
    return eidx, wts, rank, cnt[:, 0].astype(jnp.int32)


HI_MASK = 0xFFFF0000


def _pack_halves(lo, hi):
    bits = lambda a: pltpu.bitcast(a.astype(BF).astype(F32), jnp.uint32)
    return lax.shift_right_logical(bits(lo), jnp.uint32(16)) | (bits(hi) & jnp.uint32(HI_MASK))


def _pack_rows(v):
    half = v.shape[1] // 2
    return _pack_halves(v[:, :half], v[:, half:])


def _unpack_rows(w):
    lo = pltpu.bitcast(lax.shift_left(w, jnp.uint32(16)), F32)
    hi = pltpu.bitcast(w & jnp.uint32(HI_MASK), F32)
    return lo, hi


def _dispatch_kernel(pos_ref, h_ref, xs_hbm, hp_ref, sem):
    i = pl.program_id(0)
    n_k, tm = pos_ref.shape
    slot = i % 2
    hp_ref[slot] = _pack_rows(h_ref[...].astype(F32))

    for t in range(tm):
        for k in range(n_k):
            pltpu.make_async_copy(hp_ref.at[slot, pl.ds(t, 1)], xs_hbm.at[pl.ds(pos_ref[k, t], 1)],
                                  sem.at[slot]).start(priority=k % 2)

    def drain(s):
        def body(t, c):
            for k in range(n_k):
                pltpu.make_async_copy(hp_ref.at[s, pl.ds(0, 1)], xs_hbm.at[pl.ds(0, 1)], sem.at[s]).wait()
            return c
        lax.fori_loop(0, tm, body, 0, unroll=8)

    @pl.when(i > 0)
    def _():
        drain(1 - slot)

    @pl.when(i == pl.num_programs(0) - 1)
    def _():
        drain(slot)


def _dispatch(h, pos):
    t, d = h.shape
    n_k = pos.shape[0]
    tm = _pick(t, 256)
    return pl.pallas_call(
        _dispatch_kernel,
        grid=(t // tm,),
        in_specs=[
            pl.BlockSpec((n_k, tm), lambda i: (0, i), memory_space=pltpu.SMEM),
            pl.BlockSpec((tm, d), lambda i: (i, 0)),
        ],
        out_specs=pl.BlockSpec(memory_space=pl.ANY),
        out_shape=jax.ShapeDtypeStruct((n_k * t, d // 2), jnp.uint32),
        scratch_shapes=[pltpu.VMEM((2, tm, d // 2), jnp.uint32), pltpu.SemaphoreType.DMA((2,))],
        compiler_params=_cparams("arbitrary"),
        name="dispatch",
    )(pos, h)


def _experts_kernel(tile_ref, exp_ref, lo_ref, hi_ref, xs_ref, wgu_ref, wd_ref, ys_ref,
                    acc_ref, wgu_bf_ref, wd_bf_ref):
    w = pl.program_id(0)
    prev = jnp.maximum(w - 1, 0)
    tm = xs_ref.shape[0]
    half = xs_ref.shape[1]
    d_exp = wd_ref.shape[0]

    @pl.when(jnp.logical_or(w == 0, exp_ref[w] != exp_ref[prev]))
    def _():
        wgu_bf_ref[...] = wgu_ref[...].astype(BF)
        wd_bf_ref[...] = wd_ref[...].astype(BF)

    lo, hi = lo_ref[w], hi_ref[w]
    live = hi > lo
    first = jnp.logical_or(w == 0, tile_ref[w] != tile_ref[prev])

    def run(rows, accumulate):
        n = rows.stop - rows.start
        x_lo, x_hi = _unpack_rows(xs_ref[rows, :])
        gu = (jnp.dot(x_lo.astype(BF), wgu_bf_ref[:half, :], preferred_element_type=F32)
              + jnp.dot(x_hi.astype(BF), wgu_bf_ref[half:, :], preferred_element_type=F32))
        row = lax.broadcasted_iota(jnp.int32, (n, d_exp), 0) + rows.start
        mine = (row >= lo) & (row < hi)
        act = jnp.where(mine, _silu(gu[:, :d_exp]) * gu[:, d_exp:], 0.0).astype(BF)
        chunk = min(DOWN_CHUNK, half)
        assert half % chunk == 0
        for j in range(half // chunk):
            c_lo = slice(j * chunk, (j + 1) * chunk)
            c_hi = slice(half + j * chunk, half + (j + 1) * chunk)
            y_lo = jnp.dot(act, wd_bf_ref[:, c_lo], preferred_element_type=F32)
            y_hi = jnp.dot(act, wd_bf_ref[:, c_hi], preferred_element_type=F32)
            if accumulate:
                y_lo = y_lo + acc_ref[rows, c_lo]
                y_hi = y_hi + acc_ref[rows, c_hi]
            acc_ref[rows, c_lo] = y_lo
            acc_ref[rows, c_hi] = y_hi
            ys_ref[rows, c_lo] = _pack_halves(y_lo, y_hi)

    rb = tm // 2
    top, bottom = slice(0, rb), slice(rb, tm)
    in_top, in_bottom = lo < rb, hi > rb
    cases = [
        (jnp.logical_and(in_top, in_bottom), slice(0, tm), None),
        (jnp.logical_and(in_top, jnp.logical_not(in_bottom)), top, bottom),
        (jnp.logical_and(jnp.logical_not(in_top), in_bottom), bottom, top),
    ]
    for cond, rows, skipped in cases:
        @pl.when(jnp.logical_and(live, jnp.logical_and(cond, first)))
        def _(rows=rows, skipped=skipped):
            if skipped is not None:
                acc_ref[skipped, :] = jnp.zeros((rb, acc_ref.shape[1]), F32)
            run(rows, False)

        @pl.when(jnp.logical_and(live, jnp.logical_and(cond, jnp.logical_not(first))))
        def _(rows=rows):
            run(rows, True)


def _experts(xs, items, w_gu, w_down, layer, tm):
    p, dw = xs.shape
    _, _, d, f2 = w_gu.shape
    n_items = items[0].shape[0]
    grid_spec = pltpu.PrefetchScalarGridSpec(
        num_scalar_prefetch=4,
        grid=(n_items,),
        in_specs=[
            pl.BlockSpec((tm, dw), lambda w, tile, exp, lo, hi: (tile[w], 0)),
            pl.BlockSpec((None, None, d, f2), lambda w, tile, exp, lo, hi: (layer, exp[w], 0, 0)),
            pl.BlockSpec((None, None, f2 // 2, d), lambda w, tile, exp, lo, hi: (layer, exp[w], 0, 0)),
        ],
        out_specs=pl.BlockSpec((tm, dw), lambda w, tile, exp, lo, hi: (tile[w], 0)),
        scratch_shapes=[pltpu.VMEM((tm, d), F32), pltpu.VMEM((d, f2), BF), pltpu.VMEM((f2 // 2, d), BF)],
    )
    return pl.pallas_call(
        _experts_kernel,
        grid_spec=grid_spec,
        out_shape=jax.ShapeDtypeStruct((p, dw), jnp.uint32),
        compiler_params=_cparams("arbitrary"),
        name="experts",
    )(*items, xs, w_gu, w_down)


def _expert_items(cnt, tm, n_tiles):
    n_exp = cnt.shape[0]
    n_items = n_tiles + n_exp - 1
    end = jnp.cumsum(cnt)
    start = end - cnt
    first_tile = start // tm
    n_e = jnp.where(cnt > 0, (end - 1) // tm - first_tile + 1, 0)
    item_end = jnp.cumsum(n_e)
    w = jnp.arange(n_items, dtype=jnp.int32)
    e = jnp.minimum(jnp.searchsorted(item_end, w, side="right"), n_exp - 1).astype(jnp.int32)
    valid = w < item_end[-1]
    tile = jnp.where(valid, first_tile[e] + (w - (item_end[e] - n_e[e])), n_tiles - 1).astype(jnp.int32)
    lo = jnp.where(valid, jnp.maximum(start[e], tile * tm) - tile * tm, 0).astype(jnp.int32)
    hi = jnp.where(valid, jnp.minimum(end[e], (tile + 1) * tm) - tile * tm, 0).astype(jnp.int32)
    e = jnp.where(valid, e, e[jnp.maximum(item_end[-1] - 1, 0)])
    return tile, e, lo, hi


def _shared_kernel(h_ref, wgu_ref, wd_ref, o_ref):
    d_exp = wd_ref.shape[0]
    gu = jnp.dot(h_ref[...], wgu_ref[...], preferred_element_type=F32)
    act = _silu(gu[:, :d_exp]) * gu[:, d_exp:]
    o_ref[...] = jnp.dot(act.astype(BF), wd_ref[...], preferred_element_type=F32)


def _shared(h, w_gu, w_down):
    t, d = h.shape
    f2 = w_gu.shape[1]
    tm = _pick(t, 512)
    return pl.pallas_call(
        _shared_kernel,
        grid=(t // tm,),
        in_specs=[
            pl.BlockSpec((tm, d), lambda i: (i, 0)),
            pl.BlockSpec((d, f2), lambda i: (0, 0)),
            pl.BlockSpec((f2 // 2, d), lambda i: (0, 0)),
        ],
        out_specs=pl.BlockSpec((tm, d), lambda i: (i, 0)),
        out_shape=jax.ShapeDtypeStruct((t, d), F32),
        compiler_params=_cparams("parallel"),
        name="shared",
    )(h, w_gu, w_down)


def _combine_kernel(pos_ref, sh_ref, wts_ref, x_ref, gate_ref, gy_ref, *rest, with_next):
    if with_next:
        gx_ref, shift_ref, scale_ref, ys_hbm, xo_ref, ho_ref, buf_ref, sem = rest
    else:
        ys_hbm, xo_ref, buf_ref, sem = rest
    i = pl.program_id(0)
    n = pl.num_programs(0)
    n_k, tm = pos_ref.shape[1], pos_ref.shape[2]
    half = buf_ref.shape[3]

    def row_copy(j, t, k, slot):
        return pltpu.make_async_copy(ys_hbm.at[pl.ds(pos_ref[j, k, t], 1)], buf_ref.at[slot, k, pl.ds(t, 1)],
                                     sem.at[slot])

    def gather(j, slot):
        def issue(t, c):
            for k in range(n_k):
                row_copy(j, t, k, slot).start(priority=k % 2)
            return c
        lax.fori_loop(0, tm, issue, 0, unroll=8)

    @pl.when(i == 0)
    def _():
        gather(0, 0)

    def drain(s):
        def body(t, c):
            for k in range(n_k):
                pltpu.make_async_copy(ys_hbm.at[pl.ds(0, 1)], buf_ref.at[s, 0, pl.ds(0, 1)], sem.at[s]).wait()
            return c
        lax.fori_loop(0, tm, body, 0, unroll=8)

    def step(slot):
        nxt_slot = 1 - slot
        drain(slot)

        acc_lo = sh_ref[:, :half]
        acc_hi = sh_ref[:, half:]
        for k in range(n_k):
            y_lo, y_hi = _unpack_rows(buf_ref[slot, k])
            wk = wts_ref[:, k:k + 1]
            acc_lo = acc_lo + wk * y_lo
            acc_hi = acc_hi + wk * y_hi
            for t in range(tm):
                row_copy(1, t, k, nxt_slot).start(priority=t % 2)

        @pl.when(i == n - 1)
        def _():
            drain(nxt_slot)

        f = jnp.concatenate([acc_lo, acc_hi], axis=1)
        xn = x_ref[...] + gate_ref[...] * (_rms(f) * gy_ref[...])
        xo_ref[...] = xn
        if with_next:
            hn = _rms(xn) * gx_ref[...]
            ho_ref[...] = (hn * (1 + scale_ref[...]) + shift_ref[...]).astype(ho_ref.dtype)

    for parity in range(2):
        @pl.when(i % 2 == parity)
        def _(parity=parity):
            step(parity)


def _combine(ys, pos, wts, sh, x, gate, gy, nxt=None):
    t, d = sh.shape
    n_k = pos.shape[0]
    tm = _pick(t, 64)
    nt = t // tm
    pos3 = pos.reshape(n_k, nt, tm).transpose(1, 0, 2)
    pos_pair = jnp.stack([pos3, jnp.concatenate([pos3[1:], pos3[-1:]], axis=0)], axis=1)
    row = pl.BlockSpec((tm, d), lambda i: (i, 0))
    vec = pl.BlockSpec((1, d), lambda i: (0, 0))
    v = lambda a: a.reshape(1, d)
    vecs = [v(gate), v(gy)] + ([v(a) for a in nxt] if nxt is not None else [])
    out_specs = [row, row] if nxt is not None else row
    out_shape = [jax.ShapeDtypeStruct((t, d), F32)] + ([jax.ShapeDtypeStruct((t, d), BF)] if nxt is not None else [])
    return pl.pallas_call(
        functools.partial(_combine_kernel, with_next=nxt is not None),
        grid=(nt,),
        in_specs=[
            pl.BlockSpec((None, 2, n_k, tm), lambda i: (i, 0, 0, 0), memory_space=pltpu.SMEM),
            row,
            pl.BlockSpec((tm, n_k), lambda i: (i, 0)),
            row,
        ] + [vec] * len(vecs) + [pl.BlockSpec(memory_space=pl.ANY)],
        out_specs=out_specs,
        out_shape=out_shape if nxt is not None else out_shape[0],
        scratch_shapes=[pltpu.VMEM((2, n_k, tm, d // 2), jnp.uint32), pltpu.SemaphoreType.DMA((2,))],
        compiler_params=_cparams("arbitrary"),
        name="combine",
    )(pos_pair, sh, wts, x, *vecs, ys)


def _moe_layer(h, x, gate, gy, nxt, layer, router_w, router_bias, exp_w_gu, exp_w_down, shared_w_gu, shared_w_down):
    t = h.shape[0]
    s_gu, s_down = shared_w_gu[layer], shared_w_down[layer]
    eidx, wts, rank, cnt = _router(h, router_w[layer], router_bias[layer])
    start = jnp.cumsum(cnt) - cnt
    n_exp = cnt.shape[0]
    is_e = eidx[None] == jnp.arange(n_exp, dtype=jnp.int32)[:, None, None]
    pos = jnp.sum(jnp.where(is_e, start[:, None, None], 0), axis=0) + rank
    tm = _pick(TOP_K * t, EXPERT_TILE)
    items = _expert_items(cnt, tm, TOP_K * t // tm)
    xs = _dispatch(h, pos)
    ys = _experts(xs, items, exp_w_gu, exp_w_down, layer, tm)
    sh = _shared(h, s_gu.astype(BF), s_down.astype(BF))
    return _combine(ys, pos, wts.T, sh, x, gate, gy, nxt)


def kernel(x, c, ctx, c_ctx, w_mod, b_mod, norm_g, hgrn_w_in, hgrn_lb_logits, hgrn_norm_g, hgrn_w_out,
           pool_w, pool_scale, router_w, router_bias, exp_w_gu, exp_w_down, shared_w_gu, shared_w_down):
    batch, seq, d = x.shape
    assert batch == 1 and c.shape[0] == 1
    depth = w_mod.shape[0]
    assert depth == 2 and w_mod.shape[2] == N_MOD * d
    n_heads = hgrn_w_in.shape[2] // 5 // HEAD_DIM
    x0 = x[0]

    cvecs = jnp.concatenate([c, c_ctx[None], jnp.zeros((6, d), F32)], axis=0)
    mods = _mods(cvecs, w_mod, b_mod)
    mod = lambda layer, who, j: mods[layer, who, j * d:(j + 1) * d]

    w_in = hgrn_w_in[0].astype(BF)
    hx = _normmod(x0, norm_g[0, 0], mod(0, 0, 0), mod(0, 0, 1))
    hc = _normmod(ctx[0], norm_g[0, 0], mod(0, 1, 0), mod(0, 1, 1))
    p = _mm_heads(hx, w_in)
    pc = _mm_heads(hc, w_in)
    n_lb = hgrn_lb_logits.shape[1]
    lbl = hgrn_lb_logits.astype(F32).reshape(2, n_lb, n_heads, HEAD_DIM).transpose(0, 2, 1, 3)
    zero_state = jnp.zeros((n_heads, HEAD_DIM, HEAD_DIM), F32)
    scan = functools.partial(_scan, n_heads=n_heads, layer=0)
    _, s_f = scan(pc, lbl[0], zero_state, reverse=False, z_off=3 * n_heads)
    _, s_b = scan(pc, lbl[1], zero_state, reverse=True, z_off=4 * n_heads)
    o_f, _ = scan(p, lbl[0], s_f, reverse=False, z_off=3 * n_heads)
    a, _ = scan(p, lbl[1], s_b, reverse=True, z_off=4 * n_heads, gate=(o_f, hgrn_norm_g[0]))
    y = _mm(a, hgrn_w_out[0].astype(BF), F32)
    x1, h = _resid(x0, y, mod(0, 0, 2), norm_g[0, 1], (norm_g[0, 2], mod(0, 0, 3), mod(0, 0, 4)))
    moe = functools.partial(_moe_layer, router_w=router_w, router_bias=router_bias, exp_w_gu=exp_w_gu,
                            exp_w_down=exp_w_down, shared_w_gu=shared_w_gu, shared_w_down=shared_w_down)
    x2, h = moe(h, x1, mod(0, 0, 5), norm_g[0, 3], (norm_g[1, 0], mod(1, 0, 0), mod(1, 0, 1)), 0)

    y = _pool(h, pool_w[0].astype(BF), pool_scale[0])
    x3, h = _resid(x2, y, mod(1, 0, 2), norm_g[1, 1], (norm_g[1, 2], mod(1, 0, 3), mod(1, 0, 4)))
    x4 = moe(h, x3, mod(1, 0, 5), norm_g[1, 3], None, 1)
    return x4[None]
```
